```python
import jax, jax.numpy as jnp
from jax import lax
import numpy as np


D_MODEL = 4096
BATCH = 4
SEQ = 2048
DEPTH = 1

CHUNK = 64
N_META = 16
POOL_WINDOWS = (2, 4, 8, 16)
POOL_WIDTH = D_MODEL // 4
POOL_GROUP = POOL_WIDTH // len(POOL_WINDOWS)
MLSTM_WIDTH = D_MODEL - POOL_WIDTH
MLSTM_HEADS = 6
MLSTM_V_DIM = MLSTM_WIDTH // MLSTM_HEADS
MLSTM_QK_DIM = MLSTM_V_DIM // 2
GATE_SOFTCAP = 15.0
N_EXPERTS = 32
TOP_K = 4
D_FF = D_MODEL // 2
SWIGLU_ALPHA = 1.702
SWIGLU_LIMIT = 7.0
MOE_BLOCK = 128
EPS = 1e-6

kernel_name = 'hybrid_pool_mlstm_moe_block'


def rms_norm(x, g):
    xf = x.astype(jnp.float32)
    y = xf * lax.rsqrt(jnp.mean(xf * xf, axis=-1, keepdims=True) + EPS)
    return (y * g.astype(jnp.float32)).astype(x.dtype)


def soft_cap(a):
    return GATE_SOFTCAP * jnp.tanh(a / GATE_SOFTCAP)


def multiscale_pool(u, w_pool_mix, pool_scale):
    B, L, _ = u.shape
    G = len(POOL_WINDOWS)
    uf = u.astype(jnp.float32).reshape(B, L, G, POOL_GROUP)
    cs = jnp.pad(jnp.cumsum(uf, axis=1), ((0, 0), (1, 0), (0, 0), (0, 0)))
    t = np.arange(L)
    diffs = []
    for g, w in enumerate(POOL_WINDOWS):
        lo = np.maximum(t + 1 - w, 0)
        cnt = jnp.asarray(np.minimum(t + 1, w), jnp.float32)[None, :, None]
        mean = (cs[:, t + 1, g] - cs[:, lo, g]) / cnt
        diffs.append(mean - uf[:, :, g])
    d = jnp.stack(diffs, axis=2)
    y = jnp.einsum('blgc,gce->blge', d, w_pool_mix.astype(jnp.float32))
    y = y * pool_scale.astype(jnp.float32).reshape(G, POOL_GROUP)
    return y.reshape(B, L, POOL_WIDTH).astype(u.dtype)


def mlstm_chunkwise(q, k, v, logi, logf):
    B, H, Lp, Dk = q.shape
    Dv = v.shape[-1]
    nc = Lp // CHUNK

    def to_chunks(a):
        a = a.reshape(B, H, nc, CHUNK, *a.shape[3:])
        return jnp.moveaxis(a, 2, 0)

    causal = jnp.tril(jnp.ones((CHUNK, CHUNK), bool))

    def step(carry, xs):
        C, n, m = carry
        qc, kc, vc, ic, fc = xs
        b = jnp.cumsum(fc, axis=-1)
        g = b[..., -1]
        dlog = b[..., :, None] - b[..., None, :] + ic[..., None, :]
        dlog = jnp.where(causal, dlog, -jnp.inf)
        inter_log = b + m[..., None]
        m_out = jnp.maximum(inter_log, jnp.max(dlog, axis=-1))
        wts = jnp.exp(dlog - m_out[..., None])
        inter_w = jnp.exp(inter_log - m_out)
        s = jnp.einsum('bhsd,bhrd->bhsr', qc, kc) * wts
        num = jnp.einsum('bhsr,bhre->bhse', s, vc) + inter_w[..., None] * jnp.einsum('bhsd,bhde->bhse', qc, C)
        den = jnp.sum(s, axis=-1) + inter_w * jnp.einsum('bhsd,bhd->bhs', qc, n)
        h = num / jnp.maximum(jnp.abs(den), jnp.exp(-m_out))[..., None]
        a = g[..., None] - b + ic
        m_new = jnp.maximum(g + m, jnp.max(a, axis=-1))
        wk = kc * jnp.exp(a - m_new[..., None])[..., None]
        decay = jnp.exp(g + m - m_new)
        C_new = decay[..., None, None] * C + jnp.einsum('bhrd,bhre->bhde', wk, vc)
        n_new = decay[..., None] * n + jnp.sum(wk, axis=2)
        return (C_new, n_new, m_new), h

    init = (jnp.zeros((B, H, Dk, Dv), jnp.float32), jnp.zeros((B, H, Dk), jnp.float32), jnp.zeros((B, H), jnp.float32))
    _, hs = lax.scan(step, init, (to_chunks(q), to_chunks(k), to_chunks(v), to_chunks(logi), to_chunks(logf)))
    return jnp.moveaxis(hs, 0, 2).reshape(B, H, Lp, Dv)


def hybrid_mixer(h, w_in, b_igate, b_fgate, w_pool_mix, pool_scale, w_out):
    B, L, _ = h.shape
    H, Dk, Dv = MLSTM_HEADS, MLSTM_QK_DIM, MLSTM_V_DIM
    sizes = [POOL_WIDTH, H * Dk, H * Dk, MLSTM_WIDTH, MLSTM_WIDTH, H, H]
    cuts = [int(c) for c in np.cumsum(sizes)[:-1]]
    u_pool, q, k, v, o, gi, gf = jnp.split(h @ w_in, cuts, axis=-1)

    pool_out = multiscale_pool(u_pool, w_pool_mix, pool_scale)

    pad = (-L) % CHUNK

    def heads(a, d):
        a = a.astype(jnp.float32).reshape(B, L, H, d).transpose(0, 2, 1, 3)
        return jnp.pad(a, ((0, 0), (0, 0), (pad, 0), (0, 0)))

    def gate(a, bias):
        a = soft_cap(a.astype(jnp.float32) + bias.astype(jnp.float32)).transpose(0, 2, 1)
        return jnp.pad(a, ((0, 0), (0, 0), (pad, 0)))

    qh = heads(q, Dk) * (Dk ** -0.5)
    kh = heads(k, Dk)
    vh = heads(v, Dv)
    logi = gate(gi, b_igate)
    logf = jax.nn.log_sigmoid(gate(gf, b_fgate))
    ht = mlstm_chunkwise(qh, kh, vh, logi, logf)[:, :, pad:]
    ht = ht.transpose(0, 2, 1, 3).reshape(B, L, MLSTM_WIDTH)
    mlstm_out = (jax.nn.sigmoid(o.astype(jnp.float32)) * ht).astype(h.dtype)

    return jnp.concatenate([pool_out, mlstm_out], axis=-1) @ w_out


def clamped_swiglu(gu):
    gate, lin = jnp.split(gu, 2, axis=-1)
    gate = jnp.minimum(gate, SWIGLU_LIMIT)
    lin = jnp.clip(lin, -SWIGLU_LIMIT, SWIGLU_LIMIT)
    return gate * jax.nn.sigmoid(SWIGLU_ALPHA * gate) * (lin + 1.0)


def moe_ffn(h, w_router, b_router, w_gu, b_gu, w_down, b_down, layer):
    B, L, D = h.shape
    T = B * L
    xt = h.reshape(T, D)
    logits = (xt @ w_router + b_router).astype(jnp.float32)
    top_v, top_e = lax.top_k(logits, TOP_K)
    gates = jax.nn.softmax(top_v, axis=-1)
    A = T * TOP_K
    e_flat = top_e.reshape(A)
    order = jnp.argsort(e_flat)
    e_sorted = e_flat[order]
    tok_sorted = (order // TOP_K).astype(jnp.int32)
    gate_sorted = gates.reshape(A)[order]
    counts = jnp.bincount(e_flat, length=N_EXPERTS)
    padded = (counts + MOE_BLOCK - 1) // MOE_BLOCK * MOE_BLOCK
    pad_end = jnp.cumsum(padded)
    start = jnp.cumsum(counts) - counts
    dest = (pad_end - padded)[e_sorted] + jnp.arange(A) - start[e_sorted]
    n_blocks = -(-(A + N_EXPERTS * (MOE_BLOCK - 1)) // MOE_BLOCK)
    R = n_blocks * MOE_BLOCK
    row_tok = jnp.full((R,), T, jnp.int32).at[dest].set(tok_sorted)
    row_gate = jnp.zeros((R,), jnp.float32).at[dest].set(gate_sorted)
    block_e = jnp.minimum(jnp.searchsorted(pad_end, jnp.arange(n_blocks) * MOE_BLOCK, side='right'), N_EXPERTS - 1)
    x_pad = jnp.concatenate([xt, jnp.zeros((1, D), xt.dtype)], axis=0)

    def body(acc, blk):
        tok, gate, e = blk
        gu = x_pad[tok] @ w_gu[layer, e] + b_gu[layer, e]
        y = clamped_swiglu(gu) @ w_down[layer, e] + b_down[layer, e]
        return acc.at[tok].add(y.astype(jnp.float32) * gate[:, None]), None

    acc, _ = lax.scan(body, jnp.zeros((T + 1, D), jnp.float32),
                      (row_tok.reshape(n_blocks, MOE_BLOCK), row_gate.reshape(n_blocks, MOE_BLOCK), block_e))
    return acc[:T].astype(h.dtype).reshape(B, L, D)


def setup_inputs(seed: int = 0) -> dict:
    key = jax.random.key(seed)
    ks = jax.random.split(key, 20)
    D, H, F, E = D_MODEL, MLSTM_HEADS, D_FF, N_EXPERTS
    n_in = POOL_WIDTH + 2 * H * MLSTM_QK_DIM + 2 * MLSTM_WIDTH + 2 * H
    G = len(POOL_WINDOWS)

    def nrm(k, shape, scale):
        return jax.random.normal(k, shape, jnp.float32) * scale

    return {
        'x': nrm(ks[0], (BATCH, SEQ, D), 1.0),
        'meta_tokens': nrm(ks[1], (N_META, D), 1.0),
        'norm1_g': 1.0 + nrm(ks[2], (DEPTH, D), 0.02),
        'w_in': nrm(ks[3], (DEPTH, D, n_in), D ** -0.5),
        'b_igate': nrm(ks[4], (DEPTH, H), 0.1),
        'b_fgate': jnp.linspace(3.0, 6.0, H, dtype=jnp.float32)[None, :] + nrm(ks[5], (DEPTH, H), 0.1),
        'w_pool_mix': nrm(ks[6], (DEPTH, G, POOL_GROUP, POOL_GROUP), POOL_GROUP ** -0.5),
        'pool_scale': 1.0 + nrm(ks[7], (DEPTH, POOL_WIDTH), 0.02),
        'w_out': nrm(ks[8], (DEPTH, POOL_WIDTH + MLSTM_WIDTH, D), (POOL_WIDTH + MLSTM_WIDTH) ** -0.5),
        'norm2_g': 1.0 + nrm(ks[9], (DEPTH, D), 0.02),
        'w_router': nrm(ks[10], (DEPTH, D, E), D ** -0.5),
        'b_router': nrm(ks[11], (DEPTH, E), 0.01),
        'w_gu': nrm(ks[12], (DEPTH, E, D, 2 * F), D ** -0.5),
        'b_gu': nrm(ks[13], (DEPTH, E, 2 * F), 0.01),
        'w_down': nrm(ks[14], (DEPTH, E, F, D), F ** -0.5),
        'b_down': nrm(ks[15], (DEPTH, E, D), 0.01),
        'norm_f_g': 1.0 + nrm(ks[16], (D,), 0.02),
    }


def reference(x, meta_tokens, norm1_g, w_in, b_igate, b_fgate, w_pool_mix, pool_scale, w_out,
              norm2_g, w_router, b_router, w_gu, b_gu, w_down, b_down, norm_f_g):
    B = x.shape[0]
    meta = jnp.broadcast_to(meta_tokens[None].astype(x.dtype), (B, N_META, D_MODEL))
    h = jnp.concatenate([meta, x], axis=1)
    for l in range(DEPTH):
        h = h + hybrid_mixer(rms_norm(h, norm1_g[l]), w_in[l], b_igate[l], b_fgate[l],
                             w_pool_mix[l], pool_scale[l], w_out[l])
        h = h + moe_ffn(rms_norm(h, norm2_g[l]), w_router[l], b_router[l], w_gu, b_gu, w_down, b_down, l)
    return rms_norm(h, norm_f_g)[:, N_META:]
```

```python
import functools

import jax
import jax.numpy as jnp
from jax import lax
from jax.experimental import pallas as pl
from jax.experimental.pallas import tpu as pltpu

D_MODEL = 4096
N_META = 16
POOL_WINDOWS = (2, 4, 8, 16)
POOL_WIDTH = D_MODEL // 4
POOL_GROUP = POOL_WIDTH // len(POOL_WINDOWS)
MLSTM_WIDTH = D_MODEL - POOL_WIDTH
MLSTM_HEADS = 6
MLSTM_V_DIM = MLSTM_WIDTH // MLSTM_HEADS
MLSTM_QK_DIM = MLSTM_V_DIM // 2
GATE_SOFTCAP = 15.0
N_EXPERTS = 32
TOP_K = 4
D_FF = D_MODEL // 2
SWIGLU_ALPHA = 1.702
SWIGLU_LIMIT = 7.0
MOE_BLOCK = 128
EPS = 1e-6

LANES = 128
SEQ_CHUNK = 256
PROJ_COLS = POOL_WIDTH + 2 * MLSTM_HEADS * MLSTM_QK_DIM + 2 * MLSTM_WIDTH
VMEM_LIMIT = 56 * 1024 * 1024

F32 = jnp.float32
BF16 = jnp.bfloat16


def _params(sem, vmem=VMEM_LIMIT):
    return pltpu.CompilerParams(dimension_semantics=sem, vmem_limit_bytes=vmem)


def _split3(a):
    hi = a.astype(BF16)
    r1 = a - hi.astype(F32)
    mid = r1.astype(BF16)
    lo = (r1 - mid.astype(F32)).astype(BF16)
    return hi, mid, lo


def _dot(a, b):
    return jnp.dot(a, b, preferred_element_type=F32)


def _dot3(a, w_hi, w_lo):
    a_hi = a.astype(BF16)
    a_lo = (a - a_hi.astype(F32)).astype(BF16)
    return _dot(a_hi, w_hi) + (_dot(a_hi, w_lo) + _dot(a_lo, w_hi))


def _rms(x, g):
    return x * lax.rsqrt(jnp.mean(x * x, axis=-1, keepdims=True) + EPS) * g


def _norm1_kernel(h_ref, g_ref, wg_hi_ref, wg_lo_ref, bg_ref, n_ref, gate_ref):
    y = _rms(h_ref[...], g_ref[...])
    n_ref[...] = y.astype(BF16)
    gate_ref[...] = _dot3(y, wg_hi_ref[...], wg_lo_ref[...]) + bg_ref[...]


def _norm1(h, g, wg_hi, wg_lo, bg, bm):
    T = h.shape[0]
    return pl.pallas_call(
        _norm1_kernel,
        grid=(T // bm,),
        in_specs=[
            pl.BlockSpec((bm, D_MODEL), lambda i: (i, 0)),
            pl.BlockSpec((1, D_MODEL), lambda i: (0, 0)),
            pl.BlockSpec((D_MODEL, LANES), lambda i: (0, 0)),
            pl.BlockSpec((D_MODEL, LANES), lambda i: (0, 0)),
            pl.BlockSpec((1, LANES), lambda i: (0, 0)),
        ],
        out_specs=[
            pl.BlockSpec((bm, D_MODEL), lambda i: (i, 0)),
            pl.BlockSpec((bm, LANES), lambda i: (i, 0)),
        ],
        out_shape=[
            jax.ShapeDtypeStruct((T, D_MODEL), BF16),
            jax.ShapeDtypeStruct((T, LANES), F32),
        ],
        compiler_params=_params(("parallel",)),
        name="norm1_gates",
    )(h, g, wg_hi, wg_lo, bg)


def _matmul_kernel(x_ref, w_ref, o_ref):
    o_ref[...] = _dot(x_ref[...], w_ref[...]).astype(o_ref.dtype)


def _matmul(x, w, bm, bn, out_dtype, name):
    M, K = x.shape
    N = w.shape[1]
    return pl.pallas_call(
        _matmul_kernel,
        grid=(N // bn, M // bm),
        in_specs=[
            pl.BlockSpec((bm, K), lambda j, i: (i, 0)),
            pl.BlockSpec((K, bn), lambda j, i: (0, j)),
        ],
        out_specs=pl.BlockSpec((bm, bn), lambda j, i: (i, j)),
        out_shape=jax.ShapeDtypeStruct((M, N), out_dtype),
        compiler_params=_params(("parallel", "parallel")),
        name=name,
    )(x, w)


def _outproj_kernel(p_ref, m_ref, wp_ref, wm_ref, h_ref, o_ref):
    o_ref[...] = h_ref[...] + (_dot(p_ref[...], wp_ref[...]) + _dot(m_ref[...], wm_ref[...]))


def _outproj(p, m, wp, wm, h, bm, bn):
    M = p.shape[0]
    N = wp.shape[1]
    return pl.pallas_call(
        _outproj_kernel,
        grid=(N // bn, M // bm),
        in_specs=[
            pl.BlockSpec((bm, POOL_WIDTH), lambda j, i: (i, 0)),
            pl.BlockSpec((bm, MLSTM_WIDTH), lambda j, i: (i, 0)),
            pl.BlockSpec((POOL_WIDTH, bn), lambda j, i: (0, j)),
            pl.BlockSpec((MLSTM_WIDTH, bn), lambda j, i: (0, j)),
            pl.BlockSpec((bm, bn), lambda j, i: (i, j)),
        ],
        out_specs=pl.BlockSpec((bm, bn), lambda j, i: (i, j)),
        out_shape=jax.ShapeDtypeStruct((M, N), F32),
        compiler_params=_params(("parallel", "parallel")),
        name="out_proj",
    )(p, m, wp, wm, h)


def _pool_kernel(u_ref, w_ref, s_ref, o_ref, carry_ref):
    c = pl.program_id(1)

    @pl.when(c == 0)
    def _():
        carry_ref[...] = jnp.zeros_like(carry_ref)

    u = u_ref[0]
    ext = jnp.concatenate([carry_ref[...], u], axis=0)
    carry_ref[...] = u[SEQ_CHUNK - 16:, :]
    pos = c * SEQ_CHUNK + lax.broadcasted_iota(jnp.int32, (SEQ_CHUNK, 1), 0)
    for g, win in enumerate(POOL_WINDOWS):
        cols = slice(g * POOL_GROUP, (g + 1) * POOL_GROUP)
        s = ext[:, cols]
        span = 1
        while span < win:
            s = s + pltpu.roll(s, span, axis=0)
            span *= 2
        cnt = jnp.minimum(pos + 1, win).astype(F32)
        d = s[16:, :] / cnt - u[:, cols]
        y = _dot(d.astype(BF16), w_ref[g]) * s_ref[:, cols]
        o_ref[0, :, cols] = y.astype(o_ref.dtype)


def _pool(proj3, w_mix, scale):
    B, L, _ = proj3.shape
    nc = pl.cdiv(L, SEQ_CHUNK)
    return pl.pallas_call(
        _pool_kernel,
        grid=(B, nc),
        in_specs=[
            pl.BlockSpec((1, SEQ_CHUNK, POOL_WIDTH), lambda b, c: (b, c, 0)),
            pl.BlockSpec((len(POOL_WINDOWS), POOL_GROUP, POOL_GROUP), lambda b, c: (0, 0, 0)),
            pl.BlockSpec((1, POOL_WIDTH), lambda b, c: (0, 0)),
        ],
        out_specs=pl.BlockSpec((1, SEQ_CHUNK, POOL_WIDTH), lambda b, c: (b, c, 0)),
        out_shape=jax.ShapeDtypeStruct((B, L, POOL_WIDTH), BF16),
        scratch_shapes=[pltpu.VMEM((16, POOL_WIDTH), F32)],
        compiler_params=_params(("parallel", "arbitrary")),
        name="pool_mixer",
    )(proj3, w_mix, scale)


def _soft_cap(a):
    return GATE_SOFTCAP * jnp.tanh(a / GATE_SOFTCAP)


def _log_sigmoid(a):
    return jnp.minimum(a, 0.0) - jnp.log1p(jnp.exp(-jnp.abs(a)))


def _mlstm_kernel(seq_len, q_ref, k_ref, v_ref, o_ref, gate_ref, out_ref, c_ref, n_ref, m_ref):
    h = pl.program_id(1)
    c = pl.program_id(2)
    Lc = SEQ_CHUNK

    @pl.when(c == 0)
    def _():
        c_ref[...] = jnp.zeros_like(c_ref)
        n_ref[...] = jnp.zeros_like(n_ref)
        m_ref[...] = jnp.zeros_like(m_ref)

    row = lax.broadcasted_iota(jnp.int32, (Lc, 1), 0)
    col = lax.broadcasted_iota(jnp.int32, (1, Lc), 1)
    ok_col = (c * Lc + row) < seq_len
    ok_row = (c * Lc + col) < seq_len

    q = jnp.where(ok_col, q_ref[0], 0.0) * (MLSTM_QK_DIM ** -0.5)
    k = jnp.where(ok_col, k_ref[0], 0.0)
    v = jnp.where(ok_col, v_ref[0], 0.0)

    gc = gate_ref[0]
    gt = gc.T
    lane = lax.broadcasted_iota(jnp.int32, (1, LANES), 1)
    sub = lax.broadcasted_iota(jnp.int32, (LANES, 1), 0)
    gi_c = jnp.sum(jnp.where(lane == h, gc, 0.0), axis=1, keepdims=True)
    gf_c = jnp.sum(jnp.where(lane == h + MLSTM_HEADS, gc, 0.0), axis=1, keepdims=True)
    gi_r = jnp.sum(jnp.where(sub == h, gt, 0.0), axis=0, keepdims=True)
    gf_r = jnp.sum(jnp.where(sub == h + MLSTM_HEADS, gt, 0.0), axis=0, keepdims=True)
    i_c = jnp.where(ok_col, _soft_cap(gi_c), 0.0)
    f_c = jnp.where(ok_col, _log_sigmoid(_soft_cap(gf_c)), 0.0)
    i_r = jnp.where(ok_row, _soft_cap(gi_r), 0.0)
    f_r = jnp.where(ok_row, _log_sigmoid(_soft_cap(gf_r)), 0.0)

    causal = col <= row
    tri = causal.astype(BF16)
    fb_c = jnp.broadcast_to(f_c, (Lc, LANES))
    b_c = sum(_dot(tri, p) for p in _split3(fb_c))[:, 0:1]
    fb_r = jnp.broadcast_to(f_r, (8, Lc))
    tri_t = (row <= col).astype(BF16)
    b_r8 = sum(_dot(p, tri_t) for p in _split3(fb_r))
    b_r = b_r8[0:1, :]
    g_tot = b_r8[0:1, Lc - 1:Lc]

    m_prev = m_ref[...]
    dlog = jnp.where(causal, b_c - b_r + i_r, -jnp.inf)
    inter_log = b_c + m_prev
    m_out = jnp.maximum(inter_log, jnp.max(dlog, axis=1, keepdims=True))
    wts = jnp.exp(dlog - m_out)
    inter_w = jnp.exp(inter_log - m_out)

    qb = q.astype(BF16)
    kb = k.astype(BF16)
    vb = v.astype(BF16)
    s = lax.dot_general(qb, kb, (((1,), (1,)), ((), ())), preferred_element_type=F32) * wts
    num = _dot(s.astype(BF16), vb) + inter_w * _dot(qb, c_ref[...].astype(BF16))
    den = jnp.sum(s, axis=1, keepdims=True) + inter_w * jnp.sum(q * n_ref[...], axis=1, keepdims=True)
    hh = num / jnp.maximum(jnp.abs(den), jnp.exp(-m_out))
    out_ref[0] = (jax.nn.sigmoid(o_ref[0]) * hh).astype(out_ref.dtype)

    a_c = g_tot - b_c + i_c
    m_new = jnp.maximum(g_tot + m_prev, jnp.max(a_c, axis=0, keepdims=True))
    wk = k * jnp.exp(a_c - m_new)
    decay = jnp.exp(g_tot + m_prev - m_new)
    c_ref[...] = decay * c_ref[...] + lax.dot_general(
        wk.astype(BF16), vb, (((0,), (0,)), ((), ())), preferred_element_type=F32)
    n_ref[...] = decay * n_ref[...] + jnp.sum(wk, axis=0, keepdims=True)
    m_ref[...] = m_new


def _mlstm(proj3, gates3):
    B, L, _ = proj3.shape
    nc = pl.cdiv(L, SEQ_CHUNK)
    dk, dv, H = MLSTM_QK_DIM, MLSTM_V_DIM, MLSTM_HEADS
    q0 = POOL_WIDTH // dk
    k0 = q0 + H
    v0 = (POOL_WIDTH + 2 * H * dk) // dv
    o0 = v0 + H
    return pl.pallas_call(
        functools.partial(_mlstm_kernel, L),
        grid=(B, H, nc),
        in_specs=[
            pl.BlockSpec((1, SEQ_CHUNK, dk), lambda b, h, c: (b, c, q0 + h)),
            pl.BlockSpec((1, SEQ_CHUNK, dk), lambda b, h, c: (b, c, k0 + h)),
            pl.BlockSpec((1, SEQ_CHUNK, dv), lambda b, h, c: (b, c, v0 + h)),
            pl.BlockSpec((1, SEQ_CHUNK, dv), lambda b, h, c: (b, c, o0 + h)),
            pl.BlockSpec((1, SEQ_CHUNK, LANES), lambda b, h, c: (b, c, 0)),
        ],
        out_specs=pl.BlockSpec((1, SEQ_CHUNK, dv), lambda b, h, c: (b, c, h)),
        out_shape=jax.ShapeDtypeStruct((B, L, MLSTM_WIDTH), BF16),
        scratch_shapes=[
            pltpu.VMEM((dk, dv), F32),
            pltpu.VMEM((1, dk), F32),
            pltpu.VMEM((1, 1), F32),
        ],
        compiler_params=_params(("parallel", "parallel", "arbitrary")),
        name="mlstm",
    )(proj3, proj3, proj3, proj3, gates3)


def _router_kernel(h_ref, g_ref, wr_hi_ref, wr_lo_ref, br_ref, n_ref, e_ref, p_ref):
    y = _rms(h_ref[...], g_ref[...])
    n_ref[...] = y
    logits = _dot3(y, wr_hi_ref[...], wr_lo_ref[...]) + br_ref[...]
    lane = lax.broadcasted_iota(jnp.int32, logits.shape, 1).astype(F32)
    l = jnp.where(lane < N_EXPERTS, logits, -jnp.inf)
    vals, idxs = [], []
    for _ in range(TOP_K):
        m = jnp.max(l, axis=1, keepdims=True)
        idx = jnp.min(jnp.where(l == m, lane, float(LANES)), axis=1, keepdims=True)
        vals.append(m)
        idxs.append(idx)
        l = jnp.where(lane == idx, -jnp.inf, l)
    ex = [jnp.exp(v - vals[0]) for v in vals]
    tot = ex[0] + ex[1] + ex[2] + ex[3]
    e_out = jnp.zeros(logits.shape, F32)
    p_out = jnp.zeros(logits.shape, F32)
    for kk in range(TOP_K):
        e_out = jnp.where(lane == kk, idxs[kk], e_out)
        p_out = jnp.where(lane == kk, ex[kk] / tot, p_out)
    e_ref[...] = e_out.astype(jnp.int32)
    p_ref[...] = p_out


def _router(h, g, wr_hi, wr_lo, br, bm):
    T = h.shape[0]
    return pl.pallas_call(
        _router_kernel,
        grid=(T // bm,),
        in_specs=[
            pl.BlockSpec((bm, D_MODEL), lambda i: (i, 0)),
            pl.BlockSpec((1, D_MODEL), lambda i: (0, 0)),
            pl.BlockSpec((D_MODEL, LANES), lambda i: (0, 0)),
            pl.BlockSpec((D_MODEL, LANES), lambda i: (0, 0)),
            pl.BlockSpec((1, LANES), lambda i: (0, 0)),
        ],
        out_specs=[
            pl.BlockSpec((bm, D_MODEL), lambda i: (i, 0)),
            pl.BlockSpec((bm, LANES), lambda i: (i, 0)),
            pl.BlockSpec((bm, LANES), lambda i: (i, 0)),
        ],
        out_shape=[
            jax.ShapeDtypeStruct((T, D_MODEL), F32),
            jax.ShapeDtypeStruct((T, LANES), jnp.int32),
            jax.ShapeDtypeStruct((T, LANES), F32),
        ],
        compiler_params=_params(("parallel",)),
        name="norm2_router",
    )(h, g, wr_hi, wr_lo, br)


def _row_gather_start(src_hbm, idx_ref, base, n_rows, buf, slot, sem):
    def body(r, carry):
        tok = idx_ref[base + r]
        pltpu.make_async_copy(src_hbm.at[pl.ds(tok, 1), :], buf.at[slot, pl.ds(r, 1), :], sem.at[slot]).start()
        return carry
    lax.fori_loop(0, n_rows, body, 0)


def _row_gather_wait(src_hbm, n_rows, buf, slot, sem):
    pltpu.make_async_copy(src_hbm.at[pl.ds(0, n_rows), :], buf.at[slot], sem.at[slot]).wait()


def _dispatch_kernel(tok_ref, x_hbm, o_ref, buf, sem):
    i = pl.program_id(0)
    nb = pl.num_programs(0)
    slot = lax.rem(i, 2)

    @pl.when(i == 0)
    def _():
        _row_gather_start(x_hbm, tok_ref, 0, MOE_BLOCK, buf, 0, sem)

    @pl.when(i + 1 < nb)
    def _():
        _row_gather_start(x_hbm, tok_ref, (i + 1) * MOE_BLOCK, MOE_BLOCK, buf, 1 - slot, sem)

    _row_gather_wait(x_hbm, MOE_BLOCK, buf, slot, sem)
    o_ref[...] = buf[slot].astype(o_ref.dtype)


def _dispatch(row_tok, x):
    R = row_tok.shape[0]
    nb = R // MOE_BLOCK
    grid_spec = pltpu.PrefetchScalarGridSpec(
        num_scalar_prefetch=1,
        grid=(nb,),
        in_specs=[pl.BlockSpec(memory_space=pl.ANY)],
        out_specs=pl.BlockSpec((MOE_BLOCK, D_MODEL), lambda i, tok: (i, 0)),
        scratch_shapes=[
            pltpu.VMEM((2, MOE_BLOCK, D_MODEL), F32),
            pltpu.SemaphoreType.DMA((2,)),
        ],
    )
    return pl.pallas_call(
        _dispatch_kernel,
        grid_spec=grid_spec,
        out_shape=jax.ShapeDtypeStruct((R, D_MODEL), BF16),
        compiler_params=_params(("arbitrary",)),
        name="moe_dispatch",
    )(row_tok, x)


def _expert_changed(be_ref, i):
    prev = be_ref[jnp.maximum(i - 1, 0)]
    return jnp.logical_or(i == 0, be_ref[i] != prev)


def _gateup_kernel(be_ref, nb_ref, x_ref, wg_ref, wl_ref, bg_ref, bl_ref, o_ref, wg_s, wl_s):
    i = pl.program_id(1)

    @pl.when(_expert_changed(be_ref, i))
    def _():
        wg_s[...] = wg_ref[0].astype(BF16)
        wl_s[...] = wl_ref[0].astype(BF16)

    @pl.when(i < nb_ref[0])
    def _():
        x = x_ref[...]
        gate = _dot(x, wg_s[...]) + bg_ref[0]
        lin = _dot(x, wl_s[...]) + bl_ref[0]
        gate = jnp.minimum(gate, SWIGLU_LIMIT)
        lin = jnp.clip(lin, -SWIGLU_LIMIT, SWIGLU_LIMIT)
        o_ref[...] = (gate * jax.nn.sigmoid(SWIGLU_ALPHA * gate) * (lin + 1.0)).astype(o_ref.dtype)

    @pl.when(i >= nb_ref[0])
    def _():
        o_ref[...] = jnp.zeros_like(o_ref)


def _gateup(block_e, nb_used, xg, w_gu, b_gu, tf):
    R = xg.shape[0]
    nb = R // MOE_BLOCK
    nj = D_FF // tf
    grid_spec = pltpu.PrefetchScalarGridSpec(
        num_scalar_prefetch=2,
        grid=(nj, nb),
        in_specs=[
            pl.BlockSpec((MOE_BLOCK, D_MODEL), lambda j, i, be, n: (i, 0)),
            pl.BlockSpec((1, D_MODEL, tf), lambda j, i, be, n: (be[i], 0, j)),
            pl.BlockSpec((1, D_MODEL, tf), lambda j, i, be, n: (be[i], 0, nj + j)),
            pl.BlockSpec((1, 1, tf), lambda j, i, be, n: (be[i], 0, j)),
            pl.BlockSpec((1, 1, tf), lambda j, i, be, n: (be[i], 0, nj + j)),
        ],
        out_specs=pl.BlockSpec((MOE_BLOCK, tf), lambda j, i, be, n: (i, j)),
        scratch_shapes=[pltpu.VMEM((D_MODEL, tf), BF16), pltpu.VMEM((D_MODEL, tf), BF16)],
    )
    return pl.pallas_call(
        _gateup_kernel,
        grid_spec=grid_spec,
        out_shape=jax.ShapeDtypeStruct((R, D_FF), BF16),
        compiler_params=_params(("arbitrary", "arbitrary")),
        name="moe_gate_up",
    )(block_e, nb_used, xg, w_gu, w_gu, b_gu, b_gu)


def _down_kernel(be_ref, nb_ref, x_ref, w_ref, b_ref, o_ref, w_s):
    i = pl.program_id(1)

    @pl.when(_expert_changed(be_ref, i))
    def _():
        w_s[...] = w_ref[0].astype(BF16)

    @pl.when(i < nb_ref[0])
    def _():
        o_ref[...] = _dot(x_ref[...], w_s[...]) + b_ref[0]

    @pl.when(i >= nb_ref[0])
    def _():
        o_ref[...] = jnp.zeros_like(o_ref)


def _down(block_e, nb_used, hmid, w_down, b_down, tn):
    R = hmid.shape[0]
    nb = R // MOE_BLOCK
    grid_spec = pltpu.PrefetchScalarGridSpec(
        num_scalar_prefetch=2,
        grid=(D_MODEL // tn, nb),
        in_specs=[
            pl.BlockSpec((MOE_BLOCK, D_FF), lambda j, i, be, n: (i, 0)),
            pl.BlockSpec((1, D_FF, tn), lambda j, i, be, n: (be[i], 0, j)),
            pl.BlockSpec((1, 1, tn), lambda j, i, be, n: (be[i], 0, j)),
        ],
        out_specs=pl.BlockSpec((MOE_BLOCK, tn), lambda j, i, be, n: (i, j)),
        scratch_shapes=[pltpu.VMEM((D_FF, tn), BF16)],
    )
    return pl.pallas_call(
        _down_kernel,
        grid_spec=grid_spec,
        out_shape=jax.ShapeDtypeStruct((R, D_MODEL), F32),
        compiler_params=_params(("arbitrary", "arbitrary")),
        name="moe_down",
    )(block_e, nb_used, hmid, w_down, b_down)


def _combine_kernel(tb, dest_ref, y_hbm, h_ref, p_ref, g_ref, o_ref, buf, sem):
    i = pl.program_id(0)
    nb = pl.num_programs(0)
    slot = lax.rem(i, 2)
    rows = tb * TOP_K

    @pl.when(i == 0)
    def _():
        _row_gather_start(y_hbm, dest_ref, 0, rows, buf, 0, sem)

    @pl.when(i + 1 < nb)
    def _():
        _row_gather_start(y_hbm, dest_ref, (i + 1) * rows, rows, buf, 1 - slot, sem)

    _row_gather_wait(y_hbm, rows, buf, slot, sem)
    acc = h_ref[...]
    p = p_ref[...]
    for kk in range(TOP_K):
        acc = acc + p[:, kk:kk + 1] * buf[slot, pl.ds(kk * tb, tb), :]
    o_ref[...] = _rms(acc, g_ref[...])


def _combine(dest_km, y, h, probs, g, tb):
    T = h.shape[0]
    grid_spec = pltpu.PrefetchScalarGridSpec(
        num_scalar_prefetch=1,
        grid=(T // tb,),
        in_specs=[
            pl.BlockSpec(memory_space=pl.ANY),
            pl.BlockSpec((tb, D_MODEL), lambda i, d: (i, 0)),
            pl.BlockSpec((tb, LANES), lambda i, d: (i, 0)),
            pl.BlockSpec((1, D_MODEL), lambda i, d: (0, 0)),
        ],
        out_specs=pl.BlockSpec((tb, D_MODEL), lambda i, d: (i, 0)),
        scratch_shapes=[
            pltpu.VMEM((2, tb * TOP_K, D_MODEL), F32),
            pltpu.SemaphoreType.DMA((2,)),
        ],
    )
    return pl.pallas_call(
        functools.partial(_combine_kernel, tb),
        grid_spec=grid_spec,
        out_shape=jax.ShapeDtypeStruct((T, D_MODEL), F32),
        compiler_params=_params(("arbitrary",)),
        name="moe_combine_norm",
    )(dest_km, y, h, probs, g)


def _routing(top_e, tb):
    T = top_e.shape[0]
    A = T * TOP_K
    n_blocks = -(-(A + N_EXPERTS * (MOE_BLOCK - 1)) // MOE_BLOCK)
    R = n_blocks * MOE_BLOCK
    e_flat = top_e.reshape(A)
    onehot = (e_flat[:, None] == jnp.arange(N_EXPERTS, dtype=jnp.int32)[None, :]).astype(jnp.int32)
    csum = jnp.cumsum(onehot, axis=0)
    rank = jnp.sum(csum * onehot, axis=1) - 1
    counts = csum[-1]
    padded = (counts + MOE_BLOCK - 1) // MOE_BLOCK * MOE_BLOCK
    pad_end = jnp.cumsum(padded)
    dest = (pad_end - padded)[e_flat] + rank
    row_tok = jnp.zeros((R,), jnp.int32).at[dest].set(jnp.arange(A, dtype=jnp.int32) // TOP_K)
    block_e = jnp.minimum(
        jnp.searchsorted(pad_end, jnp.arange(n_blocks, dtype=jnp.int32) * MOE_BLOCK, side='right'),
        N_EXPERTS - 1).astype(jnp.int32)
    nb_used = (pad_end[-1:] // MOE_BLOCK).astype(jnp.int32)
    dest_km = dest.reshape(T // tb, tb, TOP_K).transpose(0, 2, 1).reshape(A).astype(jnp.int32)
    return row_tok, dest_km, block_e, nb_used


def kernel(x, meta_tokens, norm1_g, w_in, b_igate, b_fgate, w_pool_mix, pool_scale, w_out, norm2_g, w_router,
           b_router, w_gu, b_gu, w_down, b_down, norm_f_g):
    B, S, D = x.shape
    L = N_META + S
    T = B * L
    H = MLSTM_HEADS
    BM = 688
    BM_NORM = 192
    TB = 64
    assert T % BM == 0 and T % BM_NORM == 0 and T % TB == 0

    meta = jnp.broadcast_to(meta_tokens[None].astype(x.dtype), (B, N_META, D))
    h0 = jnp.concatenate([meta, x], axis=1).reshape(T, D)

    l = 0
    wg = jnp.pad(w_in[l][:, PROJ_COLS:], ((0, 0), (0, LANES - 2 * H)))
    wg_hi = wg.astype(BF16)
    wg_lo = (wg - wg_hi.astype(F32)).astype(BF16)
    bg = jnp.pad(jnp.concatenate([b_igate[l], b_fgate[l]]), (0, LANES - 2 * H)).reshape(1, LANES)
    n1, gates = _norm1(h0, norm1_g[l].reshape(1, D), wg_hi, wg_lo, bg, BM_NORM)

    proj = _matmul(n1, w_in[l][:, :PROJ_COLS].astype(BF16), BM, 1024, F32, "in_proj")
    proj3 = proj.reshape(B, L, PROJ_COLS)
    pool_out = _pool(proj3, w_pool_mix[l].astype(BF16), pool_scale[l].reshape(1, POOL_WIDTH))
    mlstm_out = _mlstm(proj3, gates.reshape(B, L, LANES))

    wo = w_out[l].astype(BF16)
    h1 = _outproj(pool_out.reshape(T, POOL_WIDTH), mlstm_out.reshape(T, MLSTM_WIDTH),
                  wo[:POOL_WIDTH], wo[POOL_WIDTH:], h0, BM, 1024)

    wr = jnp.pad(w_router[l], ((0, 0), (0, LANES - N_EXPERTS)))
    wr_hi = wr.astype(BF16)
    wr_lo = (wr - wr_hi.astype(F32)).astype(BF16)
    br = jnp.pad(b_router[l], (0, LANES - N_EXPERTS)).reshape(1, LANES)
    n2, top_e, probs = _router(h1, norm2_g[l].reshape(1, D), wr_hi, wr_lo, br, BM_NORM)

    row_tok, dest_km, block_e, nb_used = _routing(top_e[:, :TOP_K], TB)
    xg = _dispatch(row_tok, n2)
    hmid = _gateup(block_e, nb_used, xg, w_gu[l], b_gu[l].reshape(N_EXPERTS, 1, 2 * D_FF), 512)
    y = _down(block_e, nb_used, hmid, w_down[l], b_down[l].reshape(N_EXPERTS, 1, D), 1024)
    out = _combine(dest_km, y, h1, probs, norm_f_g.reshape(1, D), TB)
    return out.reshape(B, L, D)[:, N_META:]
```

```python
import functools

import jax
import jax.numpy as jnp
from jax import lax
from jax.experimental import pallas as pl
from jax.experimental.pallas import tpu as pltpu

D_MODEL = 4096
N_META = 16
POOL_WINDOWS = (2, 4, 8, 16)
POOL_WIDTH = D_MODEL // 4
POOL_GROUP = POOL_WIDTH // len(POOL_WINDOWS)
MLSTM_WIDTH = D_MODEL - POOL_WIDTH
MLSTM_HEADS = 6
MLSTM_V_DIM = MLSTM_WIDTH // MLSTM_HEADS
MLSTM_QK_DIM = MLSTM_V_DIM // 2
GATE_SOFTCAP = 15.0
N_EXPERTS = 32
TOP_K = 4
D_FF = D_MODEL // 2
SWIGLU_ALPHA = 1.702
SWIGLU_LIMIT = 7.0
MOE_BLOCK = 128
EPS = 1e-6

LANES = 128
SEQ_CHUNK = 256
PROJ_COLS = POOL_WIDTH + 2 * MLSTM_HEADS * MLSTM_QK_DIM + 2 * MLSTM_WIDTH
VMEM_LIMIT = 56 * 1024 * 1024
SB_BLOCKS = 10
CHUNK_BLOCKS = 3
FF_TILE = 256
DOWN_TILE = 1024
N_FF_TILES = D_FF // FF_TILE
N_DOWN_TILES = D_MODEL // DOWN_TILE

F32 = jnp.float32
BF16 = jnp.bfloat16


def _params(sem, vmem=VMEM_LIMIT):
    return pltpu.CompilerParams(dimension_semantics=sem, vmem_limit_bytes=vmem)


def _split3(a):
    hi = a.astype(BF16)
    r1 = a - hi.astype(F32)
    mid = r1.astype(BF16)
    lo = (r1 - mid.astype(F32)).astype(BF16)
    return hi, mid, lo


def _dot(a, b):
    return jnp.dot(a, b, preferred_element_type=F32)


def _dotw(a, w):
    return lax.dot_general(a, w, (((1,), (0,)), ((), ())), preferred_element_type=F32)


def _dot3(a, w_hi, w_lo):
    a_hi = a.astype(BF16)
    a_lo = (a - a_hi.astype(F32)).astype(BF16)
    return _dot(a_hi, w_hi) + (_dot(a_hi, w_lo) + _dot(a_lo, w_hi))


def _rms(x, g):
    return x * lax.rsqrt(jnp.mean(x * x, axis=-1, keepdims=True) + EPS) * g


def _split2_cols(w, n_valid):
    lane = lax.broadcasted_iota(jnp.int32, w.shape, 1)
    w = jnp.where(lane < n_valid, w, 0.0)
    hi = w.astype(BF16)
    return hi, (w - hi.astype(F32)).astype(BF16)


def _norm1_kernel(h_ref, g_ref, wg_ref, bg_ref, n_ref, gate_ref):
    y = _rms(h_ref[...], g_ref[...])
    n_ref[...] = y.astype(BF16)
    wg_hi, wg_lo = _split2_cols(wg_ref[...], 2 * MLSTM_HEADS)
    gate_ref[...] = _dot3(y, wg_hi, wg_lo) + bg_ref[...]


def _norm1(h, g, w_in2, bg, bm):
    T = h.shape[0]
    return pl.pallas_call(
        _norm1_kernel,
        grid=(T // bm,),
        in_specs=[
            pl.BlockSpec((bm, D_MODEL), lambda i: (i, 0)),
            pl.BlockSpec((1, D_MODEL), lambda i: (0, 0)),
            pl.BlockSpec((D_MODEL, LANES), lambda i: (0, PROJ_COLS // LANES)),
            pl.BlockSpec((1, LANES), lambda i: (0, 0)),
        ],
        out_specs=[
            pl.BlockSpec((bm, D_MODEL), lambda i: (i, 0)),
            pl.BlockSpec((bm, LANES), lambda i: (i, 0)),
        ],
        out_shape=[
            jax.ShapeDtypeStruct((T, D_MODEL), BF16),
            jax.ShapeDtypeStruct((T, LANES), F32),
        ],
        compiler_params=_params(("parallel",)),
        name="norm1_gates",
    )(h, g, w_in2, bg)


def _inproj_kernel(x_ref, w_ref, o_ref, w_s):
    @pl.when(pl.program_id(1) == 0)
    def _():
        w_s[...] = w_ref[...].astype(BF16)

    o_ref[...] = _dot(x_ref[...], w_s[...]).astype(o_ref.dtype)


def _inproj(x, w, n_cols, bm, bn):
    M, K = x.shape
    return pl.pallas_call(
        _inproj_kernel,
        grid=(n_cols // bn, M // bm),
        in_specs=[
            pl.BlockSpec((bm, K), lambda j, i: (i, 0)),
            pl.BlockSpec((K, bn), lambda j, i: (0, j)),
        ],
        out_specs=pl.BlockSpec((bm, bn), lambda j, i: (i, j)),
        out_shape=jax.ShapeDtypeStruct((M, n_cols), F32),
        scratch_shapes=[pltpu.VMEM((K, bn), BF16)],
        compiler_params=_params(("parallel", "arbitrary")),
        name="in_proj",
    )(x, w)


def _outproj_kernel(p_ref, m_ref, w_ref, h_ref, o_ref, w_s):
    @pl.when(pl.program_id(1) == 0)
    def _():
        w_s[...] = w_ref[...].astype(BF16)

    o_ref[...] = h_ref[...] + (_dot(p_ref[...], w_s[:POOL_WIDTH]) + _dot(m_ref[...], w_s[POOL_WIDTH:]))


def _outproj(p, m, w, h, bm, bn):
    M = p.shape[0]
    K, N = w.shape
    return pl.pallas_call(
        _outproj_kernel,
        grid=(N // bn, M // bm),
        in_specs=[
            pl.BlockSpec((bm, POOL_WIDTH), lambda j, i: (i, 0)),
            pl.BlockSpec((bm, MLSTM_WIDTH), lambda j, i: (i, 0)),
            pl.BlockSpec((K, bn), lambda j, i: (0, j)),
            pl.BlockSpec((bm, bn), lambda j, i: (i, j)),
        ],
        out_specs=pl.BlockSpec((bm, bn), lambda j, i: (i, j)),
        out_shape=jax.ShapeDtypeStruct((M, N), F32),
        scratch_shapes=[pltpu.VMEM((K, bn), BF16)],
        compiler_params=_params(("parallel", "arbitrary")),
        name="out_proj",
    )(p, m, w, h)


def _pool_kernel(u_ref, w_ref, s_ref, o_ref, carry_ref):
    c = pl.program_id(1)

    @pl.when(c == 0)
    def _():
        carry_ref[...] = jnp.zeros_like(carry_ref)

    u = u_ref[0]
    ext = jnp.concatenate([carry_ref[...], u], axis=0)
    carry_ref[...] = u[SEQ_CHUNK - 16:, :]
    pos = c * SEQ_CHUNK + lax.broadcasted_iota(jnp.int32, (SEQ_CHUNK, 1), 0)
    for g, win in enumerate(POOL_WINDOWS):
        cols = slice(g * POOL_GROUP, (g + 1) * POOL_GROUP)
        s = ext[:, cols]
        span = 1
        while span < win:
            s = s + pltpu.roll(s, span, axis=0)
            span *= 2
        cnt = jnp.minimum(pos + 1, win).astype(F32)
        d = s[16:, :] / cnt - u[:, cols]
        y = _dot(d.astype(BF16), w_ref[g]) * s_ref[:, cols]
        o_ref[0, :, cols] = y.astype(o_ref.dtype)


def _pool(proj3, w_mix, scale):
    B, L, _ = proj3.shape
    nc = pl.cdiv(L, SEQ_CHUNK)
    return pl.pallas_call(
        _pool_kernel,
        grid=(B, nc),
        in_specs=[
            pl.BlockSpec((1, SEQ_CHUNK, POOL_WIDTH), lambda b, c: (b, c, 0)),
            pl.BlockSpec((len(POOL_WINDOWS), POOL_GROUP, POOL_GROUP), lambda b, c: (0, 0, 0)),
            pl.BlockSpec((1, POOL_WIDTH), lambda b, c: (0, 0)),
        ],
        out_specs=pl.BlockSpec((1, SEQ_CHUNK, POOL_WIDTH), lambda b, c: (b, c, 0)),
        out_shape=jax.ShapeDtypeStruct((B, L, POOL_WIDTH), BF16),
        scratch_shapes=[pltpu.VMEM((16, POOL_WIDTH), F32)],
        compiler_params=_params(("parallel", "arbitrary")),
        name="pool_mixer",
    )(proj3, w_mix, scale)


def _soft_cap(a):
    return GATE_SOFTCAP * jnp.tanh(a / GATE_SOFTCAP)


def _log_sigmoid(a):
    return jnp.minimum(a, 0.0) - jnp.log1p(jnp.exp(-jnp.abs(a)))


def _mlstm_kernel(seq_len, q_ref, k_ref, v_ref, o_ref, gate_ref, out_ref, c_ref, n_ref, m_ref):
    h = pl.program_id(1)
    c = pl.program_id(2)
    Lc = SEQ_CHUNK

    @pl.when(c == 0)
    def _():
        c_ref[...] = jnp.zeros_like(c_ref)
        n_ref[...] = jnp.zeros_like(n_ref)
        m_ref[...] = jnp.zeros_like(m_ref)

    row = lax.broadcasted_iota(jnp.int32, (Lc, 1), 0)
    col = lax.broadcasted_iota(jnp.int32, (1, Lc), 1)
    ok_col = (c * Lc + row) < seq_len
    ok_row = (c * Lc + col) < seq_len

    q = jnp.where(ok_col, q_ref[0], 0.0) * (MLSTM_QK_DIM ** -0.5)
    k = jnp.where(ok_col, k_ref[0], 0.0)
    v = jnp.where(ok_col, v_ref[0], 0.0)

    gc = gate_ref[0]
    gt = gc.T
    lane = lax.broadcasted_iota(jnp.int32, (1, LANES), 1)
    sub = lax.broadcasted_iota(jnp.int32, (LANES, 1), 0)
    gi_c = jnp.sum(jnp.where(lane == h, gc, 0.0), axis=1, keepdims=True)
    gf_c = jnp.sum(jnp.where(lane == h + MLSTM_HEADS, gc, 0.0), axis=1, keepdims=True)
    gi_r = jnp.sum(jnp.where(sub == h, gt, 0.0), axis=0, keepdims=True)
    gf_r = jnp.sum(jnp.where(sub == h + MLSTM_HEADS, gt, 0.0), axis=0, keepdims=True)
    i_c = jnp.where(ok_col, _soft_cap(gi_c), 0.0)
    f_c = jnp.where(ok_col, _log_sigmoid(_soft_cap(gf_c)), 0.0)
    i_r = jnp.where(ok_row, _soft_cap(gi_r), 0.0)
    f_r = jnp.where(ok_row, _log_sigmoid(_soft_cap(gf_r)), 0.0)

    causal = col <= row
    tri = causal.astype(BF16)
    fb_c = jnp.broadcast_to(f_c, (Lc, LANES))
    b_c = sum(_dot(tri, p) for p in _split3(fb_c))[:, 0:1]
    fb_r = jnp.broadcast_to(f_r, (8, Lc))
    tri_t = (row <= col).astype(BF16)
    b_r8 = sum(_dot(p, tri_t) for p in _split3(fb_r))
    b_r = b_r8[0:1, :]
    g_tot = b_r8[0:1, Lc - 1:Lc]

    m_prev = m_ref[...]
    dlog = jnp.where(causal, b_c - b_r + i_r, -jnp.inf)
    inter_log = b_c + m_prev
    m_out = jnp.maximum(inter_log, jnp.max(dlog, axis=1, keepdims=True))
    wts = jnp.exp(dlog - m_out)
    inter_w = jnp.exp(inter_log - m_out)

    qb = q.astype(BF16)
    kb = k.astype(BF16)
    vb = v.astype(BF16)
    s = lax.dot_general(qb, kb, (((1,), (1,)), ((), ())), preferred_element_type=F32) * wts
    num = _dot(s.astype(BF16), vb) + inter_w * _dot(qb, c_ref[...].astype(BF16))
    den = jnp.sum(s, axis=1, keepdims=True) + inter_w * jnp.sum(q * n_ref[...], axis=1, keepdims=True)
    hh = num / jnp.maximum(jnp.abs(den), jnp.exp(-m_out))
    out_ref[0] = (jax.nn.sigmoid(o_ref[0]) * hh).astype(out_ref.dtype)

    a_c = g_tot - b_c + i_c
    m_new = jnp.maximum(g_tot + m_prev, jnp.max(a_c, axis=0, keepdims=True))
    wk = k * jnp.exp(a_c - m_new)
    decay = jnp.exp(g_tot + m_prev - m_new)
    c_ref[...] = decay * c_ref[...] + lax.dot_general(
        wk.astype(BF16), vb, (((0,), (0,)), ((), ())), preferred_element_type=F32)
    n_ref[...] = decay * n_ref[...] + jnp.sum(wk, axis=0, keepdims=True)
    m_ref[...] = m_new


def _mlstm(proj3, gates3):
    B, L, _ = proj3.shape
    nc = pl.cdiv(L, SEQ_CHUNK)
    dk, dv, H = MLSTM_QK_DIM, MLSTM_V_DIM, MLSTM_HEADS
    q0 = POOL_WIDTH // dk
    k0 = q0 + H
    v0 = (POOL_WIDTH + 2 * H * dk) // dv
    o0 = v0 + H
    return pl.pallas_call(
        functools.partial(_mlstm_kernel, L),
        grid=(B, H, nc),
        in_specs=[
            pl.BlockSpec((1, SEQ_CHUNK, dk), lambda b, h, c: (b, c, q0 + h)),
            pl.BlockSpec((1, SEQ_CHUNK, dk), lambda b, h, c: (b, c, k0 + h)),
            pl.BlockSpec((1, SEQ_CHUNK, dv), lambda b, h, c: (b, c, v0 + h)),
            pl.BlockSpec((1, SEQ_CHUNK, dv), lambda b, h, c: (b, c, o0 + h)),
            pl.BlockSpec((1, SEQ_CHUNK, LANES), lambda b, h, c: (b, c, 0)),
        ],
        out_specs=pl.BlockSpec((1, SEQ_CHUNK, dv), lambda b, h, c: (b, c, h)),
        out_shape=jax.ShapeDtypeStruct((B, L, MLSTM_WIDTH), BF16),
        scratch_shapes=[
            pltpu.VMEM((dk, dv), F32),
            pltpu.VMEM((1, dk), F32),
            pltpu.VMEM((1, 1), F32),
        ],
        compiler_params=_params(("parallel", "parallel", "arbitrary")),
        name="mlstm",
    )(proj3, proj3, proj3, proj3, gates3)


def _pack_bf16_pair(lo, hi):
    lo_b = lax.bitcast_convert_type(lo.astype(BF16).astype(F32), jnp.uint32)
    hi_b = lax.bitcast_convert_type(hi.astype(BF16).astype(F32), jnp.uint32)
    return (lo_b >> 16) | hi_b


def _unpack_bf16_pair(w):
    lo = lax.bitcast_convert_type(w << 16, F32).astype(BF16)
    hi = lax.bitcast_convert_type(w & jnp.uint32(0xFFFF0000), F32).astype(BF16)
    return lo, hi


def _router_kernel(h_ref, g_ref, wr_ref, br_ref, n_ref, e_ref, p_ref):
    y = _rms(h_ref[...], g_ref[...])
    half = D_MODEL // 2
    n_ref[...] = _pack_bf16_pair(y[:, :half], y[:, half:])
    wr = wr_ref[...]
    wr_hi = wr.astype(BF16)
    wr_lo = (wr - wr_hi.astype(F32)).astype(BF16)
    logits = _dot3(y, wr_hi, wr_lo) + br_ref[...]
    lane = lax.broadcasted_iota(jnp.int32, logits.shape, 1).astype(F32)
    l = jnp.where(lane < N_EXPERTS, logits, -jnp.inf)
    vals, idxs = [], []
    for _ in range(TOP_K):
        m = jnp.max(l, axis=1, keepdims=True)
        idx = jnp.min(jnp.where(l == m, lane, float(LANES)), axis=1, keepdims=True)
        vals.append(m)
        idxs.append(idx)
        l = jnp.where(lane == idx, -jnp.inf, l)
    ex = [jnp.exp(v - vals[0]) for v in vals]
    tot = ex[0] + ex[1] + ex[2] + ex[3]
    e_out = jnp.zeros(logits.shape, F32)
    p_out = jnp.zeros(logits.shape, F32)
    for kk in range(TOP_K):
        e_out = jnp.where(lane == kk, idxs[kk], e_out)
        p_out = jnp.where(lane == kk, ex[kk] / tot, p_out)
    e_ref[...] = e_out.astype(jnp.int32)
    p_ref[...] = p_out


def _router(h, g, wr, br, bm):
    T = h.shape[0]
    return pl.pallas_call(
        _router_kernel,
        grid=(T // bm,),
        in_specs=[
            pl.BlockSpec((bm, D_MODEL), lambda i: (i, 0)),
            pl.BlockSpec((1, D_MODEL), lambda i: (0, 0)),
            pl.BlockSpec((D_MODEL, N_EXPERTS), lambda i: (0, 0)),
            pl.BlockSpec((1, N_EXPERTS), lambda i: (0, 0)),
        ],
        out_specs=[
            pl.BlockSpec((bm, D_MODEL // 2), lambda i: (i, 0)),
            pl.BlockSpec((bm, N_EXPERTS), lambda i: (i, 0)),
            pl.BlockSpec((bm, N_EXPERTS), lambda i: (i, 0)),
        ],
        out_shape=[
            jax.ShapeDtypeStruct((T, D_MODEL // 2), jnp.uint32),
            jax.ShapeDtypeStruct((T, N_EXPERTS), jnp.int32),
            jax.ShapeDtypeStruct((T, N_EXPERTS), F32),
        ],
        compiler_params=_params(("parallel",)),
        name="norm2_router",
    )(h, g, wr, br)


def _row_gather_start(src_hbm, idx_ref, base, n_rows, buf, slot, sem):
    def body(r, carry):
        tok = idx_ref[base + r]
        pltpu.make_async_copy(src_hbm.at[pl.ds(tok, 1), :], buf.at[slot, pl.ds(r, 1), :], sem.at[slot]).start()
        return carry
    lax.fori_loop(0, n_rows, body, 0)


def _row_gather_wait(src_hbm, n_rows, buf, slot, sem):
    pltpu.make_async_copy(src_hbm.at[pl.ds(0, n_rows), :], buf.at[slot], sem.at[slot]).wait()


def _experts_kernel(sbe_ref, rb0_ref, nbk_ref, used_ref, tok_ref,
                    x_hbm, wg_ref, wl_ref, bg_ref, bl_ref, wd_ref, bd_ref,
                    y_hbm,
                    x_res, hmid, ystage, gsem, ysem, ycnt, ypend):
    s = pl.program_id(0)
    t = pl.program_id(1)
    ns = pl.num_programs(0)
    nt = pl.num_programs(1)
    nb = nbk_ref[s]
    half = D_MODEL // 2
    n_blocks_total = y_hbm.shape[0] // MOE_BLOCK

    def x_block_copy(tok, r, n):
        return pltpu.make_async_copy(x_hbm.at[pl.ds(tok, n), :], x_res.at[pl.ds(r, n), :], gsem.at[0])

    def gather_start(sb):
        base = rb0_ref[sb] * MOE_BLOCK

        def body(r, carry):
            x_block_copy(tok_ref[base + r], r, 1).start()
            return carry
        lax.fori_loop(0, nbk_ref[sb] * MOE_BLOCK, body, 0)

    def gather_wait(n_blocks):
        def body(r, carry):
            x_block_copy(0, 0, MOE_BLOCK).wait()
            return carry
        lax.fori_loop(0, n_blocks, body, 0)

    def y_copy(slot, piece, row0, col0):
        return pltpu.make_async_copy(
            ystage.at[slot, pl.ds(piece * MOE_BLOCK, MOE_BLOCK), :],
            y_hbm.at[pl.ds(pl.multiple_of(row0, MOE_BLOCK), MOE_BLOCK), pl.ds(pl.multiple_of(col0, DOWN_TILE), DOWN_TILE)],
            ysem.at[slot])

    def y_drain(slot):
        for piece in range(CHUNK_BLOCKS):
            @pl.when(ypend[slot] > piece)
            def _():
                y_copy(slot, 0, 0, 0).wait()
        ypend[slot] = 0

    def y_emit(val, n_pieces, row0, col0):
        slot = lax.rem(ycnt[0], 2)
        y_drain(slot)
        ystage[slot, pl.ds(0, n_pieces * MOE_BLOCK), :] = val
        for piece in range(n_pieces):
            y_copy(slot, piece, row0 + piece * MOE_BLOCK, col0).start()
        ypend[slot] = n_pieces
        ycnt[0] = ycnt[0] + 1

    def for_chunks(n_blocks, fn):
        n_full = n_blocks // CHUNK_BLOCKS

        def body(c, carry):
            fn(c * CHUNK_BLOCKS, CHUNK_BLOCKS)
            return carry
        lax.fori_loop(0, n_full, body, 0)
        rem = n_blocks - n_full * CHUNK_BLOCKS
        for k in range(1, CHUNK_BLOCKS):
            @pl.when(rem == k)
            def _():
                fn(n_full * CHUNK_BLOCKS, k)

    @pl.when(jnp.logical_and(s == 0, t == 0))
    def _():
        ycnt[0] = 0
        ypend[0] = 0
        ypend[1] = 0
        gather_start(0)

    @pl.when(t == 0)
    def _():
        gather_wait(nb)

    @pl.when(jnp.logical_and(t == N_FF_TILES, s + 1 < ns))
    def _():
        gather_start(jnp.minimum(s + 1, ns - 1))

    @pl.when(jnp.logical_and(nb > 0, t < N_FF_TILES))
    def _():
        bg = bg_ref[0]
        bl = bl_ref[0]
        tt = jnp.minimum(t, N_FF_TILES - 1)

        def chunk(b0, k):
            rows = pl.ds(pl.multiple_of(b0 * MOE_BLOCK, MOE_BLOCK), k * MOE_BLOCK)
            x_lo, x_hi = _unpack_bf16_pair(x_res[rows, :])
            gate = _dotw(x_lo, wg_ref[0, :half, :]) + _dotw(x_hi, wg_ref[0, half:, :]) + bg
            lin = _dotw(x_lo, wl_ref[0, :half, :]) + _dotw(x_hi, wl_ref[0, half:, :]) + bl
            gate = jnp.minimum(gate, SWIGLU_LIMIT)
            lin = jnp.clip(lin, -SWIGLU_LIMIT, SWIGLU_LIMIT)
            hmid[tt, rows, :] = (gate * jax.nn.sigmoid(SWIGLU_ALPHA * gate) * (lin + 1.0)).astype(BF16)
        for_chunks(nb, chunk)

    @pl.when(jnp.logical_and(nb > 0, t >= N_FF_TILES))
    def _():
        bd = bd_ref[0]
        col0 = (t - N_FF_TILES) * DOWN_TILE
        row_base = rb0_ref[s] * MOE_BLOCK

        def chunk(b0, k):
            rows = pl.ds(pl.multiple_of(b0 * MOE_BLOCK, MOE_BLOCK), k * MOE_BLOCK)
            acc = bd + _dotw(hmid[0, rows, :], wd_ref[0, 0:FF_TILE, :])
            for j in range(1, N_FF_TILES):
                acc = acc + _dotw(hmid[j, rows, :], wd_ref[0, j * FF_TILE:(j + 1) * FF_TILE, :])
            y_emit(acc, k, row_base + b0 * MOE_BLOCK, col0)
        for_chunks(nb, chunk)

    @pl.when(jnp.logical_and(s == ns - 1, t == nt - 1))
    def _():
        y_drain(0)
        y_drain(1)
        ystage[0, pl.ds(0, MOE_BLOCK), :] = jnp.zeros((MOE_BLOCK, DOWN_TILE), F32)

        def zero_copy(b, j):
            return y_copy(0, 0, b * MOE_BLOCK, j * DOWN_TILE)

        def start_body(b, carry):
            for j in range(N_DOWN_TILES):
                zero_copy(b, j).start()
            return carry

        def wait_body(b, carry):
            for j in range(N_DOWN_TILES):
                zero_copy(0, 0).wait()
            return carry
        lax.fori_loop(used_ref[0], n_blocks_total, start_body, 0)
        lax.fori_loop(used_ref[0], n_blocks_total, wait_body, 0)


def _experts(sb_e, sb_rb0, sb_nb, n_used, row_tok, n2p, w_gu, b_gu, w_down, b_down):
    R = row_tok.shape[0]
    S = sb_e.shape[0]
    sb_rows = SB_BLOCKS * MOE_BLOCK

    def ff_tile(s, t, nbk):
        return jnp.where(nbk[s] > 0, jnp.minimum(t, N_FF_TILES - 1), N_FF_TILES - 1)

    def down_tile(s, t, nbk):
        return jnp.where(nbk[s] > 0, jnp.maximum(t - N_FF_TILES, 0), N_DOWN_TILES - 1)

    grid_spec = pltpu.PrefetchScalarGridSpec(
        num_scalar_prefetch=5,
        grid=(S, N_FF_TILES + N_DOWN_TILES),
        in_specs=[
            pl.BlockSpec(memory_space=pl.ANY),
            pl.BlockSpec((1, D_MODEL, FF_TILE), lambda s, t, e, r0, nbk, u, tok: (e[s], 0, ff_tile(s, t, nbk))),
            pl.BlockSpec((1, D_MODEL, FF_TILE), lambda s, t, e, r0, nbk, u, tok: (e[s], 0, N_FF_TILES + ff_tile(s, t, nbk))),
            pl.BlockSpec((1, 1, FF_TILE), lambda s, t, e, r0, nbk, u, tok: (e[s], 0, ff_tile(s, t, nbk))),
            pl.BlockSpec((1, 1, FF_TILE), lambda s, t, e, r0, nbk, u, tok: (e[s], 0, N_FF_TILES + ff_tile(s, t, nbk))),
            pl.BlockSpec((1, D_FF, DOWN_TILE), lambda s, t, e, r0, nbk, u, tok: (e[s], 0, down_tile(s, t, nbk))),
            pl.BlockSpec((1, 1, DOWN_TILE), lambda s, t, e, r0, nbk, u, tok: (e[s], 0, down_tile(s, t, nbk))),
        ],
        out_specs=pl.BlockSpec(memory_space=pl.ANY),
        scratch_shapes=[
            pltpu.VMEM((sb_rows, D_MODEL // 2), jnp.uint32),
            pltpu.VMEM((N_FF_TILES, sb_rows, FF_TILE), BF16),
            pltpu.VMEM((2, CHUNK_BLOCKS * MOE_BLOCK, DOWN_TILE), F32),
            pltpu.SemaphoreType.DMA((1,)),
            pltpu.SemaphoreType.DMA((2,)),
            pltpu.SMEM((1,), jnp.int32),
            pltpu.SMEM((2,), jnp.int32),
        ],
    )
    return pl.pallas_call(
        _experts_kernel,
        grid_spec=grid_spec,
        out_shape=jax.ShapeDtypeStruct((R, D_MODEL), F32),
        compiler_params=_params(("arbitrary", "arbitrary")),
        name="moe_experts",
    )(sb_e, sb_rb0, sb_nb, n_used, row_tok, n2p, w_gu, w_gu, b_gu, b_gu, w_down, b_down)


def _combine_kernel(tb, dest_ref, y_hbm, h_ref, p_ref, g_ref, o_ref, buf, sem):
    i = pl.program_id(0)
    nb = pl.num_programs(0)
    slot = lax.rem(i, 2)
    rows = tb * TOP_K

    @pl.when(i == 0)
    def _():
        _row_gather_start(y_hbm, dest_ref, 0, rows, buf, 0, sem)

    @pl.when(i + 1 < nb)
    def _():
        _row_gather_start(y_hbm, dest_ref, (i + 1) * rows, rows, buf, 1 - slot, sem)

    _row_gather_wait(y_hbm, rows, buf, slot, sem)
    acc = h_ref[...]
    p = p_ref[...]
    for kk in range(TOP_K):
        acc = acc + p[:, kk:kk + 1] * buf[slot, pl.ds(kk * tb, tb), :]
    o_ref[...] = _rms(acc, g_ref[...])


def _combine(dest_km, y, h, probs, g, tb):
    T = h.shape[0]
    grid_spec = pltpu.PrefetchScalarGridSpec(
        num_scalar_prefetch=1,
        grid=(T // tb,),
        in_specs=[
            pl.BlockSpec(memory_space=pl.ANY),
            pl.BlockSpec((tb, D_MODEL), lambda i, d: (i, 0)),
            pl.BlockSpec((tb, N_EXPERTS), lambda i, d: (i, 0)),
            pl.BlockSpec((1, D_MODEL), lambda i, d: (0, 0)),
        ],
        out_specs=pl.BlockSpec((tb, D_MODEL), lambda i, d: (i, 0)),
        scratch_shapes=[
            pltpu.VMEM((2, tb * TOP_K, D_MODEL), F32),
            pltpu.SemaphoreType.DMA((2,)),
        ],
    )
    return pl.pallas_call(
        functools.partial(_combine_kernel, tb),
        grid_spec=grid_spec,
        out_shape=jax.ShapeDtypeStruct((T, D_MODEL), F32),
        compiler_params=_params(("arbitrary",)),
        name="moe_combine_norm",
    )(dest_km, y, h, probs, g)


def _routing(top_e, tb):
    T = top_e.shape[0]
    A = T * TOP_K
    n_blocks = -(-(A + N_EXPERTS * (MOE_BLOCK - 1)) // MOE_BLOCK)
    R = n_blocks * MOE_BLOCK
    n_sb = N_EXPERTS + n_blocks // SB_BLOCKS
    i32 = jnp.int32
    e_flat = top_e.reshape(A)
    onehot = (e_flat[:, None] == jnp.arange(N_EXPERTS, dtype=i32)[None, :]).astype(i32)
    csum = jnp.cumsum(onehot, axis=0)
    rank = jnp.sum(csum * onehot, axis=1) - 1
    counts = csum[-1]
    blocks = (counts + MOE_BLOCK - 1) // MOE_BLOCK
    blk_end = jnp.cumsum(blocks)
    blk_start = blk_end - blocks
    dest = (blk_start * MOE_BLOCK)[e_flat] + rank
    row_tok = jnp.zeros((R,), i32).at[dest].set(jnp.arange(A, dtype=i32) // TOP_K)
    dest_km = dest.reshape(T // tb, tb, TOP_K).transpose(0, 2, 1).reshape(A).astype(i32)

    sbs = (blocks + SB_BLOCKS - 1) // SB_BLOCKS
    sb_end = jnp.cumsum(sbs)
    sb_start = sb_end - sbs
    s_idx = jnp.arange(n_sb, dtype=i32)
    active = s_idx < sb_end[-1]
    e_s = jnp.minimum(jnp.sum((sb_end[None, :] <= s_idx[:, None]).astype(i32), axis=1), N_EXPERTS - 1)
    k_s = s_idx - sb_start[e_s]
    sb_rb0 = jnp.where(active, blk_start[e_s] + k_s * SB_BLOCKS, 0)
    sb_nb = jnp.where(active, jnp.minimum(blocks[e_s] - k_s * SB_BLOCKS, SB_BLOCKS), 0)
    sb_e = jnp.where(active, e_s, jnp.max(jnp.where(active, e_s, 0)))
    n_used = blk_end[-1:].astype(i32)
    return row_tok, dest_km, sb_e.astype(i32), sb_rb0.astype(i32), sb_nb.astype(i32), n_used


def kernel(x, meta_tokens, norm1_g, w_in, b_igate, b_fgate, w_pool_mix, pool_scale, w_out, norm2_g, w_router,
           b_router, w_gu, b_gu, w_down, b_down, norm_f_g):
    B, S, D = x.shape
    L = N_META + S
    T = B * L
    H = MLSTM_HEADS
    BM_IN = 1376
    BM_OUT = 688
    BM_NORM = 192
    TB = 64
    assert T % BM_IN == 0 and T % BM_OUT == 0 and T % BM_NORM == 0 and T % TB == 0
    assert w_in.shape[0] == 1

    meta = jnp.broadcast_to(meta_tokens[None].astype(x.dtype), (B, N_META, D))
    h0 = jnp.concatenate([meta, x], axis=1).reshape(T, D)

    l = 0
    w_in2 = w_in.reshape(D, PROJ_COLS + 2 * H)
    bg = jnp.pad(jnp.concatenate([b_igate[l], b_fgate[l]]), (0, LANES - 2 * H)).reshape(1, LANES)
    n1, gates = _norm1(h0, norm1_g[l].reshape(1, D), w_in2, bg, BM_NORM)

    proj3 = _inproj(n1, w_in2, PROJ_COLS, BM_IN, 512).reshape(B, L, PROJ_COLS)
    pool_out = _pool(proj3, w_pool_mix[l].astype(BF16), pool_scale[l].reshape(1, POOL_WIDTH))
    mlstm_out = _mlstm(proj3, gates.reshape(B, L, LANES))

    h1 = _outproj(pool_out.reshape(T, POOL_WIDTH), mlstm_out.reshape(T, MLSTM_WIDTH),
                  w_out.reshape(D, D), h0, BM_OUT, 512)

    n2p, top_e, probs = _router(h1, norm2_g[l].reshape(1, D), w_router.reshape(D, N_EXPERTS),
                                b_router.reshape(1, N_EXPERTS), BM_NORM)

    row_tok, dest_km, sb_e, sb_rb0, sb_nb, n_used = _routing(top_e[:, :TOP_K], TB)
    y = _experts(sb_e, sb_rb0, sb_nb, n_used, row_tok, n2p,
                 w_gu.reshape(N_EXPERTS, D, 2 * D_FF), b_gu.reshape(N_EXPERTS, 1, 2 * D_FF),
                 w_down.reshape(N_EXPERTS, D_FF, D), b_down.reshape(N_EXPERTS, 1, D))
    out = _combine(dest_km, y, h1, probs, norm_f_g.reshape(1, D), TB)
    return out.reshape(B, L, D)[:, N_META:]
```

```python
import functools

import jax
import jax.numpy as jnp
from jax import lax
from jax.experimental import pallas as pl
from jax.experimental.pallas import tpu as pltpu

D_MODEL = 4096
N_META = 16
POOL_WINDOWS = (2, 4, 8, 16)
POOL_WIDTH = D_MODEL // 4
POOL_GROUP = POOL_WIDTH // len(POOL_WINDOWS)
MLSTM_WIDTH = D_MODEL - POOL_WIDTH
MLSTM_HEADS = 6
MLSTM_V_DIM = MLSTM_WIDTH // MLSTM_HEADS
MLSTM_QK_DIM = MLSTM_V_DIM // 2
GATE_SOFTCAP = 15.0
N_EXPERTS = 32
TOP_K = 4
D_FF = D_MODEL // 2
SWIGLU_ALPHA = 1.702
SWIGLU_LIMIT = 7.0
MOE_BLOCK = 128
EPS = 1e-6

LANES = 128
SEQ_CHUNK = 256
PROJ_COLS = POOL_WIDTH + 2 * MLSTM_HEADS * MLSTM_QK_DIM + 2 * MLSTM_WIDTH
VMEM_LIMIT = 56 * 1024 * 1024
SB_BLOCKS = 10
CHUNK_BLOCKS = 3
FF_TILE = 256
DOWN_TILE = 1024
N_FF_TILES = D_FF // FF_TILE
N_DOWN_TILES = D_MODEL // DOWN_TILE

F32 = jnp.float32
BF16 = jnp.bfloat16


def _params(sem, vmem=VMEM_LIMIT):
    return pltpu.CompilerParams(dimension_semantics=sem, vmem_limit_bytes=vmem)


def _split3(a):
    hi = a.astype(BF16)
    r1 = a - hi.astype(F32)
    mid = r1.astype(BF16)
    lo = (r1 - mid.astype(F32)).astype(BF16)
    return hi, mid, lo


def _dot(a, b):
    return jnp.dot(a, b, preferred_element_type=F32)


def _dotw(a, w):
    return lax.dot_general(a, w, (((1,), (0,)), ((), ())), preferred_element_type=F32)


def _dot_nt(a, wt):
    return lax.dot_general(a, wt, (((1,), (1,)), ((), ())), preferred_element_type=F32)


def _dot3_nt(a, wt):
    a_hi = a.astype(BF16)
    a_lo = (a - a_hi.astype(F32)).astype(BF16)
    w_hi = wt.astype(BF16)
    w_lo = (wt - w_hi.astype(F32)).astype(BF16)
    return _dot_nt(a_hi, w_hi) + (_dot_nt(a_hi, w_lo) + _dot_nt(a_lo, w_hi))


def _rms(x, g):
    return x * lax.rsqrt(jnp.mean(x * x, axis=-1, keepdims=True) + EPS) * g


GATE_ROWS = 16


def _norm1_kernel(h_ref, g_ref, wg_ref, bg_ref, n_ref, gate_ref):
    y = _rms(h_ref[...], g_ref[...])
    n_ref[...] = y.astype(BF16)
    sub = lax.broadcasted_iota(jnp.int32, (GATE_ROWS, D_MODEL), 0)
    wg = jnp.where(sub < 2 * MLSTM_HEADS, wg_ref[...], 0.0)
    wg = jnp.concatenate([wg, jnp.zeros((LANES - GATE_ROWS, D_MODEL), F32)], axis=0)
    gate_ref[...] = _dot3_nt(y, wg) + bg_ref[...]


def _norm1(h, g, w_in_t, bg, bm):
    T = h.shape[0]
    return pl.pallas_call(
        _norm1_kernel,
        grid=(T // bm,),
        in_specs=[
            pl.BlockSpec((bm, D_MODEL), lambda i: (i, 0)),
            pl.BlockSpec((1, D_MODEL), lambda i: (0, 0)),
            pl.BlockSpec((GATE_ROWS, D_MODEL), lambda i: (PROJ_COLS // GATE_ROWS, 0)),
            pl.BlockSpec((1, LANES), lambda i: (0, 0)),
        ],
        out_specs=[
            pl.BlockSpec((bm, D_MODEL), lambda i: (i, 0)),
            pl.BlockSpec((bm, LANES), lambda i: (i, 0)),
        ],
        out_shape=[
            jax.ShapeDtypeStruct((T, D_MODEL), BF16),
            jax.ShapeDtypeStruct((T, LANES), F32),
        ],
        compiler_params=_params(("parallel",)),
        name="norm1_gates",
    )(h, g, w_in_t, bg)


def _inproj_kernel(x_ref, wt_ref, o_ref):
    o_ref[...] = _dot_nt(x_ref[...], wt_ref[...]).astype(o_ref.dtype)


def _inproj(x, wt, n_cols, bm, bn):
    M, K = x.shape
    return pl.pallas_call(
        _inproj_kernel,
        grid=(n_cols // bn, M // bm),
        in_specs=[
            pl.BlockSpec((bm, K), lambda j, i: (i, 0)),
            pl.BlockSpec((bn, K), lambda j, i: (j, 0)),
        ],
        out_specs=pl.BlockSpec((bm, bn), lambda j, i: (i, j)),
        out_shape=jax.ShapeDtypeStruct((M, n_cols), F32),
        compiler_params=_params(("parallel", "parallel")),
        name="in_proj",
    )(x, wt)


def _outproj_kernel(p_ref, m_ref, w_ref, h_ref, o_ref):
    o_ref[...] = h_ref[...] + (_dotw(p_ref[...], w_ref[:POOL_WIDTH, :]) + _dotw(m_ref[...], w_ref[POOL_WIDTH:, :]))


def _outproj(p, m, w, h, bm, bn):
    M = p.shape[0]
    K, N = w.shape
    return pl.pallas_call(
        _outproj_kernel,
        grid=(N // bn, M // bm),
        in_specs=[
            pl.BlockSpec((bm, POOL_WIDTH), lambda j, i: (i, 0)),
            pl.BlockSpec((bm, MLSTM_WIDTH), lambda j, i: (i, 0)),
            pl.BlockSpec((K, bn), lambda j, i: (0, j)),
            pl.BlockSpec((bm, bn), lambda j, i: (i, j)),
        ],
        out_specs=pl.BlockSpec((bm, bn), lambda j, i: (i, j)),
        out_shape=jax.ShapeDtypeStruct((M, N), F32),
        compiler_params=_params(("parallel", "parallel")),
        name="out_proj",
    )(p, m, w, h)


def _pool_kernel(u_ref, w_ref, s_ref, o_ref, carry_ref):
    c = pl.program_id(1)

    @pl.when(c == 0)
    def _():
        carry_ref[...] = jnp.zeros_like(carry_ref)

    u = u_ref[0]
    ext = jnp.concatenate([carry_ref[...], u], axis=0)
    carry_ref[...] = u[SEQ_CHUNK - 16:, :]
    pos = c * SEQ_CHUNK + lax.broadcasted_iota(jnp.int32, (SEQ_CHUNK, 1), 0)
    for g, win in enumerate(POOL_WINDOWS):
        cols = slice(g * POOL_GROUP, (g + 1) * POOL_GROUP)
        s = ext[:, cols]
        span = 1
        while span < win:
            s = s + pltpu.roll(s, span, axis=0)
            span *= 2
        cnt = jnp.minimum(pos + 1, win).astype(F32)
        d = s[16:, :] / cnt - u[:, cols]
        y = _dot(d.astype(BF16), w_ref[g]) * s_ref[:, cols]
        o_ref[0, :, cols] = y.astype(o_ref.dtype)


def _pool(proj3, w_mix, scale):
    B, L, _ = proj3.shape
    nc = pl.cdiv(L, SEQ_CHUNK)
    return pl.pallas_call(
        _pool_kernel,
        grid=(B, nc),
        in_specs=[
            pl.BlockSpec((1, SEQ_CHUNK, POOL_WIDTH), lambda b, c: (b, c, 0)),
            pl.BlockSpec((len(POOL_WINDOWS), POOL_GROUP, POOL_GROUP), lambda b, c: (0, 0, 0)),
            pl.BlockSpec((1, POOL_WIDTH), lambda b, c: (0, 0)),
        ],
        out_specs=pl.BlockSpec((1, SEQ_CHUNK, POOL_WIDTH), lambda b, c: (b, c, 0)),
        out_shape=jax.ShapeDtypeStruct((B, L, POOL_WIDTH), BF16),
        scratch_shapes=[pltpu.VMEM((16, POOL_WIDTH), F32)],
        compiler_params=_params(("parallel", "arbitrary")),
        name="pool_mixer",
    )(proj3, w_mix, scale)


def _soft_cap(a):
    return GATE_SOFTCAP * jnp.tanh(a / GATE_SOFTCAP)


def _log_sigmoid(a):
    return jnp.minimum(a, 0.0) - jnp.log1p(jnp.exp(-jnp.abs(a)))


def _mlstm_kernel(seq_len, q_ref, k_ref, v_ref, o_ref, gate_ref, out_ref, c_ref, n_ref, m_ref):
    h = pl.program_id(1)
    c = pl.program_id(2)
    Lc = SEQ_CHUNK

    @pl.when(c == 0)
    def _():
        c_ref[...] = jnp.zeros_like(c_ref)
        n_ref[...] = jnp.zeros_like(n_ref)
        m_ref[...] = jnp.zeros_like(m_ref)

    row = lax.broadcasted_iota(jnp.int32, (Lc, 1), 0)
    col = lax.broadcasted_iota(jnp.int32, (1, Lc), 1)
    ok_col = (c * Lc + row) < seq_len
    ok_row = (c * Lc + col) < seq_len

    q = jnp.where(ok_col, q_ref[0], 0.0) * (MLSTM_QK_DIM ** -0.5)
    k = jnp.where(ok_col, k_ref[0], 0.0)
    v = jnp.where(ok_col, v_ref[0], 0.0)

    gc = gate_ref[0]
    gt = gc.T
    lane = lax.broadcasted_iota(jnp.int32, (1, LANES), 1)
    sub = lax.broadcasted_iota(jnp.int32, (LANES, 1), 0)
    gi_c = jnp.sum(jnp.where(lane == h, gc, 0.0), axis=1, keepdims=True)
    gf_c = jnp.sum(jnp.where(lane == h + MLSTM_HEADS, gc, 0.0), axis=1, keepdims=True)
    gi_r = jnp.sum(jnp.where(sub == h, gt, 0.0), axis=0, keepdims=True)
    gf_r = jnp.sum(jnp.where(sub == h + MLSTM_HEADS, gt, 0.0), axis=0, keepdims=True)
    i_c = jnp.where(ok_col, _soft_cap(gi_c), 0.0)
    f_c = jnp.where(ok_col, _log_sigmoid(_soft_cap(gf_c)), 0.0)
    i_r = jnp.where(ok_row, _soft_cap(gi_r), 0.0)
    f_r = jnp.where(ok_row, _log_sigmoid(_soft_cap(gf_r)), 0.0)

    causal = col <= row
    tri = causal.astype(BF16)
    fb_c = jnp.broadcast_to(f_c, (Lc, LANES))
    b_c = sum(_dot(tri, p) for p in _split3(fb_c))[:, 0:1]
    fb_r = jnp.broadcast_to(f_r, (8, Lc))
    tri_t = (row <= col).astype(BF16)
    b_r8 = sum(_dot(p, tri_t) for p in _split3(fb_r))
    b_r = b_r8[0:1, :]
    g_tot = b_r8[0:1, Lc - 1:Lc]

    m_prev = m_ref[...]
    dlog = jnp.where(causal, b_c - b_r + i_r, -jnp.inf)
    inter_log = b_c + m_prev
    m_out = jnp.maximum(inter_log, jnp.max(dlog, axis=1, keepdims=True))
    wts = jnp.exp(dlog - m_out)
    inter_w = jnp.exp(inter_log - m_out)

    qb = q.astype(BF16)
    kb = k.astype(BF16)
    vb = v.astype(BF16)
    s = lax.dot_general(qb, kb, (((1,), (1,)), ((), ())), preferred_element_type=F32) * wts
    num = _dot(s.astype(BF16), vb) + inter_w * _dot(qb, c_ref[...].astype(BF16))
    den = jnp.sum(s, axis=1, keepdims=True) + inter_w * jnp.sum(q * n_ref[...], axis=1, keepdims=True)
    hh = num / jnp.maximum(jnp.abs(den), jnp.exp(-m_out))
    out_ref[0] = (jax.nn.sigmoid(o_ref[0]) * hh).astype(out_ref.dtype)

    a_c = g_tot - b_c + i_c
    m_new = jnp.maximum(g_tot + m_prev, jnp.max(a_c, axis=0, keepdims=True))
    wk = k * jnp.exp(a_c - m_new)
    decay = jnp.exp(g_tot + m_prev - m_new)
    c_ref[...] = decay * c_ref[...] + lax.dot_general(
        wk.astype(BF16), vb, (((0,), (0,)), ((), ())), preferred_element_type=F32)
    n_ref[...] = decay * n_ref[...] + jnp.sum(wk, axis=0, keepdims=True)
    m_ref[...] = m_new


def _mlstm(proj3, gates3):
    B, L, _ = proj3.shape
    nc = pl.cdiv(L, SEQ_CHUNK)
    dk, dv, H = MLSTM_QK_DIM, MLSTM_V_DIM, MLSTM_HEADS
    q0 = POOL_WIDTH // dk
    k0 = q0 + H
    v0 = (POOL_WIDTH + 2 * H * dk) // dv
    o0 = v0 + H
    return pl.pallas_call(
        functools.partial(_mlstm_kernel, L),
        grid=(B, H, nc),
        in_specs=[
            pl.BlockSpec((1, SEQ_CHUNK, dk), lambda b, h, c: (b, c, q0 + h)),
            pl.BlockSpec((1, SEQ_CHUNK, dk), lambda b, h, c: (b, c, k0 + h)),
            pl.BlockSpec((1, SEQ_CHUNK, dv), lambda b, h, c: (b, c, v0 + h)),
            pl.BlockSpec((1, SEQ_CHUNK, dv), lambda b, h, c: (b, c, o0 + h)),
            pl.BlockSpec((1, SEQ_CHUNK, LANES), lambda b, h, c: (b, c, 0)),
        ],
        out_specs=pl.BlockSpec((1, SEQ_CHUNK, dv), lambda b, h, c: (b, c, h)),
        out_shape=jax.ShapeDtypeStruct((B, L, MLSTM_WIDTH), BF16),
        scratch_shapes=[
            pltpu.VMEM((dk, dv), F32),
            pltpu.VMEM((1, dk), F32),
            pltpu.VMEM((1, 1), F32),
        ],
        compiler_params=_params(("parallel", "parallel", "arbitrary")),
        name="mlstm",
    )(proj3, proj3, proj3, proj3, gates3)


def _pack_bf16_pair(lo, hi):
    lo_b = lax.bitcast_convert_type(lo.astype(BF16).astype(F32), jnp.uint32)
    hi_b = lax.bitcast_convert_type(hi.astype(BF16).astype(F32), jnp.uint32)
    return (lo_b >> 16) | hi_b


def _unpack_bf16_pair(w):
    lo = lax.bitcast_convert_type(w << 16, F32).astype(BF16)
    hi = lax.bitcast_convert_type(w & jnp.uint32(0xFFFF0000), F32).astype(BF16)
    return lo, hi


def _router_kernel(h_ref, g_ref, wr_ref, br_ref, n_ref, e_ref, p_ref):
    y = _rms(h_ref[...], g_ref[...])
    half = D_MODEL // 2
    n_ref[...] = _pack_bf16_pair(y[:, :half], y[:, half:])
    logits = _dot3_nt(y, wr_ref[...]) + br_ref[...]
    lane = lax.broadcasted_iota(jnp.int32, logits.shape, 1).astype(F32)
    l = jnp.where(lane < N_EXPERTS, logits, -jnp.inf)
    vals, idxs = [], []
    for _ in range(TOP_K):
        m = jnp.max(l, axis=1, keepdims=True)
        idx = jnp.min(jnp.where(l == m, lane, float(LANES)), axis=1, keepdims=True)
        vals.append(m)
        idxs.append(idx)
        l = jnp.where(lane == idx, -jnp.inf, l)
    ex = [jnp.exp(v - vals[0]) for v in vals]
    tot = ex[0] + ex[1] + ex[2] + ex[3]
    e_out = jnp.zeros(logits.shape, F32)
    p_out = jnp.zeros(logits.shape, F32)
    for kk in range(TOP_K):
        e_out = jnp.where(lane == kk, idxs[kk], e_out)
        p_out = jnp.where(lane == kk, ex[kk] / tot, p_out)
    e_ref[...] = e_out.astype(jnp.int32)
    p_ref[...] = p_out


def _router(h, g, wr, br, bm):
    T = h.shape[0]
    return pl.pallas_call(
        _router_kernel,
        grid=(T // bm,),
        in_specs=[
            pl.BlockSpec((bm, D_MODEL), lambda i: (i, 0)),
            pl.BlockSpec((1, D_MODEL), lambda i: (0, 0)),
            pl.BlockSpec((N_EXPERTS, D_MODEL), lambda i: (0, 0)),
            pl.BlockSpec((1, N_EXPERTS), lambda i: (0, 0)),
        ],
        out_specs=[
            pl.BlockSpec((bm, D_MODEL // 2), lambda i: (i, 0)),
            pl.BlockSpec((bm, N_EXPERTS), lambda i: (i, 0)),
            pl.BlockSpec((bm, N_EXPERTS), lambda i: (i, 0)),
        ],
        out_shape=[
            jax.ShapeDtypeStruct((T, D_MODEL // 2), jnp.uint32),
            jax.ShapeDtypeStruct((T, N_EXPERTS), jnp.int32),
            jax.ShapeDtypeStruct((T, N_EXPERTS), F32),
        ],
        compiler_params=_params(("parallel",)),
        name="norm2_router",
    )(h, g, wr, br)


def _row_gather_start(src_hbm, idx_ref, base, n_rows, buf, slot, sem):
    def body(r, carry):
        tok = idx_ref[base + r]
        pltpu.make_async_copy(src_hbm.at[pl.ds(tok, 1), :], buf.at[slot, pl.ds(r, 1), :], sem.at[slot]).start()
        return carry
    lax.fori_loop(0, n_rows, body, 0, unroll=32)


def _row_gather_wait(src_hbm, n_rows, buf, slot, sem):
    pltpu.make_async_copy(src_hbm.at[pl.ds(0, n_rows), :], buf.at[slot], sem.at[slot]).wait()


def _experts_kernel(sbe_ref, rb0_ref, nbk_ref, used_ref, tok_ref,
                    x_hbm, wg_ref, wl_ref, bg_ref, bl_ref, wd_ref, bd_ref,
                    y_hbm,
                    x_res, hmid, ystage, gsem, ysem, ycnt, ypend, gcnt):
    s = pl.program_id(0)
    t = pl.program_id(1)
    ns = pl.num_programs(0)
    nt = pl.num_programs(1)
    nb = nbk_ref[s]
    half = D_MODEL // 2
    n_blocks_total = y_hbm.shape[0] // MOE_BLOCK

    def x_block_copy(tok, r, n):
        return pltpu.make_async_copy(x_hbm.at[pl.ds(tok, n), :], x_res.at[pl.ds(r, n), :], gsem.at[0])

    def gather_row(sb_base, r):
        tok = tok_ref[jnp.minimum(sb_base + r, tok_ref.shape[0] - 1)]
        x_block_copy(tok, r, 1).start()

    def gather_upto(sb, n_rows):
        base = rb0_ref[sb] * MOE_BLOCK

        def body(r, carry):
            gather_row(base, r)
            return carry
        lax.fori_loop(gcnt[0], n_rows, body, 0)
        gcnt[0] = jnp.maximum(gcnt[0], n_rows)

    def gather_some(sb, n):
        base = rb0_ref[sb] * MOE_BLOCK
        cur = gcnt[0]
        for i in range(n):
            gather_row(base, cur + i)
        gcnt[0] = cur + n

    def gather_wait():
        n = gcnt[0]
        n_full = n // MOE_BLOCK

        def block_body(r, carry):
            x_block_copy(0, 0, MOE_BLOCK).wait()
            return carry

        def row_body(r, carry):
            x_block_copy(0, 0, 1).wait()
            return carry
        lax.fori_loop(0, n_full, block_body, 0)
        lax.fori_loop(n_full * MOE_BLOCK, n, row_body, 0)
        gcnt[0] = 0

    def y_copy(slot, piece, row0, col0):
        return pltpu.make_async_copy(
            ystage.at[slot, pl.ds(piece * MOE_BLOCK, MOE_BLOCK), :],
            y_hbm.at[pl.ds(pl.multiple_of(row0, MOE_BLOCK), MOE_BLOCK), pl.ds(pl.multiple_of(col0, DOWN_TILE), DOWN_TILE)],
            ysem.at[slot])

    def y_drain(slot):
        for piece in range(CHUNK_BLOCKS):
            @pl.when(ypend[slot] > piece)
            def _():
                y_copy(slot, 0, 0, 0).wait()
        ypend[slot] = 0

    def y_emit(val, n_pieces, row0, col0):
        slot = lax.rem(ycnt[0], 2)
        y_drain(slot)
        ystage[slot, pl.ds(0, n_pieces * MOE_BLOCK), :] = val
        for piece in range(n_pieces):
            y_copy(slot, piece, row0 + piece * MOE_BLOCK, col0).start()
        ypend[slot] = n_pieces
        ycnt[0] = ycnt[0] + 1

    def for_chunks(n_blocks, fn):
        n_full = n_blocks // CHUNK_BLOCKS

        def body(c, carry):
            fn(c * CHUNK_BLOCKS, CHUNK_BLOCKS)
            return carry
        lax.fori_loop(0, n_full, body, 0)
        rem = n_blocks - n_full * CHUNK_BLOCKS
        for k in range(1, CHUNK_BLOCKS):
            @pl.when(rem == k)
            def _():
                fn(n_full * CHUNK_BLOCKS, k)

    @pl.when(jnp.logical_and(s == 0, t == 0))
    def _():
        ycnt[0] = 0
        ypend[0] = 0
        ypend[1] = 0
        gcnt[0] = 0
        gather_upto(0, nb * MOE_BLOCK)

    @pl.when(t == 0)
    def _():
        gather_wait()

    nxt = jnp.minimum(s + 1, ns - 1)

    @pl.when(jnp.logical_and(nb > 0, t < N_FF_TILES))
    def _():
        bg = bg_ref[0]
        bl = bl_ref[0]
        tt = jnp.minimum(t, N_FF_TILES - 1)

        def chunk(b0, k):
            rows = pl.ds(pl.multiple_of(b0 * MOE_BLOCK, MOE_BLOCK), k * MOE_BLOCK)
            x_lo, x_hi = _unpack_bf16_pair(x_res[rows, :])
            gate = _dotw(x_lo, wg_ref[0, :half, :]) + _dotw(x_hi, wg_ref[0, half:, :]) + bg
            lin = _dotw(x_lo, wl_ref[0, :half, :]) + _dotw(x_hi, wl_ref[0, half:, :]) + bl
            gate = jnp.minimum(gate, SWIGLU_LIMIT)
            lin = jnp.clip(lin, -SWIGLU_LIMIT, SWIGLU_LIMIT)
            hmid[tt, rows, :] = (gate * jax.nn.sigmoid(SWIGLU_ALPHA * gate) * (lin + 1.0)).astype(BF16)
        for_chunks(nb, chunk)

    @pl.when(jnp.logical_and(nb > 0, t >= N_FF_TILES))
    def _():
        bd = bd_ref[0]
        col0 = (t - N_FF_TILES) * DOWN_TILE
        row_base = rb0_ref[s] * MOE_BLOCK

        def chunk(b0, k):
            gather_some(nxt, k * (MOE_BLOCK // N_DOWN_TILES))
            rows = pl.ds(pl.multiple_of(b0 * MOE_BLOCK, MOE_BLOCK), k * MOE_BLOCK)
            acc = bd + _dotw(hmid[0, rows, :], wd_ref[0, 0:FF_TILE, :])
            for j in range(1, N_FF_TILES):
                acc = acc + _dotw(hmid[j, rows, :], wd_ref[0, j * FF_TILE:(j + 1) * FF_TILE, :])
            y_emit(acc, k, row_base + b0 * MOE_BLOCK, col0)
        for_chunks(nb, chunk)

        @pl.when(t == nt - 1)
        def _():
            gather_upto(nxt, jnp.where(s + 1 < ns, nbk_ref[nxt] * MOE_BLOCK, 0))

    @pl.when(jnp.logical_and(s == ns - 1, t == nt - 1))
    def _():
        gather_wait()
        y_drain(0)
        y_drain(1)
        ystage[0, pl.ds(0, MOE_BLOCK), :] = jnp.zeros((MOE_BLOCK, DOWN_TILE), F32)

        def zero_copy(b, j):
            return y_copy(0, 0, b * MOE_BLOCK, j * DOWN_TILE)

        def start_body(b, carry):
            for j in range(N_DOWN_TILES):
                zero_copy(b, j).start()
            return carry

        def wait_body(b, carry):
            for j in range(N_DOWN_TILES):
                zero_copy(0, 0).wait()
            return carry
        lax.fori_loop(used_ref[0], n_blocks_total, start_body, 0)
        lax.fori_loop(used_ref[0], n_blocks_total, wait_body, 0)


def _experts(sb_e, sb_rb0, sb_nb, n_used, row_tok, n2p, w_gu, b_gu, w_down, b_down):
    R = row_tok.shape[0]
    S = sb_e.shape[0]
    sb_rows = SB_BLOCKS * MOE_BLOCK

    def ff_tile(s, t, nbk):
        return jnp.where(nbk[s] > 0, jnp.minimum(t, N_FF_TILES - 1), N_FF_TILES - 1)

    def down_tile(s, t, nbk):
        return jnp.where(nbk[s] > 0, jnp.maximum(t - N_FF_TILES, 0), N_DOWN_TILES - 1)

    grid_spec = pltpu.PrefetchScalarGridSpec(
        num_scalar_prefetch=5,
        grid=(S, N_FF_TILES + N_DOWN_TILES),
        in_specs=[
            pl.BlockSpec(memory_space=pl.ANY),
            pl.BlockSpec((1, D_MODEL, FF_TILE), lambda s, t, e, r0, nbk, u, tok: (e[s], 0, ff_tile(s, t, nbk))),
            pl.BlockSpec((1, D_MODEL, FF_TILE), lambda s, t, e, r0, nbk, u, tok: (e[s], 0, N_FF_TILES + ff_tile(s, t, nbk))),
            pl.BlockSpec((1, 1, FF_TILE), lambda s, t, e, r0, nbk, u, tok: (e[s], 0, ff_tile(s, t, nbk))),
            pl.BlockSpec((1, 1, FF_TILE), lambda s, t, e, r0, nbk, u, tok: (e[s], 0, N_FF_TILES + ff_tile(s, t, nbk))),
            pl.BlockSpec((1, D_FF, DOWN_TILE), lambda s, t, e, r0, nbk, u, tok: (e[s], 0, down_tile(s, t, nbk))),
            pl.BlockSpec((1, 1, DOWN_TILE), lambda s, t, e, r0, nbk, u, tok: (e[s], 0, down_tile(s, t, nbk))),
        ],
        out_specs=pl.BlockSpec(memory_space=pl.ANY),
        scratch_shapes=[
            pltpu.VMEM((sb_rows, D_MODEL // 2), jnp.uint32),
            pltpu.VMEM((N_FF_TILES, sb_rows, FF_TILE), BF16),
            pltpu.VMEM((2, CHUNK_BLOCKS * MOE_BLOCK, DOWN_TILE), F32),
            pltpu.SemaphoreType.DMA((1,)),
            pltpu.SemaphoreType.DMA((2,)),
            pltpu.SMEM((1,), jnp.int32),
            pltpu.SMEM((2,), jnp.int32),
            pltpu.SMEM((1,), jnp.int32),
        ],
    )
    return pl.pallas_call(
        _experts_kernel,
        grid_spec=grid_spec,
        out_shape=jax.ShapeDtypeStruct((R, D_MODEL), F32),
        compiler_params=_params(("arbitrary", "arbitrary")),
        name="moe_experts",
    )(sb_e, sb_rb0, sb_nb, n_used, row_tok, n2p, w_gu, w_gu, b_gu, b_gu, w_down, b_down)


def _combine_kernel(tb, dest_ref, y_hbm, h_ref, p_ref, g_ref, o_ref, buf, sem):
    i = pl.program_id(0)
    nb = pl.num_programs(0)
    slot = lax.rem(i, 2)
    rows = tb * TOP_K

    @pl.when(i == 0)
    def _():
        _row_gather_start(y_hbm, dest_ref, 0, rows, buf, 0, sem)

    @pl.when(i + 1 < nb)
    def _():
        _row_gather_start(y_hbm, dest_ref, (i + 1) * rows, rows, buf, 1 - slot, sem)

    _row_gather_wait(y_hbm, rows, buf, slot, sem)
    acc = h_ref[...]
    p = p_ref[...]
    for kk in range(TOP_K):
        acc = acc + p[:, kk:kk + 1] * buf[slot, pl.ds(kk * tb, tb), :]
    o_ref[...] = _rms(acc, g_ref[...])


def _combine(dest_km, y, h, probs, g, tb):
    T = h.shape[0]
    grid_spec = pltpu.PrefetchScalarGridSpec(
        num_scalar_prefetch=1,
        grid=(T // tb,),
        in_specs=[
            pl.BlockSpec(memory_space=pl.ANY),
            pl.BlockSpec((tb, D_MODEL), lambda i, d: (i, 0)),
            pl.BlockSpec((tb, N_EXPERTS), lambda i, d: (i, 0)),
            pl.BlockSpec((1, D_MODEL), lambda i, d: (0, 0)),
        ],
        out_specs=pl.BlockSpec((tb, D_MODEL), lambda i, d: (i, 0)),
        scratch_shapes=[
            pltpu.VMEM((2, tb * TOP_K, D_MODEL), F32),
            pltpu.SemaphoreType.DMA((2,)),
        ],
    )
    return pl.pallas_call(
        functools.partial(_combine_kernel, tb),
        grid_spec=grid_spec,
        out_shape=jax.ShapeDtypeStruct((T, D_MODEL), F32),
        compiler_params=_params(("arbitrary",)),
        name="moe_combine_norm",
    )(dest_km, y, h, probs, g)


def _routing(top_e, tb):
    T = top_e.shape[0]
    A = T * TOP_K
    n_blocks = -(-(A + N_EXPERTS * (MOE_BLOCK - 1)) // MOE_BLOCK)
    R = n_blocks * MOE_BLOCK
    n_sb = N_EXPERTS + n_blocks // SB_BLOCKS
    i32 = jnp.int32
    e_flat = top_e.reshape(A)
    onehot = (e_flat[:, None] == jnp.arange(N_EXPERTS, dtype=i32)[None, :]).astype(i32)
    csum = jnp.cumsum(onehot, axis=0)
    rank = jnp.sum(csum * onehot, axis=1) - 1
    counts = csum[-1]
    blocks = (counts + MOE_BLOCK - 1) // MOE_BLOCK
    blk_end = jnp.cumsum(blocks)
    blk_start = blk_end - blocks
    dest = (blk_start * MOE_BLOCK)[e_flat] + rank
    row_tok = jnp.zeros((R,), i32).at[dest].set(jnp.arange(A, dtype=i32) // TOP_K)
    dest_km = dest.reshape(T // tb, tb, TOP_K).transpose(0, 2, 1).reshape(A).astype(i32)

    sbs = (blocks + SB_BLOCKS - 1) // SB_BLOCKS
    sb_end = jnp.cumsum(sbs)
    sb_start = sb_end - sbs
    s_idx = jnp.arange(n_sb, dtype=i32)
    active = s_idx < sb_end[-1]
    e_s = jnp.minimum(jnp.sum((sb_end[None, :] <= s_idx[:, None]).astype(i32), axis=1), N_EXPERTS - 1)
    k_s = s_idx - sb_start[e_s]
    sb_rb0 = jnp.where(active, blk_start[e_s] + k_s * SB_BLOCKS, 0)
    sb_nb = jnp.where(active, jnp.minimum(blocks[e_s] - k_s * SB_BLOCKS, SB_BLOCKS), 0)
    sb_e = jnp.where(active, e_s, jnp.max(jnp.where(active, e_s, 0)))
    n_used = blk_end[-1:].astype(i32)
    return row_tok, dest_km, sb_e.astype(i32), sb_rb0.astype(i32), sb_nb.astype(i32), n_used


def kernel(x, meta_tokens, norm1_g, w_in, b_igate, b_fgate, w_pool_mix, pool_scale, w_out, norm2_g, w_router,
           b_router, w_gu, b_gu, w_down, b_down, norm_f_g):
    B, S, D = x.shape
    L = N_META + S
    T = B * L
    H = MLSTM_HEADS
    BM_IN = 688
    BM_OUT = 688
    BM_NORM = 192
    TB = 64
    assert T % BM_IN == 0 and T % BM_OUT == 0 and T % BM_NORM == 0 and T % TB == 0
    assert w_in.shape[0] == 1

    meta = jnp.broadcast_to(meta_tokens[None].astype(x.dtype), (B, N_META, D))
    h0 = jnp.concatenate([meta, x], axis=1).reshape(T, D)

    l = 0
    w_in_t = w_in.reshape(D, PROJ_COLS + 2 * H).T
    bg = jnp.pad(jnp.concatenate([b_igate[l], b_fgate[l]]), (0, LANES - 2 * H)).reshape(1, LANES)
    n1, gates = _norm1(h0, norm1_g[l].reshape(1, D), w_in_t, bg, BM_NORM)

    proj3 = _inproj(n1, w_in_t, PROJ_COLS, BM_IN, 1024).reshape(B, L, PROJ_COLS)
    pool_out = _pool(proj3, w_pool_mix[l].astype(BF16), pool_scale[l].reshape(1, POOL_WIDTH))
    mlstm_out = _mlstm(proj3, gates.reshape(B, L, LANES))

    h1 = _outproj(pool_out.reshape(T, POOL_WIDTH), mlstm_out.reshape(T, MLSTM_WIDTH),
                  w_out.reshape(D, D), h0, BM_OUT, 512)

    n2p, top_e, probs = _router(h1, norm2_g[l].reshape(1, D), w_router.reshape(D, N_EXPERTS).T,
                                b_router.reshape(1, N_EXPERTS), BM_NORM)

    row_tok, dest_km, sb_e, sb_rb0, sb_nb, n_used = _routing(top_e[:, :TOP_K], TB)
    y = _experts(sb_e, sb_rb0, sb_nb, n_used, row_tok, n2p,
                 w_gu.reshape(N_EXPERTS, D, 2 * D_FF), b_gu.reshape(N_EXPERTS, 1, 2 * D_FF),
                 w_down.reshape(N_EXPERTS, D_FF, D), b_down.reshape(N_EXPERTS, 1, D))
    out = _combine(dest_km, y, h1, probs, norm_f_g.reshape(1, D), TB)
    return out.reshape(B, L, D)[:, N_META:]
```

```python
import functools

import jax
import jax.numpy as jnp
from jax import lax
from jax.experimental import pallas as pl
from jax.experimental.pallas import tpu as pltpu

D_MODEL = 4096
N_META = 16
POOL_WINDOWS = (2, 4, 8, 16)
POOL_WIDTH = D_MODEL // 4
POOL_GROUP = POOL_WIDTH // len(POOL_WINDOWS)
MLSTM_WIDTH = D_MODEL - POOL_WIDTH
MLSTM_HEADS = 6
MLSTM_V_DIM = MLSTM_WIDTH // MLSTM_HEADS
MLSTM_QK_DIM = MLSTM_V_DIM // 2
GATE_SOFTCAP = 15.0
N_EXPERTS = 32
TOP_K = 4
D_FF = D_MODEL // 2
SWIGLU_ALPHA = 1.702
SWIGLU_LIMIT = 7.0
MOE_BLOCK = 128
EPS = 1e-6

LANES = 128
SEQ_CHUNK = 256
PROJ_COLS = POOL_WIDTH + 2 * MLSTM_HEADS * MLSTM_QK_DIM + 2 * MLSTM_WIDTH
VMEM_LIMIT = 56 * 1024 * 1024
SB_BLOCKS = 10
CHUNK_BLOCKS = 3
FF_TILE = 256
DOWN_TILE = 1024
N_FF_TILES = D_FF // FF_TILE
N_DOWN_TILES = D_MODEL // DOWN_TILE

F32 = jnp.float32
BF16 = jnp.bfloat16


def _params(sem, vmem=VMEM_LIMIT):
    return pltpu.CompilerParams(dimension_semantics=sem, vmem_limit_bytes=vmem)


def _split3(a):
    hi = a.astype(BF16)
    r1 = a - hi.astype(F32)
    mid = r1.astype(BF16)
    lo = (r1 - mid.astype(F32)).astype(BF16)
    return hi, mid, lo


def _dot(a, b):
    return jnp.dot(a, b, preferred_element_type=F32)


def _dotw(a, w):
    return lax.dot_general(a, w, (((1,), (0,)), ((), ())), preferred_element_type=F32)


def _dot_nt(a, wt):
    return lax.dot_general(a, wt, (((1,), (1,)), ((), ())), preferred_element_type=F32)


def _dot3_nt(a, wt):
    a_hi = a.astype(BF16)
    a_lo = (a - a_hi.astype(F32)).astype(BF16)
    w_hi = wt.astype(BF16)
    w_lo = (wt - w_hi.astype(F32)).astype(BF16)
    return _dot_nt(a_hi, w_hi) + (_dot_nt(a_hi, w_lo) + _dot_nt(a_lo, w_hi))


def _rms(x, g):
    return x * lax.rsqrt(jnp.mean(x * x, axis=-1, keepdims=True) + EPS) * g


GATE_ROWS = 16


def _norm1_kernel(h_ref, g_ref, wg_ref, bg_ref, n_ref, gate_ref):
    y = _rms(h_ref[...], g_ref[...])
    n_ref[...] = y.astype(BF16)
    sub = lax.broadcasted_iota(jnp.int32, (GATE_ROWS, D_MODEL), 0)
    wg = jnp.where(sub < 2 * MLSTM_HEADS, wg_ref[...], 0.0)
    wg = jnp.concatenate([wg, jnp.zeros((LANES - GATE_ROWS, D_MODEL), F32)], axis=0)
    gate_ref[...] = _dot3_nt(y, wg) + bg_ref[...]


def _norm1(h, g, w_in_t, bg, bm):
    T = h.shape[0]
    return pl.pallas_call(
        _norm1_kernel,
        grid=(T // bm,),
        in_specs=[
            pl.BlockSpec((bm, D_MODEL), lambda i: (i, 0)),
            pl.BlockSpec((1, D_MODEL), lambda i: (0, 0)),
            pl.BlockSpec((GATE_ROWS, D_MODEL), lambda i: (PROJ_COLS // GATE_ROWS, 0)),
            pl.BlockSpec((1, LANES), lambda i: (0, 0)),
        ],
        out_specs=[
            pl.BlockSpec((bm, D_MODEL), lambda i: (i, 0)),
            pl.BlockSpec((bm, LANES), lambda i: (i, 0)),
        ],
        out_shape=[
            jax.ShapeDtypeStruct((T, D_MODEL), BF16),
            jax.ShapeDtypeStruct((T, LANES), F32),
        ],
        compiler_params=_params(("parallel",)),
        name="norm1_gates",
    )(h, g, w_in_t, bg)


def _inproj_kernel(x_ref, wt_ref, o_ref):
    o_ref[...] = _dot_nt(x_ref[...], wt_ref[...]).astype(o_ref.dtype)


def _inproj(x, wt, n_cols, bm, bn):
    M, K = x.shape
    return pl.pallas_call(
        _inproj_kernel,
        grid=(n_cols // bn, M // bm),
        in_specs=[
            pl.BlockSpec((bm, K), lambda j, i: (i, 0)),
            pl.BlockSpec((bn, K), lambda j, i: (j, 0)),
        ],
        out_specs=pl.BlockSpec((bm, bn), lambda j, i: (i, j)),
        out_shape=jax.ShapeDtypeStruct((M, n_cols), F32),
        compiler_params=_params(("parallel", "parallel")),
        name="in_proj",
    )(x, wt)


def _outproj_kernel(p_ref, m_ref, w_ref, h_ref, o_ref):
    o_ref[...] = h_ref[...] + (_dotw(p_ref[...], w_ref[:POOL_WIDTH, :]) + _dotw(m_ref[...], w_ref[POOL_WIDTH:, :]))


def _outproj(p, m, w, h, bm, bn):
    M = p.shape[0]
    K, N = w.shape
    return pl.pallas_call(
        _outproj_kernel,
        grid=(N // bn, M // bm),
        in_specs=[
            pl.BlockSpec((bm, POOL_WIDTH), lambda j, i: (i, 0)),
            pl.BlockSpec((bm, MLSTM_WIDTH), lambda j, i: (i, 0)),
            pl.BlockSpec((K, bn), lambda j, i: (0, j)),
            pl.BlockSpec((bm, bn), lambda j, i: (i, j)),
        ],
        out_specs=pl.BlockSpec((bm, bn), lambda j, i: (i, j)),
        out_shape=jax.ShapeDtypeStruct((M, N), F32),
        compiler_params=_params(("parallel", "parallel")),
        name="out_proj",
    )(p, m, w, h)


def _pool_kernel(u_ref, w_ref, s_ref, o_ref, carry_ref):
    c = pl.program_id(1)

    @pl.when(c == 0)
    def _():
        carry_ref[...] = jnp.zeros_like(carry_ref)

    u = u_ref[0]
    ext = jnp.concatenate([carry_ref[...], u], axis=0)
    carry_ref[...] = u[SEQ_CHUNK - 16:, :]
    pos = c * SEQ_CHUNK + lax.broadcasted_iota(jnp.int32, (SEQ_CHUNK, 1), 0)
    for g, win in enumerate(POOL_WINDOWS):
        cols = slice(g * POOL_GROUP, (g + 1) * POOL_GROUP)
        s = ext[:, cols]
        span = 1
        while span < win:
            s = s + pltpu.roll(s, span, axis=0)
            span *= 2
        cnt = jnp.minimum(pos + 1, win).astype(F32)
        d = s[16:, :] / cnt - u[:, cols]
        y = _dot(d.astype(BF16), w_ref[g]) * s_ref[:, cols]
        o_ref[0, :, cols] = y.astype(o_ref.dtype)


def _pool(proj3, w_mix, scale):
    B, L, _ = proj3.shape
    nc = pl.cdiv(L, SEQ_CHUNK)
    return pl.pallas_call(
        _pool_kernel,
        grid=(B, nc),
        in_specs=[
            pl.BlockSpec((1, SEQ_CHUNK, POOL_WIDTH), lambda b, c: (b, c, 0)),
            pl.BlockSpec((len(POOL_WINDOWS), POOL_GROUP, POOL_GROUP), lambda b, c: (0, 0, 0)),
            pl.BlockSpec((1, POOL_WIDTH), lambda b, c: (0, 0)),
        ],
        out_specs=pl.BlockSpec((1, SEQ_CHUNK, POOL_WIDTH), lambda b, c: (b, c, 0)),
        out_shape=jax.ShapeDtypeStruct((B, L, POOL_WIDTH), BF16),
        scratch_shapes=[pltpu.VMEM((16, POOL_WIDTH), F32)],
        compiler_params=_params(("parallel", "arbitrary")),
        name="pool_mixer",
    )(proj3, w_mix, scale)


def _soft_cap(a):
    return GATE_SOFTCAP * jnp.tanh(a / GATE_SOFTCAP)


def _log_sigmoid(a):
    return jnp.minimum(a, 0.0) - jnp.log1p(jnp.exp(-jnp.abs(a)))


def _mlstm_kernel(seq_len, q_ref, k_ref, v_ref, o_ref, gate_ref, out_ref, c_ref, n_ref, m_ref):
    h = pl.program_id(1)
    c = pl.program_id(2)
    Lc = SEQ_CHUNK

    @pl.when(c == 0)
    def _():
        c_ref[...] = jnp.zeros_like(c_ref)
        n_ref[...] = jnp.zeros_like(n_ref)
        m_ref[...] = jnp.zeros_like(m_ref)

    row = lax.broadcasted_iota(jnp.int32, (Lc, 1), 0)
    col = lax.broadcasted_iota(jnp.int32, (1, Lc), 1)
    ok_col = (c * Lc + row) < seq_len
    ok_row = (c * Lc + col) < seq_len

    q = jnp.where(ok_col, q_ref[0], 0.0) * (MLSTM_QK_DIM ** -0.5)
    k = jnp.where(ok_col, k_ref[0], 0.0)
    v = jnp.where(ok_col, v_ref[0], 0.0)

    gc = gate_ref[0]
    gt = gc.T
    lane = lax.broadcasted_iota(jnp.int32, (1, LANES), 1)
    sub = lax.broadcasted_iota(jnp.int32, (LANES, 1), 0)
    gi_c = jnp.sum(jnp.where(lane == h, gc, 0.0), axis=1, keepdims=True)
    gf_c = jnp.sum(jnp.where(lane == h + MLSTM_HEADS, gc, 0.0), axis=1, keepdims=True)
    gi_r = jnp.sum(jnp.where(sub == h, gt, 0.0), axis=0, keepdims=True)
    gf_r = jnp.sum(jnp.where(sub == h + MLSTM_HEADS, gt, 0.0), axis=0, keepdims=True)
    i_c = jnp.where(ok_col, _soft_cap(gi_c), 0.0)
    f_c = jnp.where(ok_col, _log_sigmoid(_soft_cap(gf_c)), 0.0)
    i_r = jnp.where(ok_row, _soft_cap(gi_r), 0.0)
    f_r = jnp.where(ok_row, _log_sigmoid(_soft_cap(gf_r)), 0.0)

    causal = col <= row
    tri = causal.astype(BF16)
    fb_c = jnp.broadcast_to(f_c, (Lc, LANES))
    b_c = sum(_dot(tri, p) for p in _split3(fb_c))[:, 0:1]
    fb_r = jnp.broadcast_to(f_r, (8, Lc))
    tri_t = (row <= col).astype(BF16)
    b_r8 = sum(_dot(p, tri_t) for p in _split3(fb_r))
    b_r = b_r8[0:1, :]
    g_tot = b_r8[0:1, Lc - 1:Lc]

    m_prev = m_ref[...]
    dlog = jnp.where(causal, b_c - b_r + i_r, -jnp.inf)
    inter_log = b_c + m_prev
    m_out = jnp.maximum(inter_log, jnp.max(dlog, axis=1, keepdims=True))
    wts = jnp.exp(dlog - m_out)
    inter_w = jnp.exp(inter_log - m_out)

    qb = q.astype(BF16)
    kb = k.astype(BF16)
    vb = v.astype(BF16)
    s = lax.dot_general(qb, kb, (((1,), (1,)), ((), ())), preferred_element_type=F32) * wts
    num = _dot(s.astype(BF16), vb) + inter_w * _dot(qb, c_ref[...].astype(BF16))
    den = jnp.sum(s, axis=1, keepdims=True) + inter_w * jnp.sum(q * n_ref[...], axis=1, keepdims=True)
    hh = num / jnp.maximum(jnp.abs(den), jnp.exp(-m_out))
    out_ref[0] = (jax.nn.sigmoid(o_ref[0]) * hh).astype(out_ref.dtype)

    a_c = g_tot - b_c + i_c
    m_new = jnp.maximum(g_tot + m_prev, jnp.max(a_c, axis=0, keepdims=True))
    wk = k * jnp.exp(a_c - m_new)
    decay = jnp.exp(g_tot + m_prev - m_new)
    c_ref[...] = decay * c_ref[...] + lax.dot_general(
        wk.astype(BF16), vb, (((0,), (0,)), ((), ())), preferred_element_type=F32)
    n_ref[...] = decay * n_ref[...] + jnp.sum(wk, axis=0, keepdims=True)
    m_ref[...] = m_new


def _mlstm(proj3, gates3):
    B, L, _ = proj3.shape
    nc = pl.cdiv(L, SEQ_CHUNK)
    dk, dv, H = MLSTM_QK_DIM, MLSTM_V_DIM, MLSTM_HEADS
    q0 = POOL_WIDTH // dk
    k0 = q0 + H
    v0 = (POOL_WIDTH + 2 * H * dk) // dv
    o0 = v0 + H
    return pl.pallas_call(
        functools.partial(_mlstm_kernel, L),
        grid=(B, H, nc),
        in_specs=[
            pl.BlockSpec((1, SEQ_CHUNK, dk), lambda b, h, c: (b, c, q0 + h)),
            pl.BlockSpec((1, SEQ_CHUNK, dk), lambda b, h, c: (b, c, k0 + h)),
            pl.BlockSpec((1, SEQ_CHUNK, dv), lambda b, h, c: (b, c, v0 + h)),
            pl.BlockSpec((1, SEQ_CHUNK, dv), lambda b, h, c: (b, c, o0 + h)),
            pl.BlockSpec((1, SEQ_CHUNK, LANES), lambda b, h, c: (b, c, 0)),
        ],
        out_specs=pl.BlockSpec((1, SEQ_CHUNK, dv), lambda b, h, c: (b, c, h)),
        out_shape=jax.ShapeDtypeStruct((B, L, MLSTM_WIDTH), BF16),
        scratch_shapes=[
            pltpu.VMEM((dk, dv), F32),
            pltpu.VMEM((1, dk), F32),
            pltpu.VMEM((1, 1), F32),
        ],
        compiler_params=_params(("parallel", "parallel", "arbitrary")),
        name="mlstm",
    )(proj3, proj3, proj3, proj3, gates3)


def _pack_bf16_pair(lo, hi):
    lo_b = lax.bitcast_convert_type(lo.astype(BF16).astype(F32), jnp.uint32)
    hi_b = lax.bitcast_convert_type(hi.astype(BF16).astype(F32), jnp.uint32)
    return (lo_b >> 16) | hi_b


def _unpack_bf16_pair(w):
    lo = lax.bitcast_convert_type(w << 16, F32).astype(BF16)
    hi = lax.bitcast_convert_type(w & jnp.uint32(0xFFFF0000), F32).astype(BF16)
    return lo, hi


def _router_kernel(h_ref, g_ref, wr_ref, br_ref, n_ref, e_ref, p_ref):
    y = _rms(h_ref[...], g_ref[...])
    half = D_MODEL // 2
    n_ref[...] = _pack_bf16_pair(y[:, :half], y[:, half:])
    logits = _dot3_nt(y, wr_ref[...]) + br_ref[...]
    lane = lax.broadcasted_iota(jnp.int32, logits.shape, 1).astype(F32)
    l = jnp.where(lane < N_EXPERTS, logits, -jnp.inf)
    vals, idxs = [], []
    for _ in range(TOP_K):
        m = jnp.max(l, axis=1, keepdims=True)
        idx = jnp.min(jnp.where(l == m, lane, float(LANES)), axis=1, keepdims=True)
        vals.append(m)
        idxs.append(idx)
        l = jnp.where(lane == idx, -jnp.inf, l)
    ex = [jnp.exp(v - vals[0]) for v in vals]
    tot = ex[0] + ex[1] + ex[2] + ex[3]
    e_out = jnp.zeros(logits.shape, F32)
    p_out = jnp.zeros(logits.shape, F32)
    for kk in range(TOP_K):
        e_out = jnp.where(lane == kk, idxs[kk], e_out)
        p_out = jnp.where(lane == kk, ex[kk] / tot, p_out)
    e_ref[...] = e_out.astype(jnp.int32)
    p_ref[...] = p_out


def _router(h, g, wr, br, bm):
    T = h.shape[0]
    return pl.pallas_call(
        _router_kernel,
        grid=(T // bm,),
        in_specs=[
            pl.BlockSpec((bm, D_MODEL), lambda i: (i, 0)),
            pl.BlockSpec((1, D_MODEL), lambda i: (0, 0)),
            pl.BlockSpec((N_EXPERTS, D_MODEL), lambda i: (0, 0)),
            pl.BlockSpec((1, N_EXPERTS), lambda i: (0, 0)),
        ],
        out_specs=[
            pl.BlockSpec((bm, D_MODEL // 2), lambda i: (i, 0)),
            pl.BlockSpec((bm, N_EXPERTS), lambda i: (i, 0)),
            pl.BlockSpec((bm, N_EXPERTS), lambda i: (i, 0)),
        ],
        out_shape=[
            jax.ShapeDtypeStruct((T, D_MODEL // 2), jnp.uint32),
            jax.ShapeDtypeStruct((T, N_EXPERTS), jnp.int32),
            jax.ShapeDtypeStruct((T, N_EXPERTS), F32),
        ],
        compiler_params=_params(("parallel",)),
        name="norm2_router",
    )(h, g, wr, br)


def _row_gather_start(src_hbm, idx_ref, base, n_rows, buf, slot, sem):
    def body(r, carry):
        tok = idx_ref[base + r]
        pltpu.make_async_copy(src_hbm.at[pl.ds(tok, 1), :], buf.at[slot, pl.ds(r, 1), :], sem.at[slot]).start()
        return carry
    lax.fori_loop(0, n_rows, body, 0, unroll=32)


def _row_gather_wait(src_hbm, n_rows, buf, slot, sem):
    pltpu.make_async_copy(src_hbm.at[pl.ds(0, n_rows), :], buf.at[slot], sem.at[slot]).wait()


def _experts_kernel(sbe_ref, rb0_ref, nbk_ref, used_ref, tok_ref,
                    x_hbm, wgu_hbm, wd_hbm, bgu_ref, bd_ref,
                    y_hbm,
                    x_res, hmid, wgl_buf, wd_buf, ystage, gsem, wsem, dsem, ysem, ycnt, ypend, gcnt):
    s = pl.program_id(0)
    ns = pl.num_programs(0)
    nb = nbk_ref[s]
    expert = sbe_ref[s]
    nxt = jnp.minimum(s + 1, ns - 1)
    next_active = jnp.logical_and(s + 1 < ns, nbk_ref[nxt] > 0)
    half = D_MODEL // 2
    n_blocks_total = y_hbm.shape[0] // MOE_BLOCK

    def gateup_copies(e, j, slot):
        gate_cols = pl.ds(pl.multiple_of(j * FF_TILE, FF_TILE), FF_TILE)
        lin_cols = pl.ds(pl.multiple_of(D_FF + j * FF_TILE, FF_TILE), FF_TILE)
        return (pltpu.make_async_copy(wgu_hbm.at[e, :, gate_cols], wgl_buf.at[slot, 0], wsem.at[slot]),
                pltpu.make_async_copy(wgu_hbm.at[e, :, lin_cols], wgl_buf.at[slot, 1], wsem.at[slot]))

    def down_copy(e, j, slot):
        cols = pl.ds(pl.multiple_of(j * DOWN_TILE, DOWN_TILE), DOWN_TILE)
        return pltpu.make_async_copy(wd_hbm.at[e, :, cols], wd_buf.at[slot], dsem.at[slot])

    def x_block_copy(tok, r, n):
        return pltpu.make_async_copy(x_hbm.at[pl.ds(tok, n), :], x_res.at[pl.ds(r, n), :], gsem.at[0])

    def gather_row(sb_base, r):
        tok = tok_ref[jnp.minimum(sb_base + r, tok_ref.shape[0] - 1)]
        x_block_copy(tok, r, 1).start()

    def gather_upto(sb, n_rows):
        base = rb0_ref[sb] * MOE_BLOCK

        def body(r, carry):
            gather_row(base, r)
            return carry
        lax.fori_loop(gcnt[0], n_rows, body, 0)
        gcnt[0] = jnp.maximum(gcnt[0], n_rows)

    def gather_some(sb, n):
        base = rb0_ref[sb] * MOE_BLOCK
        cur = gcnt[0]
        for i in range(n):
            gather_row(base, cur + i)
        gcnt[0] = cur + n

    def gather_wait():
        n = gcnt[0]
        n_full = n // MOE_BLOCK

        def block_body(r, carry):
            x_block_copy(0, 0, MOE_BLOCK).wait()
            return carry

        def row_body(r, carry):
            x_block_copy(0, 0, 1).wait()
            return carry
        lax.fori_loop(0, n_full, block_body, 0)
        lax.fori_loop(n_full * MOE_BLOCK, n, row_body, 0)
        gcnt[0] = 0

    def y_copy(slot, piece, row0, col0):
        return pltpu.make_async_copy(
            ystage.at[slot, pl.ds(piece * MOE_BLOCK, MOE_BLOCK), :],
            y_hbm.at[pl.ds(pl.multiple_of(row0, MOE_BLOCK), MOE_BLOCK), pl.ds(pl.multiple_of(col0, DOWN_TILE), DOWN_TILE)],
            ysem.at[slot])

    def y_drain(slot):
        for piece in range(CHUNK_BLOCKS):
            @pl.when(ypend[slot] > piece)
            def _():
                y_copy(slot, 0, 0, 0).wait()
        ypend[slot] = 0

    def y_emit(val, n_pieces, row0, col0):
        slot = lax.rem(ycnt[0], 2)
        y_drain(slot)
        ystage[slot, pl.ds(0, n_pieces * MOE_BLOCK), :] = val
        for piece in range(n_pieces):
            y_copy(slot, piece, row0 + piece * MOE_BLOCK, col0).start()
        ypend[slot] = n_pieces
        ycnt[0] = ycnt[0] + 1

    def for_chunks(n_blocks, fn):
        pair = 2 * CHUNK_BLOCKS
        n_pairs = n_blocks // pair

        def body(c, carry):
            fn(c * pair, CHUNK_BLOCKS)
            fn(c * pair + CHUNK_BLOCKS, CHUNK_BLOCKS)
            return carry
        lax.fori_loop(0, n_pairs, body, 0)
        done = n_pairs * pair
        rem = n_blocks - done

        @pl.when(rem >= CHUNK_BLOCKS)
        def _():
            fn(done, CHUNK_BLOCKS)
        tail = jnp.where(rem >= CHUNK_BLOCKS, rem - CHUNK_BLOCKS, rem)
        tail0 = jnp.where(rem >= CHUNK_BLOCKS, done + CHUNK_BLOCKS, done)
        for k in range(1, CHUNK_BLOCKS):
            @pl.when(tail == k)
            def _():
                fn(tail0, k)

    @pl.when(s == 0)
    def _():
        ycnt[0] = 0
        ypend[0] = 0
        ypend[1] = 0
        gcnt[0] = 0

        @pl.when(nb > 0)
        def _():
            for c in gateup_copies(expert, 0, 0):
                c.start()
        gather_upto(0, nb * MOE_BLOCK)

    @pl.when(nb > 0)
    def _():
        gather_wait()
        row_base = rb0_ref[s] * MOE_BLOCK

        def gateup_tile(j, carry):
            slot = lax.rem(j, 2)
            for c in gateup_copies(expert, j, slot):
                c.wait()

            @pl.when(j + 1 < N_FF_TILES)
            def _():
                for c in gateup_copies(expert, j + 1, 1 - slot):
                    c.start()

            @pl.when(j + 1 == N_FF_TILES)
            def _():
                down_copy(expert, 0, 0).start()

            bg = bgu_ref[0, pl.ds(j, 1), :]
            bl = bgu_ref[0, pl.ds(N_FF_TILES + j, 1), :]

            def chunk(b0, k):
                rows = pl.ds(pl.multiple_of(b0 * MOE_BLOCK, MOE_BLOCK), k * MOE_BLOCK)
                x_lo, x_hi = _unpack_bf16_pair(x_res[rows, :])
                gate = _dotw(x_lo, wgl_buf[slot, 0, :half, :]) + _dotw(x_hi, wgl_buf[slot, 0, half:, :]) + bg
                lin = _dotw(x_lo, wgl_buf[slot, 1, :half, :]) + _dotw(x_hi, wgl_buf[slot, 1, half:, :]) + bl
                gate = jnp.minimum(gate, SWIGLU_LIMIT)
                lin = jnp.clip(lin, -SWIGLU_LIMIT, SWIGLU_LIMIT)
                hmid[j, rows, :] = (gate * jax.nn.sigmoid(SWIGLU_ALPHA * gate) * (lin + 1.0)).astype(BF16)
            for_chunks(nb, chunk)
            return carry
        lax.fori_loop(0, N_FF_TILES, gateup_tile, 0)

        def down_tile(j, carry):
            slot = lax.rem(j, 2)
            down_copy(expert, j, slot).wait()

            @pl.when(j + 1 < N_DOWN_TILES)
            def _():
                down_copy(expert, j + 1, 1 - slot).start()

            @pl.when(jnp.logical_and(j + 1 == N_DOWN_TILES, next_active))
            def _():
                for c in gateup_copies(sbe_ref[nxt], 0, 0):
                    c.start()

            bd = bd_ref[0, pl.ds(j, 1), :]
            col0 = j * DOWN_TILE

            def chunk(b0, k):
                gather_some(nxt, k * (MOE_BLOCK // N_DOWN_TILES))
                rows = pl.ds(pl.multiple_of(b0 * MOE_BLOCK, MOE_BLOCK), k * MOE_BLOCK)
                acc = bd + _dotw(hmid[0, rows, :], wd_buf[slot, 0:FF_TILE, :])
                for jj in range(1, N_FF_TILES):
                    acc = acc + _dotw(hmid[jj, rows, :], wd_buf[slot, jj * FF_TILE:(jj + 1) * FF_TILE, :])
                y_emit(acc, k, row_base + b0 * MOE_BLOCK, col0)
            for_chunks(nb, chunk)
            return carry
        lax.fori_loop(0, N_DOWN_TILES, down_tile, 0)
        gather_upto(nxt, jnp.where(s + 1 < ns, nbk_ref[nxt] * MOE_BLOCK, 0))

    @pl.when(s == ns - 1)
    def _():
        gather_wait()
        y_drain(0)
        y_drain(1)
        ystage[0, pl.ds(0, MOE_BLOCK), :] = jnp.zeros((MOE_BLOCK, DOWN_TILE), F32)

        def zero_copy(b, j):
            return y_copy(0, 0, b * MOE_BLOCK, j * DOWN_TILE)

        def start_body(b, carry):
            for j in range(N_DOWN_TILES):
                zero_copy(b, j).start()
            return carry

        def wait_body(b, carry):
            for j in range(N_DOWN_TILES):
                zero_copy(0, 0).wait()
            return carry
        lax.fori_loop(used_ref[0], n_blocks_total, start_body, 0)
        lax.fori_loop(used_ref[0], n_blocks_total, wait_body, 0)


def _experts(sb_e, sb_rb0, sb_nb, n_used, row_tok, n2p, w_gu, b_gu, w_down, b_down):
    R = row_tok.shape[0]
    S = sb_e.shape[0]
    sb_rows = SB_BLOCKS * MOE_BLOCK

    grid_spec = pltpu.PrefetchScalarGridSpec(
        num_scalar_prefetch=5,
        grid=(S,),
        in_specs=[
            pl.BlockSpec(memory_space=pl.ANY),
            pl.BlockSpec(memory_space=pl.ANY),
            pl.BlockSpec(memory_space=pl.ANY),
            pl.BlockSpec((1, 2 * N_FF_TILES, FF_TILE), lambda s, e, r0, nbk, u, tok: (e[s], 0, 0)),
            pl.BlockSpec((1, N_DOWN_TILES, DOWN_TILE), lambda s, e, r0, nbk, u, tok: (e[s], 0, 0)),
        ],
        out_specs=pl.BlockSpec(memory_space=pl.ANY),
        scratch_shapes=[
            pltpu.VMEM((sb_rows, D_MODEL // 2), jnp.uint32),
            pltpu.VMEM((N_FF_TILES, sb_rows, FF_TILE), BF16),
            pltpu.VMEM((2, 2, D_MODEL, FF_TILE), F32),
            pltpu.VMEM((2, D_FF, DOWN_TILE), F32),
            pltpu.VMEM((2, CHUNK_BLOCKS * MOE_BLOCK, DOWN_TILE), F32),
            pltpu.SemaphoreType.DMA((1,)),
            pltpu.SemaphoreType.DMA((2,)),
            pltpu.SemaphoreType.DMA((2,)),
            pltpu.SemaphoreType.DMA((2,)),
            pltpu.SMEM((1,), jnp.int32),
            pltpu.SMEM((2,), jnp.int32),
            pltpu.SMEM((1,), jnp.int32),
        ],
    )
    return pl.pallas_call(
        _experts_kernel,
        grid_spec=grid_spec,
        out_shape=jax.ShapeDtypeStruct((R, D_MODEL), F32),
        compiler_params=_params(("arbitrary",)),
        name="moe_experts",
    )(sb_e, sb_rb0, sb_nb, n_used, row_tok, n2p, w_gu, w_down,
      b_gu.reshape(N_EXPERTS, 2 * N_FF_TILES, FF_TILE), b_down.reshape(N_EXPERTS, N_DOWN_TILES, DOWN_TILE))


def _combine_kernel(tb, dest_ref, y_hbm, h_ref, p_ref, g_ref, o_ref, buf, sem):
    i = pl.program_id(0)
    nb = pl.num_programs(0)
    slot = lax.rem(i, 2)
    rows = tb * TOP_K

    @pl.when(i == 0)
    def _():
        _row_gather_start(y_hbm, dest_ref, 0, rows, buf, 0, sem)

    @pl.when(i + 1 < nb)
    def _():
        _row_gather_start(y_hbm, dest_ref, (i + 1) * rows, rows, buf, 1 - slot, sem)

    _row_gather_wait(y_hbm, rows, buf, slot, sem)
    acc = h_ref[...]
    p = p_ref[...]
    for kk in range(TOP_K):
        acc = acc + p[:, kk:kk + 1] * buf[slot, pl.ds(kk * tb, tb), :]
    o_ref[...] = _rms(acc, g_ref[...])


def _combine(dest_km, y, h, probs, g, tb):
    T = h.shape[0]
    grid_spec = pltpu.PrefetchScalarGridSpec(
        num_scalar_prefetch=1,
        grid=(T // tb,),
        in_specs=[
            pl.BlockSpec(memory_space=pl.ANY),
            pl.BlockSpec((tb, D_MODEL), lambda i, d: (i, 0)),
            pl.BlockSpec((tb, N_EXPERTS), lambda i, d: (i, 0)),
            pl.BlockSpec((1, D_MODEL), lambda i, d: (0, 0)),
        ],
        out_specs=pl.BlockSpec((tb, D_MODEL), lambda i, d: (i, 0)),
        scratch_shapes=[
            pltpu.VMEM((2, tb * TOP_K, D_MODEL), F32),
            pltpu.SemaphoreType.DMA((2,)),
        ],
    )
    return pl.pallas_call(
        functools.partial(_combine_kernel, tb),
        grid_spec=grid_spec,
        out_shape=jax.ShapeDtypeStruct((T, D_MODEL), F32),
        compiler_params=_params(("arbitrary",)),
        name="moe_combine_norm",
    )(dest_km, y, h, probs, g)


def _routing(top_e, tb):
    T = top_e.shape[0]
    A = T * TOP_K
    n_blocks = -(-(A + N_EXPERTS * (MOE_BLOCK - 1)) // MOE_BLOCK)
    R = n_blocks * MOE_BLOCK
    n_sb = N_EXPERTS + n_blocks // SB_BLOCKS
    i32 = jnp.int32
    e_flat = top_e.reshape(A)
    onehot = (e_flat[:, None] == jnp.arange(N_EXPERTS, dtype=i32)[None, :]).astype(i32)
    csum = jnp.cumsum(onehot, axis=0)
    rank = jnp.sum(csum * onehot, axis=1) - 1
    counts = csum[-1]
    blocks = (counts + MOE_BLOCK - 1) // MOE_BLOCK
    blk_end = jnp.cumsum(blocks)
    blk_start = blk_end - blocks
    dest = (blk_start * MOE_BLOCK)[e_flat] + rank
    row_tok = jnp.zeros((R,), i32).at[dest].set(jnp.arange(A, dtype=i32) // TOP_K)
    dest_km = dest.reshape(T // tb, tb, TOP_K).transpose(0, 2, 1).reshape(A).astype(i32)

    sbs = (blocks + SB_BLOCKS - 1) // SB_BLOCKS
    sb_end = jnp.cumsum(sbs)
    sb_start = sb_end - sbs
    s_idx = jnp.arange(n_sb, dtype=i32)
    active = s_idx < sb_end[-1]
    e_s = jnp.minimum(jnp.sum((sb_end[None, :] <= s_idx[:, None]).astype(i32), axis=1), N_EXPERTS - 1)
    k_s = s_idx - sb_start[e_s]
    sb_rb0 = jnp.where(active, blk_start[e_s] + k_s * SB_BLOCKS, 0)
    sb_nb = jnp.where(active, jnp.minimum(blocks[e_s] - k_s * SB_BLOCKS, SB_BLOCKS), 0)
    sb_e = jnp.where(active, e_s, jnp.max(jnp.where(active, e_s, 0)))
    n_used = blk_end[-1:].astype(i32)
    return row_tok, dest_km, sb_e.astype(i32), sb_rb0.astype(i32), sb_nb.astype(i32), n_used


def kernel(x, meta_tokens, norm1_g, w_in, b_igate, b_fgate, w_pool_mix, pool_scale, w_out, norm2_g, w_router,
           b_router, w_gu, b_gu, w_down, b_down, norm_f_g):
    B, S, D = x.shape
    L = N_META + S
    T = B * L
    H = MLSTM_HEADS
    BM_IN = 688
    BM_OUT = 688
    BM_NORM = 192
    TB = 64
    assert T % BM_IN == 0 and T % BM_OUT == 0 and T % BM_NORM == 0 and T % TB == 0
    assert w_in.shape[0] == 1

    meta = jnp.broadcast_to(meta_tokens[None].astype(x.dtype), (B, N_META, D))
    h0 = jnp.concatenate([meta, x], axis=1).reshape(T, D)

    l = 0
    w_in_t = w_in.reshape(D, PROJ_COLS + 2 * H).T
    bg = jnp.pad(jnp.concatenate([b_igate[l], b_fgate[l]]), (0, LANES - 2 * H)).reshape(1, LANES)
    n1, gates = _norm1(h0, norm1_g[l].reshape(1, D), w_in_t, bg, BM_NORM)

    proj3 = _inproj(n1, w_in_t, PROJ_COLS, BM_IN, 1024).reshape(B, L, PROJ_COLS)
    pool_out = _pool(proj3, w_pool_mix[l].astype(BF16), pool_scale[l].reshape(1, POOL_WIDTH))
    mlstm_out = _mlstm(proj3, gates.reshape(B, L, LANES))

    h1 = _outproj(pool_out.reshape(T, POOL_WIDTH), mlstm_out.reshape(T, MLSTM_WIDTH),
                  w_out.reshape(D, D), h0, BM_OUT, 512)

    n2p, top_e, probs = _router(h1, norm2_g[l].reshape(1, D), w_router.reshape(D, N_EXPERTS).T,
                                b_router.reshape(1, N_EXPERTS), BM_NORM)

    row_tok, dest_km, sb_e, sb_rb0, sb_nb, n_used = _routing(top_e[:, :TOP_K], TB)
    y = _experts(sb_e, sb_rb0, sb_nb, n_used, row_tok, n2p,
                 w_gu.reshape(N_EXPERTS, D, 2 * D_FF), b_gu.reshape(N_EXPERTS, 1, 2 * D_FF),
                 w_down.reshape(N_EXPERTS, D_FF, D), b_down.reshape(N_EXPERTS, 1, D))
    out = _combine(dest_km, y, h1, probs, norm_f_g.reshape(1, D), TB)
    return out.reshape(B, L, D)[:, N_META:]
```

```python
import functools

import jax
import jax.numpy as jnp
from jax import lax
from jax.experimental import pallas as pl
from jax.experimental.pallas import tpu as pltpu

D_MODEL = 4096
N_META = 16
POOL_WINDOWS = (2, 4, 8, 16)
POOL_WIDTH = D_MODEL // 4
POOL_GROUP = POOL_WIDTH // len(POOL_WINDOWS)
MLSTM_WIDTH = D_MODEL - POOL_WIDTH
MLSTM_HEADS = 6
MLSTM_V_DIM = MLSTM_WIDTH // MLSTM_HEADS
MLSTM_QK_DIM = MLSTM_V_DIM // 2
GATE_SOFTCAP = 15.0
N_EXPERTS = 32
TOP_K = 4
D_FF = D_MODEL // 2
SWIGLU_ALPHA = 1.702
SWIGLU_LIMIT = 7.0
MOE_BLOCK = 128
EPS = 1e-6

LANES = 128
SEQ_CHUNK = 256
PROJ_COLS = POOL_WIDTH + 2 * MLSTM_HEADS * MLSTM_QK_DIM + 2 * MLSTM_WIDTH
VMEM_LIMIT = 56 * 1024 * 1024
SB_BLOCKS = 10
CHUNK_BLOCKS = 6
HALF_CHUNK = 3
FF_TILE = 256
DOWN_TILE = 1024
N_FF_TILES = D_FF // FF_TILE
N_DOWN_TILES = D_MODEL // DOWN_TILE
Y_WORDS = DOWN_TILE // 2

F32 = jnp.float32
BF16 = jnp.bfloat16


def _params(sem, vmem=VMEM_LIMIT):
    return pltpu.CompilerParams(dimension_semantics=sem, vmem_limit_bytes=vmem)


def _split3(a):
    hi = a.astype(BF16)
    r1 = a - hi.astype(F32)
    mid = r1.astype(BF16)
    lo = (r1 - mid.astype(F32)).astype(BF16)
    return hi, mid, lo


def _dot(a, b):
    return jnp.dot(a, b, preferred_element_type=F32)


def _dotw(a, w):
    return lax.dot_general(a, w, (((1,), (0,)), ((), ())), preferred_element_type=F32)


def _dot_nt(a, wt):
    return lax.dot_general(a, wt, (((1,), (1,)), ((), ())), preferred_element_type=F32)


def _dot3_nt(a, wt):
    a_hi = a.astype(BF16)
    a_lo = (a - a_hi.astype(F32)).astype(BF16)
    w_hi = wt.astype(BF16)
    w_lo = (wt - w_hi.astype(F32)).astype(BF16)
    n = wt.shape[0]
    if n == LANES:
        both = _dot_nt(a_hi, jnp.concatenate([w_hi, w_lo], axis=0))
        return both[:, :n] + (both[:, n:] + _dot_nt(a_lo, w_hi))
    return _dot_nt(a_hi, w_hi) + (_dot_nt(a_hi, w_lo) + _dot_nt(a_lo, w_hi))


def _rms(x, g):
    return x * lax.rsqrt(jnp.mean(x * x, axis=-1, keepdims=True) + EPS) * g


GATE_ROWS = 16


def _norm1_kernel(h_ref, g_ref, wg_ref, bg_ref, n_ref, gate_ref):
    y = _rms(h_ref[...], g_ref[...])
    n_ref[...] = y.astype(BF16)
    sub = lax.broadcasted_iota(jnp.int32, (GATE_ROWS, D_MODEL), 0)
    wg = jnp.where(sub < 2 * MLSTM_HEADS, wg_ref[...], 0.0)
    wg = jnp.concatenate([wg, jnp.zeros((LANES - GATE_ROWS, D_MODEL), F32)], axis=0)
    gate_ref[...] = _dot3_nt(y, wg) + bg_ref[...]


def _norm1(h, g, w_in_t, bg, bm):
    T = h.shape[0]
    return pl.pallas_call(
        _norm1_kernel,
        grid=(T // bm,),
        in_specs=[
            pl.BlockSpec((bm, D_MODEL), lambda i: (i, 0)),
            pl.BlockSpec((1, D_MODEL), lambda i: (0, 0)),
            pl.BlockSpec((GATE_ROWS, D_MODEL), lambda i: (PROJ_COLS // GATE_ROWS, 0)),
            pl.BlockSpec((1, LANES), lambda i: (0, 0)),
        ],
        out_specs=[
            pl.BlockSpec((bm, D_MODEL), lambda i: (i, 0)),
            pl.BlockSpec((bm, LANES), lambda i: (i, 0)),
        ],
        out_shape=[
            jax.ShapeDtypeStruct((T, D_MODEL), BF16),
            jax.ShapeDtypeStruct((T, LANES), F32),
        ],
        compiler_params=_params(("parallel",)),
        name="norm1_gates",
    )(h, g, w_in_t, bg)


def _inproj_kernel(x_ref, wt_ref, o_ref):
    o_ref[...] = _dot_nt(x_ref[...], wt_ref[...]).astype(o_ref.dtype)


def _inproj(x, wt, n_cols, bm, bn):
    M, K = x.shape
    return pl.pallas_call(
        _inproj_kernel,
        grid=(n_cols // bn, M // bm),
        in_specs=[
            pl.BlockSpec((bm, K), lambda j, i: (i, 0)),
            pl.BlockSpec((bn, K), lambda j, i: (j, 0)),
        ],
        out_specs=pl.BlockSpec((bm, bn), lambda j, i: (i, j)),
        out_shape=jax.ShapeDtypeStruct((M, n_cols), F32),
        compiler_params=_params(("parallel", "parallel")),
        name="in_proj",
    )(x, wt)


def _outproj_kernel(p_ref, m_ref, w_ref, h_ref, o_ref):
    o_ref[...] = h_ref[...] + (_dotw(p_ref[...], w_ref[:POOL_WIDTH, :]) + _dotw(m_ref[...], w_ref[POOL_WIDTH:, :]))


def _outproj(p, m, w, h, bm, bn):
    M = p.shape[0]
    K, N = w.shape
    return pl.pallas_call(
        _outproj_kernel,
        grid=(N // bn, M // bm),
        in_specs=[
            pl.BlockSpec((bm, POOL_WIDTH), lambda j, i: (i, 0)),
            pl.BlockSpec((bm, MLSTM_WIDTH), lambda j, i: (i, 0)),
            pl.BlockSpec((K, bn), lambda j, i: (0, j)),
            pl.BlockSpec((bm, bn), lambda j, i: (i, j)),
        ],
        out_specs=pl.BlockSpec((bm, bn), lambda j, i: (i, j)),
        out_shape=jax.ShapeDtypeStruct((M, N), F32),
        compiler_params=_params(("parallel", "parallel")),
        name="out_proj",
    )(p, m, w, h)


def _pool_kernel(u_ref, w_ref, s_ref, o_ref, carry_ref):
    c = pl.program_id(1)

    @pl.when(c == 0)
    def _():
        carry_ref[...] = jnp.zeros_like(carry_ref)

    u = u_ref[0]
    ext = jnp.concatenate([carry_ref[...], u], axis=0)
    carry_ref[...] = u[SEQ_CHUNK - 16:, :]
    pos = c * SEQ_CHUNK + lax.broadcasted_iota(jnp.int32, (SEQ_CHUNK, 1), 0)
    for g, win in enumerate(POOL_WINDOWS):
        cols = slice(g * POOL_GROUP, (g + 1) * POOL_GROUP)
        s = ext[:, cols]
        span = 1
        while span < win:
            s = s + pltpu.roll(s, span, axis=0)
            span *= 2
        cnt = jnp.minimum(pos + 1, win).astype(F32)
        d = s[16:, :] / cnt - u[:, cols]
        y = _dot(d.astype(BF16), w_ref[g]) * s_ref[:, cols]
        o_ref[0, :, cols] = y.astype(o_ref.dtype)


def _pool(proj3, w_mix, scale):
    B, L, _ = proj3.shape
    nc = pl.cdiv(L, SEQ_CHUNK)
    return pl.pallas_call(
        _pool_kernel,
        grid=(B, nc),
        in_specs=[
            pl.BlockSpec((1, SEQ_CHUNK, POOL_WIDTH), lambda b, c: (b, c, 0)),
            pl.BlockSpec((len(POOL_WINDOWS), POOL_GROUP, POOL_GROUP), lambda b, c: (0, 0, 0)),
            pl.BlockSpec((1, POOL_WIDTH), lambda b, c: (0, 0)),
        ],
        out_specs=pl.BlockSpec((1, SEQ_CHUNK, POOL_WIDTH), lambda b, c: (b, c, 0)),
        out_shape=jax.ShapeDtypeStruct((B, L, POOL_WIDTH), BF16),
        scratch_shapes=[pltpu.VMEM((16, POOL_WIDTH), F32)],
        compiler_params=_params(("parallel", "arbitrary")),
        name="pool_mixer",
    )(proj3, w_mix, scale)


def _soft_cap(a):
    return GATE_SOFTCAP * jnp.tanh(a / GATE_SOFTCAP)


def _log_sigmoid(a):
    return jnp.minimum(a, 0.0) - jnp.log1p(jnp.exp(-jnp.abs(a)))


HEADS_PER_STEP = 2


def _mlstm_kernel(seq_len, q_ref, k_ref, v_ref, o_ref, gate_ref, out_ref, c_ref, n_ref, m_ref):
    c = pl.program_id(2)
    Lc = SEQ_CHUNK
    dk, dv = MLSTM_QK_DIM, MLSTM_V_DIM

    @pl.when(c == 0)
    def _():
        c_ref[...] = jnp.zeros_like(c_ref)
        n_ref[...] = jnp.zeros_like(n_ref)
        m_ref[...] = jnp.zeros_like(m_ref)

    row = lax.broadcasted_iota(jnp.int32, (Lc, 1), 0)
    col = lax.broadcasted_iota(jnp.int32, (1, Lc), 1)
    ok_col = (c * Lc + row) < seq_len
    ok_row = (c * Lc + col) < seq_len
    causal = col <= row
    tri = causal.astype(BF16)
    tri_t = (row <= col).astype(BF16)
    gc = gate_ref[0]
    gt = gc.T
    for i in range(HEADS_PER_STEP):
        _mlstm_head(pl.program_id(1) * HEADS_PER_STEP + i, ok_col, ok_row, causal, tri, tri_t, gc, gt,
                    q_ref[0, :, i * dk:(i + 1) * dk], k_ref[0, :, i * dk:(i + 1) * dk],
                    v_ref[0, :, i * dv:(i + 1) * dv], o_ref[0, :, i * dv:(i + 1) * dv],
                    out_ref.at[0, :, i * dv:(i + 1) * dv], c_ref.at[i], n_ref.at[i], m_ref.at[i])


def _mlstm_head(h, ok_col, ok_row, causal, tri, tri_t, gc, gt, q_in, k_in, v_in, o_in, out_ref, c_ref, n_ref, m_ref):
    Lc = SEQ_CHUNK
    q = jnp.where(ok_col, q_in, 0.0) * (MLSTM_QK_DIM ** -0.5)
    k = jnp.where(ok_col, k_in, 0.0)
    v = jnp.where(ok_col, v_in, 0.0)

    lane = lax.broadcasted_iota(jnp.int32, (1, LANES), 1)
    sub = lax.broadcasted_iota(jnp.int32, (LANES, 1), 0)
    gi_c = jnp.sum(jnp.where(lane == h, gc, 0.0), axis=1, keepdims=True)
    gf_c = jnp.sum(jnp.where(lane == h + MLSTM_HEADS, gc, 0.0), axis=1, keepdims=True)
    gi_r = jnp.sum(jnp.where(sub == h, gt, 0.0), axis=0, keepdims=True)
    gf_r = jnp.sum(jnp.where(sub == h + MLSTM_HEADS, gt, 0.0), axis=0, keepdims=True)
    i_c = jnp.where(ok_col, _soft_cap(gi_c), 0.0)
    f_c = jnp.where(ok_col, _log_sigmoid(_soft_cap(gf_c)), 0.0)
    i_r = jnp.where(ok_row, _soft_cap(gi_r), 0.0)
    f_r = jnp.where(ok_row, _log_sigmoid(_soft_cap(gf_r)), 0.0)

    fb_c = jnp.broadcast_to(f_c, (Lc, LANES))
    b_c = sum(_dot(tri, p) for p in _split3(fb_c))[:, 0:1]
    fb_r = jnp.broadcast_to(f_r, (8, Lc))
    b_r8 = sum(_dot(p, tri_t) for p in _split3(fb_r))
    b_r = b_r8[0:1, :]
    g_tot = b_r8[0:1, Lc - 1:Lc]

    m_prev = m_ref[...]
    dlog = jnp.where(causal, b_c - b_r + i_r, -jnp.inf)
    inter_log = b_c + m_prev
    m_out = jnp.maximum(inter_log, jnp.max(dlog, axis=1, keepdims=True))
    wts = jnp.exp(dlog - m_out)
    inter_w = jnp.exp(inter_log - m_out)

    qb = q.astype(BF16)
    kb = k.astype(BF16)
    vb = v.astype(BF16)
    s = lax.dot_general(qb, kb, (((1,), (1,)), ((), ())), preferred_element_type=F32) * wts
    num = _dot(s.astype(BF16), vb) + inter_w * _dot(qb, c_ref[...].astype(BF16))
    den = jnp.sum(s, axis=1, keepdims=True) + inter_w * jnp.sum(q * n_ref[...], axis=1, keepdims=True)
    hh = num / jnp.maximum(jnp.abs(den), jnp.exp(-m_out))
    out_ref[...] = (jax.nn.sigmoid(o_in) * hh).astype(out_ref.dtype)

    a_c = g_tot - b_c + i_c
    m_new = jnp.maximum(g_tot + m_prev, jnp.max(a_c, axis=0, keepdims=True))
    wk = k * jnp.exp(a_c - m_new)
    decay = jnp.exp(g_tot + m_prev - m_new)
    c_ref[...] = decay * c_ref[...] + lax.dot_general(
        wk.astype(BF16), vb, (((0,), (0,)), ((), ())), preferred_element_type=F32)
    n_ref[...] = decay * n_ref[...] + jnp.sum(wk, axis=0, keepdims=True)
    m_ref[...] = m_new


def _mlstm(proj3, gates3):
    B, L, _ = proj3.shape
    nc = pl.cdiv(L, SEQ_CHUNK)
    G = HEADS_PER_STEP
    dk, dv, H = G * MLSTM_QK_DIM, G * MLSTM_V_DIM, MLSTM_HEADS // G
    q0 = POOL_WIDTH // dk
    k0 = q0 + H
    v0 = (POOL_WIDTH + 2 * MLSTM_HEADS * MLSTM_QK_DIM) // dv
    o0 = v0 + H
    return pl.pallas_call(
        functools.partial(_mlstm_kernel, L),
        grid=(B, H, nc),
        in_specs=[
            pl.BlockSpec((1, SEQ_CHUNK, dk), lambda b, h, c: (b, c, q0 + h)),
            pl.BlockSpec((1, SEQ_CHUNK, dk), lambda b, h, c: (b, c, k0 + h)),
            pl.BlockSpec((1, SEQ_CHUNK, dv), lambda b, h, c: (b, c, v0 + h)),
            pl.BlockSpec((1, SEQ_CHUNK, dv), lambda b, h, c: (b, c, o0 + h)),
            pl.BlockSpec((1, SEQ_CHUNK, LANES), lambda b, h, c: (b, c, 0)),
        ],
        out_specs=pl.BlockSpec((1, SEQ_CHUNK, dv), lambda b, h, c: (b, c, h)),
        out_shape=jax.ShapeDtypeStruct((B, L, MLSTM_WIDTH), BF16),
        scratch_shapes=[
            pltpu.VMEM((G, MLSTM_QK_DIM, MLSTM_V_DIM), F32),
            pltpu.VMEM((G, 1, MLSTM_QK_DIM), F32),
            pltpu.VMEM((G, 1, 1), F32),
        ],
        compiler_params=_params(("parallel", "parallel", "arbitrary")),
        name="mlstm",
    )(proj3, proj3, proj3, proj3, gates3)


def _pack_bf16_pair(lo, hi):
    lo_b = lax.bitcast_convert_type(lo.astype(BF16).astype(F32), jnp.uint32)
    hi_b = lax.bitcast_convert_type(hi.astype(BF16).astype(F32), jnp.uint32)
    return (lo_b >> 16) | hi_b


def _unpack_bf16_pair(w):
    lo = lax.bitcast_convert_type(w << 16, F32).astype(BF16)
    hi = lax.bitcast_convert_type(w & jnp.uint32(0xFFFF0000), F32).astype(BF16)
    return lo, hi


def _router_kernel(h_ref, g_ref, wr_ref, br_ref, n_ref, e_ref, p_ref):
    y = _rms(h_ref[...], g_ref[...])
    half = D_MODEL // 2
    n_ref[...] = _pack_bf16_pair(y[:, :half], y[:, half:])
    logits = _dot3_nt(y, wr_ref[...]) + br_ref[...]
    lane = lax.broadcasted_iota(jnp.int32, logits.shape, 1).astype(F32)
    l = jnp.where(lane < N_EXPERTS, logits, -jnp.inf)
    vals, idxs = [], []
    for _ in range(TOP_K):
        m = jnp.max(l, axis=1, keepdims=True)
        idx = jnp.min(jnp.where(l == m, lane, float(LANES)), axis=1, keepdims=True)
        vals.append(m)
        idxs.append(idx)
        l = jnp.where(lane == idx, -jnp.inf, l)
    ex = [jnp.exp(v - vals[0]) for v in vals]
    tot = ex[0] + ex[1] + ex[2] + ex[3]
    e_out = jnp.zeros(logits.shape, F32)
    p_out = jnp.zeros(logits.shape, F32)
    for kk in range(TOP_K):
        e_out = jnp.where(lane == kk, idxs[kk], e_out)
        p_out = jnp.where(lane == kk, ex[kk] / tot, p_out)
    e_ref[...] = e_out.astype(jnp.int32)
    p_ref[...] = p_out


def _router(h, g, wr, br, bm):
    T = h.shape[0]
    return pl.pallas_call(
        _router_kernel,
        grid=(T // bm,),
        in_specs=[
            pl.BlockSpec((bm, D_MODEL), lambda i: (i, 0)),
            pl.BlockSpec((1, D_MODEL), lambda i: (0, 0)),
            pl.BlockSpec((N_EXPERTS, D_MODEL), lambda i: (0, 0)),
            pl.BlockSpec((1, N_EXPERTS), lambda i: (0, 0)),
        ],
        out_specs=[
            pl.BlockSpec((bm, D_MODEL // 2), lambda i: (i, 0)),
            pl.BlockSpec((bm, N_EXPERTS), lambda i: (i, 0)),
            pl.BlockSpec((bm, N_EXPERTS), lambda i: (i, 0)),
        ],
        out_shape=[
            jax.ShapeDtypeStruct((T, D_MODEL // 2), jnp.uint32),
            jax.ShapeDtypeStruct((T, N_EXPERTS), jnp.int32),
            jax.ShapeDtypeStruct((T, N_EXPERTS), F32),
        ],
        compiler_params=_params(("parallel",)),
        name="norm2_router",
    )(h, g, wr, br)


def _row_gather_start(src_hbm, idx_ref, base, n_rows, buf, slot, sem):
    def body(r, carry):
        tok = idx_ref[base + r]
        pltpu.make_async_copy(src_hbm.at[pl.ds(tok, 1), :], buf.at[slot, pl.ds(r, 1), :], sem.at[slot]).start()
        return carry
    lax.fori_loop(0, n_rows, body, 0, unroll=32)


def _row_gather_wait(src_hbm, n_rows, buf, slot, sem):
    pltpu.make_async_copy(src_hbm.at[pl.ds(0, n_rows), :], buf.at[slot], sem.at[slot]).wait()


def _experts_kernel(sbe_ref, rb0_ref, nbk_ref, used_ref, tok_ref,
                    x_hbm, wgu_hbm, wd_hbm, bgu_ref, bd_ref,
                    y_hbm,
                    x_res, hmid, wgl_buf, wd_buf, ystage, gsem, wsem, dsem, ysem, ypend, gcnt):
    s = pl.program_id(0)
    ns = pl.num_programs(0)
    nb = nbk_ref[s]
    expert = sbe_ref[s]
    nxt = jnp.minimum(s + 1, ns - 1)
    next_active = jnp.logical_and(s + 1 < ns, nbk_ref[nxt] > 0)
    half = D_MODEL // 2
    n_blocks_total = y_hbm.shape[0] // MOE_BLOCK

    def gateup_copies(e, j, slot):
        gate_cols = pl.ds(pl.multiple_of(j * FF_TILE, FF_TILE), FF_TILE)
        lin_cols = pl.ds(pl.multiple_of(D_FF + j * FF_TILE, FF_TILE), FF_TILE)
        return (pltpu.make_async_copy(wgu_hbm.at[e, :, gate_cols], wgl_buf.at[slot, 0], wsem.at[slot]),
                pltpu.make_async_copy(wgu_hbm.at[e, :, lin_cols], wgl_buf.at[slot, 1], wsem.at[slot]))

    def down_copy(e, j, slot):
        cols = pl.ds(pl.multiple_of(j * DOWN_TILE, DOWN_TILE), DOWN_TILE)
        return pltpu.make_async_copy(wd_hbm.at[e, :, cols], wd_buf.at[slot], dsem.at[slot])

    def x_block_copy(tok, r, n):
        return pltpu.make_async_copy(x_hbm.at[pl.ds(tok, n), :], x_res.at[pl.ds(r, n), :], gsem.at[0])

    def gather_row(sb_base, r):
        tok = tok_ref[jnp.minimum(sb_base + r, tok_ref.shape[0] - 1)]
        x_block_copy(tok, r, 1).start()

    def gather_upto(sb, n_rows):
        base = rb0_ref[sb] * MOE_BLOCK

        def body(r, carry):
            gather_row(base, r)
            return carry
        lax.fori_loop(gcnt[0], n_rows, body, 0)
        gcnt[0] = jnp.maximum(gcnt[0], n_rows)

    def gather_some(sb, n):
        base = rb0_ref[sb] * MOE_BLOCK
        cur = gcnt[0]
        for i in range(n):
            gather_row(base, cur + i)
        gcnt[0] = cur + n

    def gather_wait():
        n = gcnt[0]
        n_full = n // MOE_BLOCK

        def block_body(r, carry):
            x_block_copy(0, 0, MOE_BLOCK).wait()
            return carry

        def row_body(r, carry):
            x_block_copy(0, 0, 1).wait()
            return carry
        lax.fori_loop(0, n_full, block_body, 0)
        lax.fori_loop(n_full * MOE_BLOCK, n, row_body, 0)
        gcnt[0] = 0

    def y_copy(piece, row0, word0):
        return pltpu.make_async_copy(
            ystage.at[pl.ds(piece * MOE_BLOCK, MOE_BLOCK), :],
            y_hbm.at[pl.ds(pl.multiple_of(row0, MOE_BLOCK), MOE_BLOCK), pl.ds(pl.multiple_of(word0, Y_WORDS), Y_WORDS)],
            ysem.at[0])

    def y_drain():
        for piece in range(CHUNK_BLOCKS):
            @pl.when(ypend[0] > piece)
            def _():
                y_copy(0, 0, 0).wait()
        ypend[0] = 0

    def y_emit(val, n_pieces, row0, word0):
        ystage[pl.ds(0, n_pieces * MOE_BLOCK), :] = _pack_bf16_pair(val[:, :Y_WORDS], val[:, Y_WORDS:])
        for piece in range(n_pieces):
            y_copy(piece, row0 + piece * MOE_BLOCK, word0).start()
        ypend[0] = n_pieces

    def for_chunks(n_blocks, fn):
        n_full = n_blocks // CHUNK_BLOCKS

        def body(c, carry):
            fn(c * CHUNK_BLOCKS, CHUNK_BLOCKS)
            return carry
        lax.fori_loop(0, n_full, body, 0)
        done = n_full * CHUNK_BLOCKS
        rem = n_blocks - done

        @pl.when(rem >= HALF_CHUNK)
        def _():
            fn(done, HALF_CHUNK)
        tail = jnp.where(rem >= HALF_CHUNK, rem - HALF_CHUNK, rem)
        tail0 = jnp.where(rem >= HALF_CHUNK, done + HALF_CHUNK, done)
        for k in range(1, HALF_CHUNK):
            @pl.when(tail == k)
            def _():
                fn(tail0, k)

    @pl.when(s == 0)
    def _():
        ypend[0] = 0
        gcnt[0] = 0

        @pl.when(nb > 0)
        def _():
            for c in gateup_copies(expert, 0, 0):
                c.start()
        gather_upto(0, nb * MOE_BLOCK)

    @pl.when(nb > 0)
    def _():
        gather_wait()
        row_base = rb0_ref[s] * MOE_BLOCK

        def gateup_tile(j, carry):
            slot = lax.rem(j, 2)
            for c in gateup_copies(expert, j, slot):
                c.wait()

            @pl.when(j + 1 < N_FF_TILES)
            def _():
                for c in gateup_copies(expert, j + 1, 1 - slot):
                    c.start()

            @pl.when(j + 1 == N_FF_TILES)
            def _():
                down_copy(expert, 0, 0).start()

            bg = bgu_ref[0, pl.ds(j, 1), :]
            bl = bgu_ref[0, pl.ds(N_FF_TILES + j, 1), :]

            def chunk(b0, k):
                rows = pl.ds(pl.multiple_of(b0 * MOE_BLOCK, MOE_BLOCK), k * MOE_BLOCK)
                x_lo, x_hi = _unpack_bf16_pair(x_res[rows, :])
                gate = _dotw(x_lo, wgl_buf[slot, 0, :half, :]) + _dotw(x_hi, wgl_buf[slot, 0, half:, :]) + bg
                lin = _dotw(x_lo, wgl_buf[slot, 1, :half, :]) + _dotw(x_hi, wgl_buf[slot, 1, half:, :]) + bl
                gate = jnp.minimum(gate, SWIGLU_LIMIT)
                lin = jnp.clip(lin, -SWIGLU_LIMIT, SWIGLU_LIMIT)
                hmid[j, rows, :] = (gate * jax.nn.sigmoid(SWIGLU_ALPHA * gate) * (lin + 1.0)).astype(BF16)
            for_chunks(nb, chunk)
            return carry
        lax.fori_loop(0, N_FF_TILES, gateup_tile, 0)

        def down_tile(j, carry):
            slot = lax.rem(j, 2)
            down_copy(expert, j, slot).wait()

            @pl.when(j + 1 < N_DOWN_TILES)
            def _():
                down_copy(expert, j + 1, 1 - slot).start()

            @pl.when(jnp.logical_and(j + 1 == N_DOWN_TILES, next_active))
            def _():
                for c in gateup_copies(sbe_ref[nxt], 0, 0):
                    c.start()

            bd = bd_ref[0, pl.ds(j, 1), :]
            word0 = j * Y_WORDS

            def chunk(b0, k):
                y_drain()
                gather_some(nxt, k * (MOE_BLOCK // N_DOWN_TILES))
                rows = pl.ds(pl.multiple_of(b0 * MOE_BLOCK, MOE_BLOCK), k * MOE_BLOCK)
                acc = bd + _dotw(hmid[0, rows, :], wd_buf[slot, 0:FF_TILE, :])
                for jj in range(1, N_FF_TILES):
                    acc = acc + _dotw(hmid[jj, rows, :], wd_buf[slot, jj * FF_TILE:(jj + 1) * FF_TILE, :])
                y_emit(acc, k, row_base + b0 * MOE_BLOCK, word0)
            for_chunks(nb, chunk)
            return carry
        lax.fori_loop(0, N_DOWN_TILES, down_tile, 0)
        gather_upto(nxt, jnp.where(s + 1 < ns, nbk_ref[nxt] * MOE_BLOCK, 0))

    @pl.when(s == ns - 1)
    def _():
        gather_wait()
        y_drain()
        ystage[pl.ds(0, MOE_BLOCK), :] = jnp.zeros((MOE_BLOCK, Y_WORDS), jnp.uint32)

        def zero_copy(b, j):
            return y_copy(0, b * MOE_BLOCK, j * Y_WORDS)

        def start_body(b, carry):
            for j in range(N_DOWN_TILES):
                zero_copy(b, j).start()
            return carry

        def wait_body(b, carry):
            for j in range(N_DOWN_TILES):
                zero_copy(0, 0).wait()
            return carry
        lax.fori_loop(used_ref[0], n_blocks_total, start_body, 0)
        lax.fori_loop(used_ref[0], n_blocks_total, wait_body, 0)


def _experts(sb_e, sb_rb0, sb_nb, n_used, row_tok, n2p, w_gu, b_gu, w_down, b_down):
    R = row_tok.shape[0]
    S = sb_e.shape[0]
    sb_rows = SB_BLOCKS * MOE_BLOCK

    grid_spec = pltpu.PrefetchScalarGridSpec(
        num_scalar_prefetch=5,
        grid=(S,),
        in_specs=[
            pl.BlockSpec(memory_space=pl.ANY),
            pl.BlockSpec(memory_space=pl.ANY),
            pl.BlockSpec(memory_space=pl.ANY),
            pl.BlockSpec((1, 2 * N_FF_TILES, FF_TILE), lambda s, e, r0, nbk, u, tok: (e[s], 0, 0)),
            pl.BlockSpec((1, N_DOWN_TILES, DOWN_TILE), lambda s, e, r0, nbk, u, tok: (e[s], 0, 0)),
        ],
        out_specs=pl.BlockSpec(memory_space=pl.ANY),
        scratch_shapes=[
            pltpu.VMEM((sb_rows, D_MODEL // 2), jnp.uint32),
            pltpu.VMEM((N_FF_TILES, sb_rows, FF_TILE), BF16),
            pltpu.VMEM((2, 2, D_MODEL, FF_TILE), F32),
            pltpu.VMEM((2, D_FF, DOWN_TILE), F32),
            pltpu.VMEM((CHUNK_BLOCKS * MOE_BLOCK, Y_WORDS), jnp.uint32),
            pltpu.SemaphoreType.DMA((1,)),
            pltpu.SemaphoreType.DMA((2,)),
            pltpu.SemaphoreType.DMA((2,)),
            pltpu.SemaphoreType.DMA((1,)),
            pltpu.SMEM((1,), jnp.int32),
            pltpu.SMEM((1,), jnp.int32),
        ],
    )
    return pl.pallas_call(
        _experts_kernel,
        grid_spec=grid_spec,
        out_shape=jax.ShapeDtypeStruct((R, D_MODEL // 2), jnp.uint32),
        compiler_params=_params(("arbitrary",)),
        name="moe_experts",
    )(sb_e, sb_rb0, sb_nb, n_used, row_tok, n2p, w_gu, w_down,
      b_gu.reshape(N_EXPERTS, 2 * N_FF_TILES, FF_TILE), b_down.reshape(N_EXPERTS, N_DOWN_TILES, DOWN_TILE))


def _combine_kernel(tb, dest_ref, y_hbm, h_ref, p_ref, g_ref, o_ref, buf, sem):
    i = pl.program_id(0)
    nb = pl.num_programs(0)
    rows = tb * TOP_K

    @pl.when(i == 0)
    def _():
        _row_gather_start(y_hbm, dest_ref, 0, rows, buf, 0, sem)

    def step(slot):
        @pl.when(i + 1 < nb)
        def _():
            _row_gather_start(y_hbm, dest_ref, (i + 1) * rows, rows, buf, 1 - slot, sem)

        _row_gather_wait(y_hbm, rows, buf, slot, sem)
        acc = h_ref[...]
        p = p_ref[...]
        for kk in range(TOP_K):
            w = buf[slot, pl.ds(kk * tb, tb), :]
            lo = lax.bitcast_convert_type(w << 16, F32)
            hi = lax.bitcast_convert_type(w & jnp.uint32(0xFFFF0000), F32)
            pieces = []
            for j in range(N_DOWN_TILES):
                pieces += [lo[:, j * Y_WORDS:(j + 1) * Y_WORDS], hi[:, j * Y_WORDS:(j + 1) * Y_WORDS]]
            acc = acc + p[:, kk:kk + 1] * jnp.concatenate(pieces, axis=1)
        o_ref[...] = _rms(acc, g_ref[...])

    for slot in range(2):
        pl.when(lax.rem(i, 2) == slot)(functools.partial(step, slot))


def _combine(dest_km, y, h, probs, g, tb):
    T = h.shape[0]
    grid_spec = pltpu.PrefetchScalarGridSpec(
        num_scalar_prefetch=1,
        grid=(T // tb,),
        in_specs=[
            pl.BlockSpec(memory_space=pl.ANY),
            pl.BlockSpec((tb, D_MODEL), lambda i, d: (i, 0)),
            pl.BlockSpec((tb, N_EXPERTS), lambda i, d: (i, 0)),
            pl.BlockSpec((1, D_MODEL), lambda i, d: (0, 0)),
        ],
        out_specs=pl.BlockSpec((tb, D_MODEL), lambda i, d: (i, 0)),
        scratch_shapes=[
            pltpu.VMEM((2, tb * TOP_K, D_MODEL // 2), jnp.uint32),
            pltpu.SemaphoreType.DMA((2,)),
        ],
    )
    return pl.pallas_call(
        functools.partial(_combine_kernel, tb),
        grid_spec=grid_spec,
        out_shape=jax.ShapeDtypeStruct((T, D_MODEL), F32),
        compiler_params=_params(("arbitrary",)),
        name="moe_combine_norm",
    )(dest_km, y, h, probs, g)


def _routing(top_e, tb):
    T = top_e.shape[0]
    A = T * TOP_K
    n_blocks = -(-(A + N_EXPERTS * (MOE_BLOCK - 1)) // MOE_BLOCK)
    R = n_blocks * MOE_BLOCK
    n_sb = N_EXPERTS + n_blocks // SB_BLOCKS
    i32 = jnp.int32
    e_flat = top_e.reshape(A)
    onehot = (e_flat[:, None] == jnp.arange(N_EXPERTS, dtype=i32)[None, :]).astype(i32)
    csum = jnp.cumsum(onehot, axis=0)
    rank = jnp.sum(csum * onehot, axis=1) - 1
    counts = csum[-1]
    blocks = (counts + MOE_BLOCK - 1) // MOE_BLOCK
    blk_end = jnp.cumsum(blocks)
    blk_start = blk_end - blocks
    dest = (blk_start * MOE_BLOCK)[e_flat] + rank
    row_tok = jnp.zeros((R,), i32).at[dest].set(jnp.arange(A, dtype=i32) // TOP_K)
    dest_km = dest.reshape(T // tb, tb, TOP_K).transpose(0, 2, 1).reshape(A).astype(i32)

    sbs = (blocks + SB_BLOCKS - 1) // SB_BLOCKS
    sb_end = jnp.cumsum(sbs)
    sb_start = sb_end - sbs
    s_idx = jnp.arange(n_sb, dtype=i32)
    active = s_idx < sb_end[-1]
    e_s = jnp.minimum(jnp.sum((sb_end[None, :] <= s_idx[:, None]).astype(i32), axis=1), N_EXPERTS - 1)
    k_s = s_idx - sb_start[e_s]
    sb_rb0 = jnp.where(active, blk_start[e_s] + k_s * SB_BLOCKS, 0)
    sb_nb = jnp.where(active, jnp.minimum(blocks[e_s] - k_s * SB_BLOCKS, SB_BLOCKS), 0)
    sb_e = jnp.where(active, e_s, jnp.max(jnp.where(active, e_s, 0)))
    n_used = blk_end[-1:].astype(i32)
    return row_tok, dest_km, sb_e.astype(i32), sb_rb0.astype(i32), sb_nb.astype(i32), n_used


def kernel(x, meta_tokens, norm1_g, w_in, b_igate, b_fgate, w_pool_mix, pool_scale, w_out, norm2_g, w_router,
           b_router, w_gu, b_gu, w_down, b_down, norm_f_g):
    B, S, D = x.shape
    L = N_META + S
    T = B * L
    H = MLSTM_HEADS
    BM_IN = 688
    BM_OUT = 688
    BM_NORM1 = 688
    BM_NORM = 192
    TB = 192
    assert T % BM_IN == 0 and T % BM_OUT == 0 and T % BM_NORM1 == 0 and T % BM_NORM == 0 and T % TB == 0
    assert w_in.shape[0] == 1

    meta = jnp.broadcast_to(meta_tokens[None].astype(x.dtype), (B, N_META, D))
    h0 = jnp.concatenate([meta, x], axis=1).reshape(T, D)

    l = 0
    w_in_t = w_in.reshape(D, PROJ_COLS + 2 * H).T
    bg = jnp.pad(jnp.concatenate([b_igate[l], b_fgate[l]]), (0, LANES - 2 * H)).reshape(1, LANES)
    n1, gates = _norm1(h0, norm1_g[l].reshape(1, D), w_in_t, bg, BM_NORM1)

    proj3 = _inproj(n1, w_in_t, PROJ_COLS, BM_IN, 1024).reshape(B, L, PROJ_COLS)
    pool_out = _pool(proj3, w_pool_mix[l].astype(BF16), pool_scale[l].reshape(1, POOL_WIDTH))
    mlstm_out = _mlstm(proj3, gates.reshape(B, L, LANES))

    h1 = _outproj(pool_out.reshape(T, POOL_WIDTH), mlstm_out.reshape(T, MLSTM_WIDTH),
                  w_out.reshape(D, D), h0, BM_OUT, 512)

    n2p, top_e, probs = _router(h1, norm2_g[l].reshape(1, D), w_router.reshape(D, N_EXPERTS).T,
                                b_router.reshape(1, N_EXPERTS), BM_NORM)

    row_tok, dest_km, sb_e, sb_rb0, sb_nb, n_used = _routing(top_e[:, :TOP_K], TB)
    y = _experts(sb_e, sb_rb0, sb_nb, n_used, row_tok, n2p,
                 w_gu.reshape(N_EXPERTS, D, 2 * D_FF), b_gu.reshape(N_EXPERTS, 1, 2 * D_FF),
                 w_down.reshape(N_EXPERTS, D_FF, D), b_down.reshape(N_EXPERTS, 1, D))
    out = _combine(dest_km, y, h1, probs, norm_f_g.reshape(1, D), TB)
    return out.reshape(B, L, D)[:, N_META:]
```

```python
import functools

import jax
import jax.numpy as jnp
import numpy as np
from jax import lax
from jax.experimental import pallas as pl
from jax.experimental.pallas import tpu as pltpu

D_MODEL = 4096
N_META = 16
POOL_WINDOWS = (2, 4, 8, 16)
POOL_WIDTH = D_MODEL // 4
POOL_GROUP = POOL_WIDTH // len(POOL_WINDOWS)
MLSTM_WIDTH = D_MODEL - POOL_WIDTH
MLSTM_HEADS = 6
MLSTM_V_DIM = MLSTM_WIDTH // MLSTM_HEADS
MLSTM_QK_DIM = MLSTM_V_DIM // 2
GATE_SOFTCAP = 15.0
N_EXPERTS = 32
TOP_K = 4
D_FF = D_MODEL // 2
SWIGLU_ALPHA = 1.702
SWIGLU_LIMIT = 7.0
MOE_BLOCK = 128
EPS = 1e-6

LANES = 128
SEQ_CHUNK = 256
PROJ_COLS = POOL_WIDTH + 2 * MLSTM_HEADS * MLSTM_QK_DIM + 2 * MLSTM_WIDTH
VMEM_LIMIT = 56 * 1024 * 1024
SB_BLOCKS = 10
CHUNK_BLOCKS = 6
HALF_CHUNK = 3
FF_TILE = 256
DOWN_TILE = 1024
N_FF_TILES = D_FF // FF_TILE
N_DOWN_TILES = D_MODEL // DOWN_TILE
Y_WORDS = DOWN_TILE // 2

F32 = jnp.float32
BF16 = jnp.bfloat16


def _params(sem, vmem=VMEM_LIMIT):
    return pltpu.CompilerParams(dimension_semantics=sem, vmem_limit_bytes=vmem)


def _split3(a):
    hi = a.astype(BF16)
    r1 = a - hi.astype(F32)
    mid = r1.astype(BF16)
    lo = (r1 - mid.astype(F32)).astype(BF16)
    return hi, mid, lo


def _dot(a, b):
    return jnp.dot(a, b, preferred_element_type=F32)


def _dotw(a, w):
    return lax.dot_general(a, w, (((1,), (0,)), ((), ())), preferred_element_type=F32)


def _dot_nt(a, wt):
    return lax.dot_general(a, wt, (((1,), (1,)), ((), ())), preferred_element_type=F32)


def _dot3_nt(a, wt):
    a_hi = a.astype(BF16)
    a_lo = (a - a_hi.astype(F32)).astype(BF16)
    w_hi = wt.astype(BF16)
    w_lo = (wt - w_hi.astype(F32)).astype(BF16)
    n = wt.shape[0]
    if n == LANES:
        both = _dot_nt(a_hi, jnp.concatenate([w_hi, w_lo], axis=0))
        return both[:, :n] + (both[:, n:] + _dot_nt(a_lo, w_hi))
    return _dot_nt(a_hi, w_hi) + (_dot_nt(a_hi, w_lo) + _dot_nt(a_lo, w_hi))


def _rms(x, g):
    return x * lax.rsqrt(jnp.mean(x * x, axis=-1, keepdims=True) + EPS) * g


GATE_ROWS = 16


def _norm1_kernel(h_ref, g_ref, wg_ref, bg_ref, n_ref, gate_ref):
    y = _rms(h_ref[...], g_ref[...])
    n_ref[...] = y.astype(BF16)
    sub = lax.broadcasted_iota(jnp.int32, (GATE_ROWS, D_MODEL), 0)
    wg = jnp.where(sub < 2 * MLSTM_HEADS, wg_ref[...], 0.0)
    wg = jnp.concatenate([wg, jnp.zeros((LANES - GATE_ROWS, D_MODEL), F32)], axis=0)
    gate_ref[...] = _dot3_nt(y, wg) + bg_ref[...]


def _norm1(h, g, w_in_t, bg, bm):
    T = h.shape[0]
    return pl.pallas_call(
        _norm1_kernel,
        grid=(T // bm,),
        in_specs=[
            pl.BlockSpec((bm, D_MODEL), lambda i: (i, 0)),
            pl.BlockSpec((1, D_MODEL), lambda i: (0, 0)),
            pl.BlockSpec((GATE_ROWS, D_MODEL), lambda i: (PROJ_COLS // GATE_ROWS, 0)),
            pl.BlockSpec((1, LANES), lambda i: (0, 0)),
        ],
        out_specs=[
            pl.BlockSpec((bm, D_MODEL), lambda i: (i, 0)),
            pl.BlockSpec((bm, LANES), lambda i: (i, 0)),
        ],
        out_shape=[
            jax.ShapeDtypeStruct((T, D_MODEL), BF16),
            jax.ShapeDtypeStruct((T, LANES), F32),
        ],
        compiler_params=_params(("parallel",)),
        name="norm1_gates",
    )(h, g, w_in_t, bg)


def _inproj_kernel(x_ref, wt_ref, o_ref):
    o_ref[...] = _dot_nt(x_ref[...], wt_ref[...]).astype(o_ref.dtype)


def _inproj(x, wt, n_cols, bm, bn):
    M, K = x.shape
    return pl.pallas_call(
        _inproj_kernel,
        grid=(n_cols // bn, M // bm),
        in_specs=[
            pl.BlockSpec((bm, K), lambda j, i: (i, 0)),
            pl.BlockSpec((bn, K), lambda j, i: (j, 0)),
        ],
        out_specs=pl.BlockSpec((bm, bn), lambda j, i: (i, j)),
        out_shape=jax.ShapeDtypeStruct((M, n_cols), F32),
        compiler_params=_params(("parallel", "parallel")),
        name="in_proj",
    )(x, wt)


def _outproj_kernel(p_ref, m_ref, w_ref, h_ref, o_ref):
    o_ref[...] = h_ref[...] + (_dotw(p_ref[...], w_ref[:POOL_WIDTH, :]) + _dotw(m_ref[...], w_ref[POOL_WIDTH:, :]))


def _outproj(p, m, w, h, bm, bn):
    M = p.shape[0]
    K, N = w.shape
    return pl.pallas_call(
        _outproj_kernel,
        grid=(N // bn, M // bm),
        in_specs=[
            pl.BlockSpec((bm, POOL_WIDTH), lambda j, i: (i, 0)),
            pl.BlockSpec((bm, MLSTM_WIDTH), lambda j, i: (i, 0)),
            pl.BlockSpec((K, bn), lambda j, i: (0, j)),
            pl.BlockSpec((bm, bn), lambda j, i: (i, j)),
        ],
        out_specs=pl.BlockSpec((bm, bn), lambda j, i: (i, j)),
        out_shape=jax.ShapeDtypeStruct((M, N), F32),
        compiler_params=_params(("parallel", "parallel")),
        name="out_proj",
    )(p, m, w, h)


def _pool_kernel(u_ref, w_ref, s_ref, o_ref, carry_ref):
    c = pl.program_id(1)

    @pl.when(c == 0)
    def _():
        carry_ref[...] = jnp.zeros_like(carry_ref)

    u = u_ref[0]
    ext = jnp.concatenate([carry_ref[...], u], axis=0)
    carry_ref[...] = u[SEQ_CHUNK - 16:, :]
    pos = c * SEQ_CHUNK + lax.broadcasted_iota(jnp.int32, (SEQ_CHUNK, 1), 0)
    for g, win in enumerate(POOL_WINDOWS):
        cols = slice(g * POOL_GROUP, (g + 1) * POOL_GROUP)
        s = ext[:, cols]
        span = 1
        while span < win:
            s = s + pltpu.roll(s, span, axis=0)
            span *= 2
        cnt = jnp.minimum(pos + 1, win).astype(F32)
        d = s[16:, :] / cnt - u[:, cols]
        y = _dot(d.astype(BF16), w_ref[g]) * s_ref[:, cols]
        o_ref[0, :, cols] = y.astype(o_ref.dtype)


def _pool(proj3, w_mix, scale):
    B, L, _ = proj3.shape
    nc = pl.cdiv(L, SEQ_CHUNK)
    return pl.pallas_call(
        _pool_kernel,
        grid=(B, nc),
        in_specs=[
            pl.BlockSpec((1, SEQ_CHUNK, POOL_WIDTH), lambda b, c: (b, c, 0)),
            pl.BlockSpec((len(POOL_WINDOWS), POOL_GROUP, POOL_GROUP), lambda b, c: (0, 0, 0)),
            pl.BlockSpec((1, POOL_WIDTH), lambda b, c: (0, 0)),
        ],
        out_specs=pl.BlockSpec((1, SEQ_CHUNK, POOL_WIDTH), lambda b, c: (b, c, 0)),
        out_shape=jax.ShapeDtypeStruct((B, L, POOL_WIDTH), BF16),
        scratch_shapes=[pltpu.VMEM((16, POOL_WIDTH), F32)],
        compiler_params=_params(("parallel", "arbitrary")),
        name="pool_mixer",
    )(proj3, w_mix, scale)


def _soft_cap(a):
    return GATE_SOFTCAP * jnp.tanh(a / GATE_SOFTCAP)


def _log_sigmoid(a):
    return jnp.minimum(a, 0.0) - jnp.log1p(jnp.exp(-jnp.abs(a)))


HEADS_PER_STEP = 2


def _mlstm_kernel(seq_len, q_ref, k_ref, v_ref, o_ref, gate_ref, out_ref, c_ref, n_ref, m_ref):
    c = pl.program_id(2)
    Lc = SEQ_CHUNK
    dk, dv = MLSTM_QK_DIM, MLSTM_V_DIM

    @pl.when(c == 0)
    def _():
        c_ref[...] = jnp.zeros_like(c_ref)
        n_ref[...] = jnp.zeros_like(n_ref)
        m_ref[...] = jnp.zeros_like(m_ref)

    row = lax.broadcasted_iota(jnp.int32, (Lc, 1), 0)
    col = lax.broadcasted_iota(jnp.int32, (1, Lc), 1)
    ok_col = (c * Lc + row) < seq_len
    causal = col <= row
    tri = causal.astype(BF16)
    tri_t = (row <= col).astype(BF16)
    lane = lax.broadcasted_iota(jnp.int32, (1, LANES), 1)
    capped = _soft_cap(gate_ref[0])
    gc = jnp.where(ok_col, jnp.where(lane < MLSTM_HEADS, capped, _log_sigmoid(capped)), 0.0)
    gt = gc.T
    for i in range(HEADS_PER_STEP):
        _mlstm_head(pl.program_id(1) * HEADS_PER_STEP + i, ok_col, causal, tri, tri_t, gc, gt,
                    q_ref[0, :, i * dk:(i + 1) * dk], k_ref[0, :, i * dk:(i + 1) * dk],
                    v_ref[0, :, i * dv:(i + 1) * dv], o_ref[0, :, i * dv:(i + 1) * dv],
                    out_ref.at[0, :, i * dv:(i + 1) * dv], c_ref.at[i], n_ref.at[i], m_ref.at[i])


def _mlstm_head(h, ok_col, causal, tri, tri_t, gc, gt, q_in, k_in, v_in, o_in, out_ref, c_ref, n_ref, m_ref):
    Lc = SEQ_CHUNK
    q = jnp.where(ok_col, q_in, 0.0) * (MLSTM_QK_DIM ** -0.5)
    k = jnp.where(ok_col, k_in, 0.0)
    v = jnp.where(ok_col, v_in, 0.0)

    lane = lax.broadcasted_iota(jnp.int32, (1, LANES), 1)
    sub = lax.broadcasted_iota(jnp.int32, (LANES, 1), 0)
    i_c = jnp.sum(jnp.where(lane == h, gc, 0.0), axis=1, keepdims=True)
    f_c = jnp.sum(jnp.where(lane == h + MLSTM_HEADS, gc, 0.0), axis=1, keepdims=True)
    i_r = jnp.sum(jnp.where(sub == h, gt, 0.0), axis=0, keepdims=True)
    f_r = jnp.sum(jnp.where(sub == h + MLSTM_HEADS, gt, 0.0), axis=0, keepdims=True)

    fb_c = jnp.broadcast_to(f_c, (Lc, LANES))
    b_c = sum(_dot(tri, p) for p in _split3(fb_c))[:, 0:1]
    fb_r = jnp.broadcast_to(f_r, (8, Lc))
    b_r8 = sum(_dot(p, tri_t) for p in _split3(fb_r))
    b_r = b_r8[0:1, :]
    g_tot = b_r8[0:1, Lc - 1:Lc]

    m_prev = m_ref[...]
    dlog = jnp.where(causal, b_c - b_r + i_r, -jnp.inf)
    inter_log = b_c + m_prev
    m_out = jnp.maximum(inter_log, jnp.max(dlog, axis=1, keepdims=True))
    wts = jnp.exp(dlog - m_out)
    inter_w = jnp.exp(inter_log - m_out)

    qb = q.astype(BF16)
    kb = k.astype(BF16)
    vb = v.astype(BF16)
    s = lax.dot_general(qb, kb, (((1,), (1,)), ((), ())), preferred_element_type=F32) * wts
    num = _dot(s.astype(BF16), vb) + inter_w * _dot(qb, c_ref[...].astype(BF16))
    den = jnp.sum(s, axis=1, keepdims=True) + inter_w * jnp.sum(q * n_ref[...], axis=1, keepdims=True)
    hh = num / jnp.maximum(jnp.abs(den), jnp.exp(-m_out))
    out_ref[...] = (jax.nn.sigmoid(o_in) * hh).astype(out_ref.dtype)

    a_c = g_tot - b_c + i_c
    m_new = jnp.maximum(g_tot + m_prev, jnp.max(a_c, axis=0, keepdims=True))
    wk = k * jnp.exp(a_c - m_new)
    decay = jnp.exp(g_tot + m_prev - m_new)
    c_ref[...] = decay * c_ref[...] + lax.dot_general(
        wk.astype(BF16), vb, (((0,), (0,)), ((), ())), preferred_element_type=F32)
    n_ref[...] = decay * n_ref[...] + jnp.sum(wk, axis=0, keepdims=True)
    m_ref[...] = m_new


def _mlstm(proj3, gates3):
    B, L, _ = proj3.shape
    nc = pl.cdiv(L, SEQ_CHUNK)
    G = HEADS_PER_STEP
    dk, dv, H = G * MLSTM_QK_DIM, G * MLSTM_V_DIM, MLSTM_HEADS // G
    q0 = POOL_WIDTH // dk
    k0 = q0 + H
    v0 = (POOL_WIDTH + 2 * MLSTM_HEADS * MLSTM_QK_DIM) // dv
    o0 = v0 + H
    return pl.pallas_call(
        functools.partial(_mlstm_kernel, L),
        grid=(B, H, nc),
        in_specs=[
            pl.BlockSpec((1, SEQ_CHUNK, dk), lambda b, h, c: (b, c, q0 + h)),
            pl.BlockSpec((1, SEQ_CHUNK, dk), lambda b, h, c: (b, c, k0 + h)),
            pl.BlockSpec((1, SEQ_CHUNK, dv), lambda b, h, c: (b, c, v0 + h)),
            pl.BlockSpec((1, SEQ_CHUNK, dv), lambda b, h, c: (b, c, o0 + h)),
            pl.BlockSpec((1, SEQ_CHUNK, LANES), lambda b, h, c: (b, c, 0)),
        ],
        out_specs=pl.BlockSpec((1, SEQ_CHUNK, dv), lambda b, h, c: (b, c, h)),
        out_shape=jax.ShapeDtypeStruct((B, L, MLSTM_WIDTH), BF16),
        scratch_shapes=[
            pltpu.VMEM((G, MLSTM_QK_DIM, MLSTM_V_DIM), F32),
            pltpu.VMEM((G, 1, MLSTM_QK_DIM), F32),
            pltpu.VMEM((G, 1, 1), F32),
        ],
        compiler_params=_params(("parallel", "parallel", "arbitrary")),
        name="mlstm",
    )(proj3, proj3, proj3, proj3, gates3)


def _pack_bf16_pair(lo, hi):
    lo_b = lax.bitcast_convert_type(lo.astype(BF16).astype(F32), jnp.uint32)
    hi_b = lax.bitcast_convert_type(hi.astype(BF16).astype(F32), jnp.uint32)
    return (lo_b >> 16) | hi_b


def _unpack_bf16_pair(w):
    lo = lax.bitcast_convert_type(w << 16, F32).astype(BF16)
    hi = lax.bitcast_convert_type(w & jnp.uint32(0xFFFF0000), F32).astype(BF16)
    return lo, hi


def _router_kernel(h_ref, g_ref, wr_ref, br_ref, n_ref, e_ref, p_ref):
    y = _rms(h_ref[...], g_ref[...])
    half = D_MODEL // 2
    n_ref[...] = _pack_bf16_pair(y[:, :half], y[:, half:])
    logits = _dot3_nt(y, wr_ref[...]) + br_ref[...]
    lane = lax.broadcasted_iota(jnp.int32, logits.shape, 1).astype(F32)
    l = jnp.where(lane < N_EXPERTS, logits, -jnp.inf)
    vals, idxs = [], []
    for _ in range(TOP_K):
        m = jnp.max(l, axis=1, keepdims=True)
        idx = jnp.min(jnp.where(l == m, lane, float(LANES)), axis=1, keepdims=True)
        vals.append(m)
        idxs.append(idx)
        l = jnp.where(lane == idx, -jnp.inf, l)
    ex = [jnp.exp(v - vals[0]) for v in vals]
    tot = ex[0] + ex[1] + ex[2] + ex[3]
    e_out = jnp.zeros(logits.shape, F32)
    p_out = jnp.zeros(logits.shape, F32)
    for kk in range(TOP_K):
        e_out = jnp.where(lane == kk, idxs[kk], e_out)
        p_out = jnp.where(lane == kk, ex[kk] / tot, p_out)
    e_ref[...] = e_out.astype(jnp.int32)
    p_ref[...] = p_out


def _router(h, g, wr, br, bm):
    T = h.shape[0]
    return pl.pallas_call(
        _router_kernel,
        grid=(T // bm,),
        in_specs=[
            pl.BlockSpec((bm, D_MODEL), lambda i: (i, 0)),
            pl.BlockSpec((1, D_MODEL), lambda i: (0, 0)),
            pl.BlockSpec((N_EXPERTS, D_MODEL), lambda i: (0, 0)),
            pl.BlockSpec((1, N_EXPERTS), lambda i: (0, 0)),
        ],
        out_specs=[
            pl.BlockSpec((bm, D_MODEL // 2), lambda i: (i, 0)),
            pl.BlockSpec((bm, N_EXPERTS), lambda i: (i, 0)),
            pl.BlockSpec((bm, N_EXPERTS), lambda i: (i, 0)),
        ],
        out_shape=[
            jax.ShapeDtypeStruct((T, D_MODEL // 2), jnp.uint32),
            jax.ShapeDtypeStruct((T, N_EXPERTS), jnp.int32),
            jax.ShapeDtypeStruct((T, N_EXPERTS), F32),
        ],
        compiler_params=_params(("parallel",)),
        name="norm2_router",
    )(h, g, wr, br)


def _row_gather_start(src_hbm, idx_ref, base, n_rows, buf, slot, sem):
    def body(r, carry):
        tok = idx_ref[base + r]
        pltpu.make_async_copy(src_hbm.at[pl.ds(tok, 1), :], buf.at[slot, pl.ds(r, 1), :], sem.at[slot]).start()
        return carry
    lax.fori_loop(0, n_rows, body, 0, unroll=32)


def _row_gather_wait(src_hbm, n_rows, buf, slot, sem):
    pltpu.make_async_copy(src_hbm.at[pl.ds(0, n_rows), :], buf.at[slot], sem.at[slot]).wait()


def _experts_kernel(sbe_ref, rb0_ref, nbk_ref, used_ref, tok_ref,
                    x_hbm, wgu_hbm, wd_hbm, bgu_ref, bd_ref,
                    y_hbm,
                    x_res, hmid, wgl_buf, wd_buf, ystage, gsem, wsem, dsem, ysem, ypend, gcnt):
    s = pl.program_id(0)
    ns = pl.num_programs(0)
    nb = nbk_ref[s]
    expert = sbe_ref[s]
    nxt = jnp.minimum(s + 1, ns - 1)
    next_active = jnp.logical_and(s + 1 < ns, nbk_ref[nxt] > 0)
    half = D_MODEL // 2
    n_blocks_total = y_hbm.shape[0] // MOE_BLOCK

    def gateup_copies(e, j, slot):
        gate_cols = pl.ds(pl.multiple_of(j * FF_TILE, FF_TILE), FF_TILE)
        lin_cols = pl.ds(pl.multiple_of(D_FF + j * FF_TILE, FF_TILE), FF_TILE)
        return (pltpu.make_async_copy(wgu_hbm.at[e, :, gate_cols], wgl_buf.at[slot, 0], wsem.at[slot]),
                pltpu.make_async_copy(wgu_hbm.at[e, :, lin_cols], wgl_buf.at[slot, 1], wsem.at[slot]))

    def down_copy(e, j, slot):
        cols = pl.ds(pl.multiple_of(j * DOWN_TILE, DOWN_TILE), DOWN_TILE)
        return pltpu.make_async_copy(wd_hbm.at[e, :, cols], wd_buf.at[slot], dsem.at[slot])

    def x_block_copy(tok, r, n):
        return pltpu.make_async_copy(x_hbm.at[pl.ds(tok, n), :], x_res.at[pl.ds(r, n), :], gsem.at[0])

    def gather_row(sb_base, r):
        tok = tok_ref[jnp.minimum(sb_base + r, tok_ref.shape[0] - 1)]
        x_block_copy(tok, r, 1).start()

    def gather_upto(sb, n_rows):
        base = rb0_ref[sb] * MOE_BLOCK

        def body(r, carry):
            gather_row(base, r)
            return carry
        lax.fori_loop(gcnt[0], n_rows, body, 0)
        gcnt[0] = jnp.maximum(gcnt[0], n_rows)

    def gather_some(sb, n):
        base = rb0_ref[sb] * MOE_BLOCK
        cur = gcnt[0]
        for i in range(n):
            gather_row(base, cur + i)
        gcnt[0] = cur + n

    def gather_wait():
        n = gcnt[0]
        n_full = n // MOE_BLOCK

        def block_body(r, carry):
            x_block_copy(0, 0, MOE_BLOCK).wait()
            return carry

        def row_body(r, carry):
            x_block_copy(0, 0, 1).wait()
            return carry
        lax.fori_loop(0, n_full, block_body, 0)
        lax.fori_loop(n_full * MOE_BLOCK, n, row_body, 0)
        gcnt[0] = 0

    def y_copy(piece, row0, word0):
        return pltpu.make_async_copy(
            ystage.at[pl.ds(piece * MOE_BLOCK, MOE_BLOCK), :],
            y_hbm.at[pl.ds(pl.multiple_of(row0, MOE_BLOCK), MOE_BLOCK), pl.ds(pl.multiple_of(word0, Y_WORDS), Y_WORDS)],
            ysem.at[0])

    def y_drain():
        for piece in range(CHUNK_BLOCKS):
            @pl.when(ypend[0] > piece)
            def _():
                y_copy(0, 0, 0).wait()
        ypend[0] = 0

    def y_emit(val, n_pieces, row0, word0):
        ystage[pl.ds(0, n_pieces * MOE_BLOCK), :] = _pack_bf16_pair(val[:, :Y_WORDS], val[:, Y_WORDS:])
        for piece in range(n_pieces):
            y_copy(piece, row0 + piece * MOE_BLOCK, word0).start()
        ypend[0] = n_pieces

    def for_chunks(n_blocks, fn):
        n_full = n_blocks // CHUNK_BLOCKS

        def body(c, carry):
            fn(c * CHUNK_BLOCKS, CHUNK_BLOCKS)
            return carry
        lax.fori_loop(0, n_full, body, 0)
        done = n_full * CHUNK_BLOCKS
        rem = n_blocks - done

        @pl.when(rem >= HALF_CHUNK)
        def _():
            fn(done, HALF_CHUNK)
        tail = jnp.where(rem >= HALF_CHUNK, rem - HALF_CHUNK, rem)
        tail0 = jnp.where(rem >= HALF_CHUNK, done + HALF_CHUNK, done)
        for k in range(1, HALF_CHUNK):
            @pl.when(tail == k)
            def _():
                fn(tail0, k)

    @pl.when(s == 0)
    def _():
        ypend[0] = 0
        gcnt[0] = 0

        @pl.when(nb > 0)
        def _():
            for c in gateup_copies(expert, 0, 0):
                c.start()
        gather_upto(0, nb * MOE_BLOCK)

    @pl.when(nb > 0)
    def _():
        gather_wait()
        row_base = rb0_ref[s] * MOE_BLOCK

        def gateup_tile(j, carry):
            slot = lax.rem(j, 2)
            for c in gateup_copies(expert, j, slot):
                c.wait()

            @pl.when(j + 1 < N_FF_TILES)
            def _():
                for c in gateup_copies(expert, j + 1, 1 - slot):
                    c.start()

            @pl.when(j + 1 == N_FF_TILES)
            def _():
                down_copy(expert, 0, 0).start()

            bg = bgu_ref[0, pl.ds(j, 1), :]
            bl = bgu_ref[0, pl.ds(N_FF_TILES + j, 1), :]

            def chunk(b0, k):
                rows = pl.ds(pl.multiple_of(b0 * MOE_BLOCK, MOE_BLOCK), k * MOE_BLOCK)
                x_lo, x_hi = _unpack_bf16_pair(x_res[rows, :])
                gate = _dotw(x_lo, wgl_buf[slot, 0, :half, :]) + _dotw(x_hi, wgl_buf[slot, 0, half:, :]) + bg
                lin = _dotw(x_lo, wgl_buf[slot, 1, :half, :]) + _dotw(x_hi, wgl_buf[slot, 1, half:, :]) + bl
                gate = jnp.minimum(gate, SWIGLU_LIMIT)
                lin = jnp.clip(lin, -SWIGLU_LIMIT, SWIGLU_LIMIT)
                hmid[j, rows, :] = (gate * jax.nn.sigmoid(SWIGLU_ALPHA * gate) * (lin + 1.0)).astype(BF16)
            for_chunks(nb, chunk)
            return carry
        lax.fori_loop(0, N_FF_TILES, gateup_tile, 0)

        def down_tile(j, carry):
            slot = lax.rem(j, 2)
            down_copy(expert, j, slot).wait()

            @pl.when(j + 1 < N_DOWN_TILES)
            def _():
                down_copy(expert, j + 1, 1 - slot).start()

            @pl.when(jnp.logical_and(j + 1 == N_DOWN_TILES, next_active))
            def _():
                for c in gateup_copies(sbe_ref[nxt], 0, 0):
                    c.start()

            bd = bd_ref[0, pl.ds(j, 1), :]
            word0 = j * Y_WORDS

            def chunk(b0, k):
                gather_some(nxt, k * (MOE_BLOCK // N_DOWN_TILES))
                rows = pl.ds(pl.multiple_of(b0 * MOE_BLOCK, MOE_BLOCK), k * MOE_BLOCK)
                acc = bd + _dotw(hmid[0, rows, :], wd_buf[slot, 0:FF_TILE, :])
                for jj in range(1, N_FF_TILES):
                    acc = acc + _dotw(hmid[jj, rows, :], wd_buf[slot, jj * FF_TILE:(jj + 1) * FF_TILE, :])
                y_drain()
                y_emit(acc, k, row_base + b0 * MOE_BLOCK, word0)
            for_chunks(nb, chunk)
            return carry
        lax.fori_loop(0, N_DOWN_TILES, down_tile, 0)
        gather_upto(nxt, jnp.where(s + 1 < ns, nbk_ref[nxt] * MOE_BLOCK, 0))

    @pl.when(s == ns - 1)
    def _():
        gather_wait()
        y_drain()
        ystage[pl.ds(0, MOE_BLOCK), :] = jnp.zeros((MOE_BLOCK, Y_WORDS), jnp.uint32)

        def zero_copy(b, j):
            return y_copy(0, b * MOE_BLOCK, j * Y_WORDS)

        def start_body(b, carry):
            for j in range(N_DOWN_TILES):
                zero_copy(b, j).start()
            return carry

        def wait_body(b, carry):
            for j in range(N_DOWN_TILES):
                zero_copy(0, 0).wait()
            return carry
        lax.fori_loop(used_ref[0], n_blocks_total, start_body, 0)
        lax.fori_loop(used_ref[0], n_blocks_total, wait_body, 0)


def _experts(sb_e, sb_rb0, sb_nb, n_used, row_tok, n2p, w_gu, b_gu, w_down, b_down):
    R = row_tok.shape[0]
    S = sb_e.shape[0]
    sb_rows = SB_BLOCKS * MOE_BLOCK

    grid_spec = pltpu.PrefetchScalarGridSpec(
        num_scalar_prefetch=5,
        grid=(S,),
        in_specs=[
            pl.BlockSpec(memory_space=pl.ANY),
            pl.BlockSpec(memory_space=pl.ANY),
            pl.BlockSpec(memory_space=pl.ANY),
            pl.BlockSpec((1, 2 * N_FF_TILES, FF_TILE), lambda s, e, r0, nbk, u, tok: (e[s], 0, 0)),
            pl.BlockSpec((1, N_DOWN_TILES, DOWN_TILE), lambda s, e, r0, nbk, u, tok: (e[s], 0, 0)),
        ],
        out_specs=pl.BlockSpec(memory_space=pl.ANY),
        scratch_shapes=[
            pltpu.VMEM((sb_rows, D_MODEL // 2), jnp.uint32),
            pltpu.VMEM((N_FF_TILES, sb_rows, FF_TILE), BF16),
            pltpu.VMEM((2, 2, D_MODEL, FF_TILE), F32),
            pltpu.VMEM((2, D_FF, DOWN_TILE), F32),
            pltpu.VMEM((CHUNK_BLOCKS * MOE_BLOCK, Y_WORDS), jnp.uint32),
            pltpu.SemaphoreType.DMA((1,)),
            pltpu.SemaphoreType.DMA((2,)),
            pltpu.SemaphoreType.DMA((2,)),
            pltpu.SemaphoreType.DMA((1,)),
            pltpu.SMEM((1,), jnp.int32),
            pltpu.SMEM((1,), jnp.int32),
        ],
    )
    return pl.pallas_call(
        _experts_kernel,
        grid_spec=grid_spec,
        out_shape=jax.ShapeDtypeStruct((R, D_MODEL // 2), jnp.uint32),
        compiler_params=_params(("arbitrary",)),
        name="moe_experts",
    )(sb_e, sb_rb0, sb_nb, n_used, row_tok, n2p, w_gu, w_down,
      b_gu.reshape(N_EXPERTS, 2 * N_FF_TILES, FF_TILE), b_down.reshape(N_EXPERTS, N_DOWN_TILES, DOWN_TILE))


def _combine_kernel(tb, dest_ref, piece_ref, y_hbm, h_ref, p_ref, g_ref, out_hbm, buf, ostage, sem, osem, opend):
    i = pl.program_id(0)
    nb = pl.num_programs(0)
    rows = tb * TOP_K
    n_pieces = tb // N_META

    def out_copy(slot, piece, dst):
        return pltpu.make_async_copy(ostage.at[slot, pl.ds(piece * N_META, N_META), :],
                                     out_hbm.at[pl.ds(pl.multiple_of(dst, N_META), N_META), :], osem.at[slot])

    def out_drain(slot):
        for piece in range(n_pieces):
            @pl.when(opend[slot] > piece)
            def _():
                out_copy(slot, 0, 0).wait()
        opend[slot] = 0

    @pl.when(i == 0)
    def _():
        opend[0] = 0
        opend[1] = 0
        _row_gather_start(y_hbm, dest_ref, 0, rows, buf, 0, sem)

    def step(slot):
        @pl.when(i + 1 < nb)
        def _():
            _row_gather_start(y_hbm, dest_ref, (i + 1) * rows, rows, buf, 1 - slot, sem)

        _row_gather_wait(y_hbm, rows, buf, slot, sem)
        acc = h_ref[...]
        p = p_ref[...]
        for kk in range(TOP_K):
            w = buf[slot, pl.ds(kk * tb, tb), :]
            lo = lax.bitcast_convert_type(w << 16, F32)
            hi = lax.bitcast_convert_type(w & jnp.uint32(0xFFFF0000), F32)
            pieces = []
            for j in range(N_DOWN_TILES):
                pieces += [lo[:, j * Y_WORDS:(j + 1) * Y_WORDS], hi[:, j * Y_WORDS:(j + 1) * Y_WORDS]]
            acc = acc + p[:, kk:kk + 1] * jnp.concatenate(pieces, axis=1)
        out_drain(slot)
        ostage[slot] = _rms(acc, g_ref[...])
        started = 0
        for piece in range(n_pieces):
            dst = piece_ref[i * n_pieces + piece]

            @pl.when(dst >= 0)
            def _():
                out_copy(slot, piece, dst).start()
            started = started + (dst >= 0).astype(jnp.int32)
        opend[slot] = started

    for slot in range(2):
        pl.when(lax.rem(i, 2) == slot)(functools.partial(step, slot))

    @pl.when(i == nb - 1)
    def _():
        out_drain(0)
        out_drain(1)


def _combine(dest_km, y, h, probs, g, tb, batch, seq):
    T = h.shape[0]
    tokens_per_seq = T // batch
    first = np.arange(T // N_META) * N_META
    pos = first % tokens_per_seq
    piece_dst = np.where(pos >= N_META, (first // tokens_per_seq) * seq + pos - N_META, -1).astype(np.int32)
    grid_spec = pltpu.PrefetchScalarGridSpec(
        num_scalar_prefetch=2,
        grid=(T // tb,),
        in_specs=[
            pl.BlockSpec(memory_space=pl.ANY),
            pl.BlockSpec((tb, D_MODEL), lambda i, d, q: (i, 0)),
            pl.BlockSpec((tb, N_EXPERTS), lambda i, d, q: (i, 0)),
            pl.BlockSpec((1, D_MODEL), lambda i, d, q: (0, 0)),
        ],
        out_specs=pl.BlockSpec(memory_space=pl.ANY),
        scratch_shapes=[
            pltpu.VMEM((2, tb * TOP_K, D_MODEL // 2), jnp.uint32),
            pltpu.VMEM((2, tb, D_MODEL), F32),
            pltpu.SemaphoreType.DMA((2,)),
            pltpu.SemaphoreType.DMA((2,)),
            pltpu.SMEM((2,), jnp.int32),
        ],
    )
    return pl.pallas_call(
        functools.partial(_combine_kernel, tb),
        grid_spec=grid_spec,
        out_shape=jax.ShapeDtypeStruct((batch * seq, D_MODEL), F32),
        compiler_params=_params(("arbitrary",)),
        name="moe_combine_norm",
    )(dest_km, jnp.asarray(piece_dst), y, h, probs, g)


def _routing(top_e, tb):
    T = top_e.shape[0]
    A = T * TOP_K
    n_blocks = -(-(A + N_EXPERTS * (MOE_BLOCK - 1)) // MOE_BLOCK)
    R = n_blocks * MOE_BLOCK
    n_sb = N_EXPERTS + n_blocks // SB_BLOCKS
    i32 = jnp.int32
    e_flat = top_e.reshape(A)
    onehot = (e_flat[:, None] == jnp.arange(N_EXPERTS, dtype=i32)[None, :]).astype(i32)
    csum = jnp.cumsum(onehot, axis=0)
    rank = jnp.sum(csum * onehot, axis=1) - 1
    counts = csum[-1]
    blocks = (counts + MOE_BLOCK - 1) // MOE_BLOCK
    blk_end = jnp.cumsum(blocks)
    blk_start = blk_end - blocks
    dest = (blk_start * MOE_BLOCK)[e_flat] + rank
    row_tok = jnp.zeros((R,), i32).at[dest].set(jnp.arange(A, dtype=i32) // TOP_K)
    dest_km = dest.reshape(T // tb, tb, TOP_K).transpose(0, 2, 1).reshape(A).astype(i32)

    sbs = (blocks + SB_BLOCKS - 1) // SB_BLOCKS
    sb_end = jnp.cumsum(sbs)
    sb_start = sb_end - sbs
    s_idx = jnp.arange(n_sb, dtype=i32)
    active = s_idx < sb_end[-1]
    e_s = jnp.minimum(jnp.sum((sb_end[None, :] <= s_idx[:, None]).astype(i32), axis=1), N_EXPERTS - 1)
    k_s = s_idx - sb_start[e_s]
    sb_rb0 = jnp.where(active, blk_start[e_s] + k_s * SB_BLOCKS, 0)
    sb_nb = jnp.where(active, jnp.minimum(blocks[e_s] - k_s * SB_BLOCKS, SB_BLOCKS), 0)
    sb_e = jnp.where(active, e_s, jnp.max(jnp.where(active, e_s, 0)))
    n_used = blk_end[-1:].astype(i32)
    return row_tok, dest_km, sb_e.astype(i32), sb_rb0.astype(i32), sb_nb.astype(i32), n_used


def kernel(x, meta_tokens, norm1_g, w_in, b_igate, b_fgate, w_pool_mix, pool_scale, w_out, norm2_g, w_router,
           b_router, w_gu, b_gu, w_down, b_down, norm_f_g):
    B, S, D = x.shape
    L = N_META + S
    T = B * L
    H = MLSTM_HEADS
    BM_IN = 688
    BM_OUT = 688
    BM_NORM1 = 688
    BM_NORM = 192
    TB = 192
    assert T % BM_IN == 0 and T % BM_OUT == 0 and T % BM_NORM1 == 0 and T % BM_NORM == 0 and T % TB == 0
    assert w_in.shape[0] == 1
    assert L % N_META == 0 and TB % N_META == 0

    meta = jnp.broadcast_to(meta_tokens[None].astype(x.dtype), (B, N_META, D))
    h0 = jnp.concatenate([meta, x], axis=1).reshape(T, D)

    l = 0
    w_in_t = w_in.reshape(D, PROJ_COLS + 2 * H).T
    bg = jnp.pad(jnp.concatenate([b_igate[l], b_fgate[l]]), (0, LANES - 2 * H)).reshape(1, LANES)
    n1, gates = _norm1(h0, norm1_g[l].reshape(1, D), w_in_t, bg, BM_NORM1)

    proj3 = _inproj(n1, w_in_t, PROJ_COLS, BM_IN, 1024).reshape(B, L, PROJ_COLS)
    pool_out = _pool(proj3, w_pool_mix[l].astype(BF16), pool_scale[l].reshape(1, POOL_WIDTH))
    mlstm_out = _mlstm(proj3, gates.reshape(B, L, LANES))

    h1 = _outproj(pool_out.reshape(T, POOL_WIDTH), mlstm_out.reshape(T, MLSTM_WIDTH),
                  w_out.reshape(D, D), h0, BM_OUT, 512)

    n2p, top_e, probs = _router(h1, norm2_g[l].reshape(1, D), w_router.reshape(D, N_EXPERTS).T,
                                b_router.reshape(1, N_EXPERTS), BM_NORM)

    row_tok, dest_km, sb_e, sb_rb0, sb_nb, n_used = _routing(top_e[:, :TOP_K], TB)
    y = _experts(sb_e, sb_rb0, sb_nb, n_used, row_tok, n2p,
                 w_gu.reshape(N_EXPERTS, D, 2 * D_FF), b_gu.reshape(N_EXPERTS, 1, 2 * D_FF),
                 w_down.reshape(N_EXPERTS, D_FF, D), b_down.reshape(N_EXPERTS, 1, D))
    out = _combine(dest_km, y, h1, probs, norm_f_g.reshape(1, D), TB, B, S)
    return out.reshape(B, S, D)
```

```python
import functools

import jax
import jax.numpy as jnp
import numpy as np
from jax import lax
from jax.experimental import pallas as pl
from jax.experimental.pallas import tpu as pltpu

D_MODEL = 4096
N_META = 16
POOL_WINDOWS = (2, 4, 8, 16)
POOL_WIDTH = D_MODEL // 4
POOL_GROUP = POOL_WIDTH // len(POOL_WINDOWS)
MLSTM_WIDTH = D_MODEL - POOL_WIDTH
MLSTM_HEADS = 6
MLSTM_V_DIM = MLSTM_WIDTH // MLSTM_HEADS
MLSTM_QK_DIM = MLSTM_V_DIM // 2
GATE_SOFTCAP = 15.0
N_EXPERTS = 32
TOP_K = 4
D_FF = D_MODEL // 2
SWIGLU_ALPHA = 1.702
SWIGLU_LIMIT = 7.0
MOE_BLOCK = 128
EPS = 1e-6

LANES = 128
SEQ_CHUNK = 256
PROJ_COLS = POOL_WIDTH + 2 * MLSTM_HEADS * MLSTM_QK_DIM + 2 * MLSTM_WIDTH
VMEM_LIMIT = 56 * 1024 * 1024
SB_BLOCKS = 10
CHUNK_BLOCKS = 6
HALF_CHUNK = 3
FF_TILE = 256
DOWN_TILE = 1024
N_FF_TILES = D_FF // FF_TILE
N_DOWN_TILES = D_MODEL // DOWN_TILE
Y_WORDS = DOWN_TILE // 2

F32 = jnp.float32
BF16 = jnp.bfloat16


def _params(sem, vmem=VMEM_LIMIT):
    return pltpu.CompilerParams(dimension_semantics=sem, vmem_limit_bytes=vmem)


def _split3(a):
    hi = a.astype(BF16)
    r1 = a - hi.astype(F32)
    mid = r1.astype(BF16)
    lo = (r1 - mid.astype(F32)).astype(BF16)
    return hi, mid, lo


def _dot(a, b):
    return jnp.dot(a, b, preferred_element_type=F32)


def _dotw(a, w):
    return lax.dot_general(a, w, (((1,), (0,)), ((), ())), preferred_element_type=F32)


def _dot_nt(a, wt):
    return lax.dot_general(a, wt, (((1,), (1,)), ((), ())), preferred_element_type=F32)


def _dot3_nt(a, wt):
    a_hi = a.astype(BF16)
    a_lo = (a - a_hi.astype(F32)).astype(BF16)
    w_hi = wt.astype(BF16)
    w_lo = (wt - w_hi.astype(F32)).astype(BF16)
    n = wt.shape[0]
    if n == LANES:
        both = _dot_nt(a_hi, jnp.concatenate([w_hi, w_lo], axis=0))
        return both[:, :n] + (both[:, n:] + _dot_nt(a_lo, w_hi))
    return _dot_nt(a_hi, w_hi) + (_dot_nt(a_hi, w_lo) + _dot_nt(a_lo, w_hi))


def _rms(x, g):
    return x * lax.rsqrt(jnp.mean(x * x, axis=-1, keepdims=True) + EPS) * g


GATE_ROWS = 16


def _piece_table(batch, seq):
    L = N_META + seq
    first = np.arange(batch * L // N_META) * N_META
    pos = first % L
    return np.where(pos >= N_META, (first // L) * seq + pos - N_META, -1).astype(np.int32)


def _token_pieces_start(piece_ref, first_piece, n_pieces, x_hbm, meta_hbm, cols, dst, sem):
    def body(p, carry):
        src = piece_ref[first_piece + p]
        rows = dst.at[pl.ds(pl.multiple_of(p * N_META, N_META), N_META), :]

        @pl.when(src >= 0)
        def _():
            pltpu.make_async_copy(x_hbm.at[pl.ds(pl.multiple_of(src, N_META), N_META), cols], rows, sem).start()

        @pl.when(src < 0)
        def _():
            pltpu.make_async_copy(meta_hbm.at[:, cols], rows, sem).start()
        return carry
    lax.fori_loop(0, n_pieces, body, 0)


def _token_pieces_wait(x_hbm, cols, dst, sem):
    pltpu.make_async_copy(x_hbm.at[pl.ds(0, dst.shape[0]), cols], dst, sem).wait()


def _norm1_kernel(piece_ref, x_hbm, meta_hbm, g_ref, wg_ref, bg_ref, n_ref, gate_ref, hbuf, sem):
    i = pl.program_id(0)
    bm = hbuf.shape[1]
    n_pieces = bm // N_META
    all_cols = pl.ds(0, D_MODEL)

    @pl.when(i == 0)
    def _():
        _token_pieces_start(piece_ref, 0, n_pieces, x_hbm, meta_hbm, all_cols, hbuf.at[0], sem.at[0])

    def step(slot):
        @pl.when(i + 1 < pl.num_programs(0))
        def _():
            _token_pieces_start(piece_ref, (i + 1) * n_pieces, n_pieces, x_hbm, meta_hbm, all_cols,
                                hbuf.at[1 - slot], sem.at[1 - slot])
        _token_pieces_wait(x_hbm, all_cols, hbuf.at[slot], sem.at[slot])
        _norm1_body(hbuf[slot], g_ref, wg_ref, bg_ref, n_ref, gate_ref)

    for slot in range(2):
        pl.when(lax.rem(i, 2) == slot)(functools.partial(step, slot))


def _norm1_body(h, g_ref, wg_ref, bg_ref, n_ref, gate_ref):
    y = _rms(h, g_ref[...])
    n_ref[...] = y.astype(BF16)
    sub = lax.broadcasted_iota(jnp.int32, (GATE_ROWS, D_MODEL), 0)
    wg = jnp.where(sub < 2 * MLSTM_HEADS, wg_ref[...], 0.0)
    wg = jnp.concatenate([wg, jnp.zeros((LANES - GATE_ROWS, D_MODEL), F32)], axis=0)
    gate_ref[...] = _dot3_nt(y, wg) + bg_ref[...]


def _norm1(pieces, x2, meta, g, w_in_t, bg, bm):
    T = pieces.shape[0] * N_META
    grid_spec = pltpu.PrefetchScalarGridSpec(
        num_scalar_prefetch=1,
        grid=(T // bm,),
        in_specs=[
            pl.BlockSpec(memory_space=pl.ANY),
            pl.BlockSpec(memory_space=pl.ANY),
            pl.BlockSpec((1, D_MODEL), lambda i, q: (0, 0)),
            pl.BlockSpec((GATE_ROWS, D_MODEL), lambda i, q: (PROJ_COLS // GATE_ROWS, 0)),
            pl.BlockSpec((1, LANES), lambda i, q: (0, 0)),
        ],
        out_specs=[
            pl.BlockSpec((bm, D_MODEL), lambda i, q: (i, 0)),
            pl.BlockSpec((bm, LANES), lambda i, q: (i, 0)),
        ],
        scratch_shapes=[pltpu.VMEM((2, bm, D_MODEL), F32), pltpu.SemaphoreType.DMA((2,))],
    )
    return pl.pallas_call(
        _norm1_kernel,
        grid_spec=grid_spec,
        out_shape=[
            jax.ShapeDtypeStruct((T, D_MODEL), BF16),
            jax.ShapeDtypeStruct((T, LANES), F32),
        ],
        compiler_params=_params(("arbitrary",)),
        name="norm1_gates",
    )(pieces, x2, meta, g, w_in_t, bg)


def _inproj_kernel(x_ref, wt_ref, o_ref):
    o_ref[...] = _dot_nt(x_ref[...], wt_ref[...]).astype(o_ref.dtype)


def _inproj(x, wt, n_cols, bm, bn):
    M, K = x.shape
    return pl.pallas_call(
        _inproj_kernel,
        grid=(n_cols // bn, M // bm),
        in_specs=[
            pl.BlockSpec((bm, K), lambda j, i: (i, 0)),
            pl.BlockSpec((bn, K), lambda j, i: (j, 0)),
        ],
        out_specs=pl.BlockSpec((bm, bn), lambda j, i: (i, j)),
        out_shape=jax.ShapeDtypeStruct((M, n_cols), F32),
        compiler_params=_params(("parallel", "parallel")),
        name="in_proj",
    )(x, wt)


def _outproj_kernel(piece_ref, p_ref, m_ref, w_ref, x_hbm, meta_hbm, o_ref, rbuf, sem):
    j = pl.program_id(0)
    i = pl.program_id(1)
    ni = pl.num_programs(1)
    bm, bn = rbuf.shape[1], rbuf.shape[2]
    n_pieces = bm // N_META
    t = j * ni + i

    def fetch(step, slot):
        cols = pl.ds(pl.multiple_of((step // ni) * bn, bn), bn)
        _token_pieces_start(piece_ref, lax.rem(step, ni) * n_pieces, n_pieces, x_hbm, meta_hbm, cols,
                            rbuf.at[slot], sem.at[slot])

    @pl.when(t == 0)
    def _():
        fetch(0, 0)

    def step(slot):
        @pl.when(t + 1 < pl.num_programs(0) * ni)
        def _():
            fetch(t + 1, 1 - slot)
        acc = _dotw(p_ref[...], w_ref[:POOL_WIDTH, :]) + _dotw(m_ref[...], w_ref[POOL_WIDTH:, :])
        _token_pieces_wait(x_hbm, pl.ds(0, bn), rbuf.at[slot], sem.at[slot])
        o_ref[...] = rbuf[slot] + acc

    for slot in range(2):
        pl.when(lax.rem(t, 2) == slot)(functools.partial(step, slot))


def _outproj(pieces, p, m, w, x2, meta, bm, bn):
    M = p.shape[0]
    K, N = w.shape
    grid_spec = pltpu.PrefetchScalarGridSpec(
        num_scalar_prefetch=1,
        grid=(N // bn, M // bm),
        in_specs=[
            pl.BlockSpec((bm, POOL_WIDTH), lambda j, i, q: (i, 0)),
            pl.BlockSpec((bm, MLSTM_WIDTH), lambda j, i, q: (i, 0)),
            pl.BlockSpec((K, bn), lambda j, i, q: (0, j)),
            pl.BlockSpec(memory_space=pl.ANY),
            pl.BlockSpec(memory_space=pl.ANY),
        ],
        out_specs=pl.BlockSpec((bm, bn), lambda j, i, q: (i, j)),
        scratch_shapes=[pltpu.VMEM((2, bm, bn), F32), pltpu.SemaphoreType.DMA((2,))],
    )
    return pl.pallas_call(
        _outproj_kernel,
        grid_spec=grid_spec,
        out_shape=jax.ShapeDtypeStruct((M, N), F32),
        compiler_params=_params(("arbitrary", "arbitrary")),
        name="out_proj",
    )(pieces, p, m, w, x2, meta)


def _pool_kernel(u_ref, w_ref, s_ref, o_ref, carry_ref):
    c = pl.program_id(1)

    @pl.when(c == 0)
    def _():
        carry_ref[...] = jnp.zeros_like(carry_ref)

    u = u_ref[0]
    ext = jnp.concatenate([carry_ref[...], u], axis=0)
    carry_ref[...] = u[SEQ_CHUNK - 16:, :]
    pos = c * SEQ_CHUNK + lax.broadcasted_iota(jnp.int32, (SEQ_CHUNK, 1), 0)
    for g, win in enumerate(POOL_WINDOWS):
        cols = slice(g * POOL_GROUP, (g + 1) * POOL_GROUP)
        s = ext[:, cols]
        span = 1
        while span < win:
            s = s + pltpu.roll(s, span, axis=0)
            span *= 2
        cnt = jnp.minimum(pos + 1, win).astype(F32)
        d = s[16:, :] / cnt - u[:, cols]
        y = _dot(d.astype(BF16), w_ref[g]) * s_ref[:, cols]
        o_ref[0, :, cols] = y.astype(o_ref.dtype)


def _pool(proj3, w_mix, scale):
    B, L, _ = proj3.shape
    nc = pl.cdiv(L, SEQ_CHUNK)
    return pl.pallas_call(
        _pool_kernel,
        grid=(B, nc),
        in_specs=[
            pl.BlockSpec((1, SEQ_CHUNK, POOL_WIDTH), lambda b, c: (b, c, 0)),
            pl.BlockSpec((len(POOL_WINDOWS), POOL_GROUP, POOL_GROUP), lambda b, c: (0, 0, 0)),
            pl.BlockSpec((1, POOL_WIDTH), lambda b, c: (0, 0)),
        ],
        out_specs=pl.BlockSpec((1, SEQ_CHUNK, POOL_WIDTH), lambda b, c: (b, c, 0)),
        out_shape=jax.ShapeDtypeStruct((B, L, POOL_WIDTH), BF16),
        scratch_shapes=[pltpu.VMEM((16, POOL_WIDTH), F32)],
        compiler_params=_params(("parallel", "arbitrary")),
        name="pool_mixer",
    )(proj3, w_mix, scale)


def _soft_cap(a):
    return GATE_SOFTCAP * jnp.tanh(a / GATE_SOFTCAP)


def _log_sigmoid(a):
    return jnp.minimum(a, 0.0) - jnp.log1p(jnp.exp(-jnp.abs(a)))


HEADS_PER_STEP = 2


def _mlstm_kernel(seq_len, q_ref, k_ref, v_ref, o_ref, gate_ref, out_ref, c_ref, n_ref, m_ref):
    c = pl.program_id(2)
    Lc = SEQ_CHUNK
    dk, dv = MLSTM_QK_DIM, MLSTM_V_DIM

    @pl.when(c == 0)
    def _():
        c_ref[...] = jnp.zeros_like(c_ref)
        n_ref[...] = jnp.zeros_like(n_ref)
        m_ref[...] = jnp.zeros_like(m_ref)

    row = lax.broadcasted_iota(jnp.int32, (Lc, 1), 0)
    col = lax.broadcasted_iota(jnp.int32, (1, Lc), 1)
    ok_col = (c * Lc + row) < seq_len
    causal = col <= row
    tri = causal.astype(BF16)
    tri_t = (row <= col).astype(BF16)
    lane = lax.broadcasted_iota(jnp.int32, (1, LANES), 1)
    capped = _soft_cap(gate_ref[0])
    gc = jnp.where(ok_col, jnp.where(lane < MLSTM_HEADS, capped, _log_sigmoid(capped)), 0.0)
    gt = gc.T
    for i in range(HEADS_PER_STEP):
        _mlstm_head(pl.program_id(1) * HEADS_PER_STEP + i, ok_col, causal, tri, tri_t, gc, gt,
                    q_ref[0, :, i * dk:(i + 1) * dk], k_ref[0, :, i * dk:(i + 1) * dk],
                    v_ref[0, :, i * dv:(i + 1) * dv], o_ref[0, :, i * dv:(i + 1) * dv],
                    out_ref.at[0, :, i * dv:(i + 1) * dv], c_ref.at[i], n_ref.at[i], m_ref.at[i])


def _mlstm_head(h, ok_col, causal, tri, tri_t, gc, gt, q_in, k_in, v_in, o_in, out_ref, c_ref, n_ref, m_ref):
    Lc = SEQ_CHUNK
    q = jnp.where(ok_col, q_in, 0.0) * (MLSTM_QK_DIM ** -0.5)
    k = jnp.where(ok_col, k_in, 0.0)
    v = jnp.where(ok_col, v_in, 0.0)

    lane = lax.broadcasted_iota(jnp.int32, (1, LANES), 1)
    sub = lax.broadcasted_iota(jnp.int32, (LANES, 1), 0)
    i_c = jnp.sum(jnp.where(lane == h, gc, 0.0), axis=1, keepdims=True)
    f_c = jnp.sum(jnp.where(lane == h + MLSTM_HEADS, gc, 0.0), axis=1, keepdims=True)
    i_r = jnp.sum(jnp.where(sub == h, gt, 0.0), axis=0, keepdims=True)
    f_r = jnp.sum(jnp.where(sub == h + MLSTM_HEADS, gt, 0.0), axis=0, keepdims=True)

    fb_c = jnp.broadcast_to(f_c, (Lc, LANES))
    b_c = sum(_dot(tri, p) for p in _split3(fb_c))[:, 0:1]
    fb_r = jnp.broadcast_to(f_r, (8, Lc))
    b_r8 = sum(_dot(p, tri_t) for p in _split3(fb_r))
    b_r = b_r8[0:1, :]
    g_tot = b_r8[0:1, Lc - 1:Lc]

    m_prev = m_ref[...]
    dlog = jnp.where(causal, b_c - b_r + i_r, -jnp.inf)
    inter_log = b_c + m_prev
    m_out = jnp.maximum(inter_log, jnp.max(dlog, axis=1, keepdims=True))
    wts = jnp.exp(dlog - m_out)
    inter_w = jnp.exp(inter_log - m_out)

    qb = q.astype(BF16)
    kb = k.astype(BF16)
    vb = v.astype(BF16)
    s = lax.dot_general(qb, kb, (((1,), (1,)), ((), ())), preferred_element_type=F32) * wts
    num = _dot(s.astype(BF16), vb) + inter_w * _dot(qb, c_ref[...].astype(BF16))
    den = jnp.sum(s, axis=1, keepdims=True) + inter_w * jnp.sum(q * n_ref[...], axis=1, keepdims=True)
    hh = num / jnp.maximum(jnp.abs(den), jnp.exp(-m_out))
    out_ref[...] = (jax.nn.sigmoid(o_in) * hh).astype(out_ref.dtype)

    a_c = g_tot - b_c + i_c
    m_new = jnp.maximum(g_tot + m_prev, jnp.max(a_c, axis=0, keepdims=True))
    wk = k * jnp.exp(a_c - m_new)
    decay = jnp.exp(g_tot + m_prev - m_new)
    c_ref[...] = decay * c_ref[...] + lax.dot_general(
        wk.astype(BF16), vb, (((0,), (0,)), ((), ())), preferred_element_type=F32)
    n_ref[...] = decay * n_ref[...] + jnp.sum(wk, axis=0, keepdims=True)
    m_ref[...] = m_new


def _mlstm(proj3, gates3):
    B, L, _ = proj3.shape
    nc = pl.cdiv(L, SEQ_CHUNK)
    G = HEADS_PER_STEP
    dk, dv, H = G * MLSTM_QK_DIM, G * MLSTM_V_DIM, MLSTM_HEADS // G
    q0 = POOL_WIDTH // dk
    k0 = q0 + H
    v0 = (POOL_WIDTH + 2 * MLSTM_HEADS * MLSTM_QK_DIM) // dv
    o0 = v0 + H
    return pl.pallas_call(
        functools.partial(_mlstm_kernel, L),
        grid=(B, H, nc),
        in_specs=[
            pl.BlockSpec((1, SEQ_CHUNK, dk), lambda b, h, c: (b, c, q0 + h)),
            pl.BlockSpec((1, SEQ_CHUNK, dk), lambda b, h, c: (b, c, k0 + h)),
            pl.BlockSpec((1, SEQ_CHUNK, dv), lambda b, h, c: (b, c, v0 + h)),
            pl.BlockSpec((1, SEQ_CHUNK, dv), lambda b, h, c: (b, c, o0 + h)),
            pl.BlockSpec((1, SEQ_CHUNK, LANES), lambda b, h, c: (b, c, 0)),
        ],
        out_specs=pl.BlockSpec((1, SEQ_CHUNK, dv), lambda b, h, c: (b, c, h)),
        out_shape=jax.ShapeDtypeStruct((B, L, MLSTM_WIDTH), BF16),
        scratch_shapes=[
            pltpu.VMEM((G, MLSTM_QK_DIM, MLSTM_V_DIM), F32),
            pltpu.VMEM((G, 1, MLSTM_QK_DIM), F32),
            pltpu.VMEM((G, 1, 1), F32),
        ],
        compiler_params=_params(("parallel", "parallel", "arbitrary")),
        name="mlstm",
    )(proj3, proj3, proj3, proj3, gates3)


def _pack_bf16_pair(lo, hi):
    lo_b = lax.bitcast_convert_type(lo.astype(BF16).astype(F32), jnp.uint32)
    hi_b = lax.bitcast_convert_type(hi.astype(BF16).astype(F32), jnp.uint32)
    return (lo_b >> 16) | hi_b


def _unpack_bf16_pair(w):
    lo = lax.bitcast_convert_type(w << 16, F32).astype(BF16)
    hi = lax.bitcast_convert_type(w & jnp.uint32(0xFFFF0000), F32).astype(BF16)
    return lo, hi


def _router_kernel(h_ref, g_ref, wr_ref, br_ref, n_ref, e_ref, p_ref):
    y = _rms(h_ref[...], g_ref[...])
    half = D_MODEL // 2
    n_ref[...] = _pack_bf16_pair(y[:, :half], y[:, half:])
    logits = _dot3_nt(y, wr_ref[...]) + br_ref[...]
    lane = lax.broadcasted_iota(jnp.int32, logits.shape, 1).astype(F32)
    l = jnp.where(lane < N_EXPERTS, logits, -jnp.inf)
    vals, idxs = [], []
    for _ in range(TOP_K):
        m = jnp.max(l, axis=1, keepdims=True)
        idx = jnp.min(jnp.where(l == m, lane, float(LANES)), axis=1, keepdims=True)
        vals.append(m)
        idxs.append(idx)
        l = jnp.where(lane == idx, -jnp.inf, l)
    ex = [jnp.exp(v - vals[0]) for v in vals]
    tot = ex[0] + ex[1] + ex[2] + ex[3]
    e_out = jnp.zeros(logits.shape, F32)
    p_out = jnp.zeros(logits.shape, F32)
    for kk in range(TOP_K):
        e_out = jnp.where(lane == kk, idxs[kk], e_out)
        p_out = jnp.where(lane == kk, ex[kk] / tot, p_out)
    e_ref[...] = e_out.astype(jnp.int32)
    p_ref[...] = p_out


def _router(h, g, wr, br, bm):
    T = h.shape[0]
    return pl.pallas_call(
        _router_kernel,
        grid=(T // bm,),
        in_specs=[
            pl.BlockSpec((bm, D_MODEL), lambda i: (i, 0)),
            pl.BlockSpec((1, D_MODEL), lambda i: (0, 0)),
            pl.BlockSpec((N_EXPERTS, D_MODEL), lambda i: (0, 0)),
            pl.BlockSpec((1, N_EXPERTS), lambda i: (0, 0)),
        ],
        out_specs=[
            pl.BlockSpec((bm, D_MODEL // 2), lambda i: (i, 0)),
            pl.BlockSpec((bm, N_EXPERTS), lambda i: (i, 0)),
            pl.BlockSpec((bm, N_EXPERTS), lambda i: (i, 0)),
        ],
        out_shape=[
            jax.ShapeDtypeStruct((T, D_MODEL // 2), jnp.uint32),
            jax.ShapeDtypeStruct((T, N_EXPERTS), jnp.int32),
            jax.ShapeDtypeStruct((T, N_EXPERTS), F32),
        ],
        compiler_params=_params(("parallel",)),
        name="norm2_router",
    )(h, g, wr, br)


ISSUE_GROUP = 32


def _row_gather_start(src_hbm, idx_ref, base, n_rows, buf, slot, sem):
    def body(g, carry):
        r0 = pl.multiple_of(g * ISSUE_GROUP, ISSUE_GROUP)
        for i in range(ISSUE_GROUP):
            tok = idx_ref[base + r0 + i]
            pltpu.make_async_copy(src_hbm.at[pl.ds(tok, 1), :], buf.at[slot, pl.ds(r0 + i, 1), :],
                                  sem.at[slot]).start()
        return carry
    lax.fori_loop(0, n_rows // ISSUE_GROUP, body, 0)


def _row_gather_wait(src_hbm, n_rows, buf, slot, sem):
    pltpu.make_async_copy(src_hbm.at[pl.ds(0, n_rows), :], buf.at[slot], sem.at[slot]).wait()


def _experts_kernel(sbe_ref, rb0_ref, nbk_ref, used_ref, tok_ref,
                    x_hbm, wgu_hbm, wd_hbm, bgu_ref, bd_ref,
                    y_hbm,
                    x_res, hmid, wgl_buf, wd_buf, ystage, gsem, wsem, dsem, ysem, ypend, ycnt, gcnt):
    s = pl.program_id(0)
    ns = pl.num_programs(0)
    nb = nbk_ref[s]
    expert = sbe_ref[s]
    nxt = jnp.minimum(s + 1, ns - 1)
    next_active = jnp.logical_and(s + 1 < ns, nbk_ref[nxt] > 0)
    half = D_MODEL // 2
    n_blocks_total = y_hbm.shape[0] // MOE_BLOCK

    def gateup_copies(e, j, slot):
        gate_cols = pl.ds(pl.multiple_of(j * FF_TILE, FF_TILE), FF_TILE)
        lin_cols = pl.ds(pl.multiple_of(D_FF + j * FF_TILE, FF_TILE), FF_TILE)
        return (pltpu.make_async_copy(wgu_hbm.at[e, :, gate_cols], wgl_buf.at[slot, 0], wsem.at[slot]),
                pltpu.make_async_copy(wgu_hbm.at[e, :, lin_cols], wgl_buf.at[slot, 1], wsem.at[slot]))

    def down_copy(e, j, slot):
        cols = pl.ds(pl.multiple_of(j * DOWN_TILE, DOWN_TILE), DOWN_TILE)
        return pltpu.make_async_copy(wd_hbm.at[e, :, cols], wd_buf.at[slot], dsem.at[slot])

    def x_block_copy(tok, r, n):
        return pltpu.make_async_copy(x_hbm.at[pl.ds(tok, n), :], x_res.at[pl.ds(r, n), :], gsem.at[0])

    def gather_row(sb_base, r):
        tok = tok_ref[jnp.minimum(sb_base + r, tok_ref.shape[0] - 1)]
        x_block_copy(tok, r, 1).start()

    def gather_upto(sb, n_rows):
        base = rb0_ref[sb] * MOE_BLOCK

        def body(r, carry):
            gather_row(base, r)
            return carry
        lax.fori_loop(gcnt[0], n_rows, body, 0)
        gcnt[0] = jnp.maximum(gcnt[0], n_rows)

    def gather_some(sb, n):
        base = rb0_ref[sb] * MOE_BLOCK
        cur = gcnt[0]
        for i in range(n):
            gather_row(base, cur + i)
        gcnt[0] = cur + n

    def gather_wait():
        n = gcnt[0]
        n_full = n // MOE_BLOCK

        def block_body(r, carry):
            x_block_copy(0, 0, MOE_BLOCK).wait()
            return carry

        def row_body(r, carry):
            x_block_copy(0, 0, 1).wait()
            return carry
        lax.fori_loop(0, n_full, block_body, 0)
        lax.fori_loop(n_full * MOE_BLOCK, n, row_body, 0)
        gcnt[0] = 0

    def y_copy(slot, piece, row0, word0):
        return pltpu.make_async_copy(
            ystage.at[slot, pl.ds(piece * MOE_BLOCK, MOE_BLOCK), :],
            y_hbm.at[pl.ds(pl.multiple_of(row0, MOE_BLOCK), MOE_BLOCK), pl.ds(pl.multiple_of(word0, Y_WORDS), Y_WORDS)],
            ysem.at[slot])

    def y_drain(slot):
        for piece in range(CHUNK_BLOCKS):
            @pl.when(ypend[slot] > piece)
            def _():
                y_copy(slot, 0, 0, 0).wait()
        ypend[slot] = 0

    def y_emit(val, n_pieces, slot, row0, word0):
        ystage[slot, pl.ds(0, n_pieces * MOE_BLOCK), :] = _pack_bf16_pair(val[:, :Y_WORDS], val[:, Y_WORDS:])
        for piece in range(n_pieces):
            y_copy(slot, piece, row0 + piece * MOE_BLOCK, word0).start()
        ypend[slot] = n_pieces

    def for_chunks(n_blocks, fn):
        n_full = n_blocks // CHUNK_BLOCKS

        def body(c, carry):
            fn(c * CHUNK_BLOCKS, CHUNK_BLOCKS)
            return carry
        lax.fori_loop(0, n_full, body, 0)
        done = n_full * CHUNK_BLOCKS
        rem = n_blocks - done

        @pl.when(rem >= HALF_CHUNK)
        def _():
            fn(done, HALF_CHUNK)
        tail = jnp.where(rem >= HALF_CHUNK, rem - HALF_CHUNK, rem)
        tail0 = jnp.where(rem >= HALF_CHUNK, done + HALF_CHUNK, done)
        for k in range(1, HALF_CHUNK):
            @pl.when(tail == k)
            def _():
                fn(tail0, k)

    @pl.when(s == 0)
    def _():
        ypend[0] = 0
        ypend[1] = 0
        ycnt[0] = 0
        gcnt[0] = 0

        @pl.when(nb > 0)
        def _():
            for c in gateup_copies(expert, 0, 0):
                c.start()
        gather_upto(0, nb * MOE_BLOCK)

    @pl.when(nb > 0)
    def _():
        gather_wait()
        row_base = rb0_ref[s] * MOE_BLOCK

        def gateup_tile(j, carry):
            slot = lax.rem(j, 2)
            for c in gateup_copies(expert, j, slot):
                c.wait()

            @pl.when(j + 1 < N_FF_TILES)
            def _():
                for c in gateup_copies(expert, j + 1, 1 - slot):
                    c.start()

            @pl.when(j + 1 == N_FF_TILES)
            def _():
                down_copy(expert, 0, 0).start()

            bg = bgu_ref[0, pl.ds(j, 1), :]
            bl = bgu_ref[0, pl.ds(N_FF_TILES + j, 1), :]

            def chunk(b0, k):
                rows = pl.ds(pl.multiple_of(b0 * MOE_BLOCK, MOE_BLOCK), k * MOE_BLOCK)
                x_lo, x_hi = _unpack_bf16_pair(x_res[rows, :])
                gate = _dotw(x_lo, wgl_buf[slot, 0, :half, :]) + _dotw(x_hi, wgl_buf[slot, 0, half:, :]) + bg
                lin = _dotw(x_lo, wgl_buf[slot, 1, :half, :]) + _dotw(x_hi, wgl_buf[slot, 1, half:, :]) + bl
                gate = jnp.minimum(gate, SWIGLU_LIMIT)
                lin = jnp.clip(lin, -SWIGLU_LIMIT, SWIGLU_LIMIT)
                hmid[j, rows, :] = (gate * jax.nn.sigmoid(SWIGLU_ALPHA * gate) * (lin + 1.0)).astype(BF16)
            for_chunks(nb, chunk)
            return carry
        lax.fori_loop(0, N_FF_TILES, gateup_tile, 0)

        def down_tile(j, carry):
            slot = lax.rem(j, 2)
            down_copy(expert, j, slot).wait()

            @pl.when(j + 1 < N_DOWN_TILES)
            def _():
                down_copy(expert, j + 1, 1 - slot).start()

            @pl.when(jnp.logical_and(j + 1 == N_DOWN_TILES, next_active))
            def _():
                for c in gateup_copies(sbe_ref[nxt], 0, 0):
                    c.start()

            bd = bd_ref[0, pl.ds(j, 1), :]
            word0 = j * Y_WORDS

            def chunk(b0, k):
                yslot = lax.rem(ycnt[0], 2)
                ycnt[0] = ycnt[0] + 1
                y_drain(yslot)
                gather_some(nxt, k * (MOE_BLOCK // N_DOWN_TILES))
                rows = pl.ds(pl.multiple_of(b0 * MOE_BLOCK, MOE_BLOCK), k * MOE_BLOCK)
                acc = bd + _dotw(hmid[0, rows, :], wd_buf[slot, 0:FF_TILE, :])
                for jj in range(1, N_FF_TILES):
                    acc = acc + _dotw(hmid[jj, rows, :], wd_buf[slot, jj * FF_TILE:(jj + 1) * FF_TILE, :])
                y_emit(acc, k, yslot, row_base + b0 * MOE_BLOCK, word0)
            for_chunks(nb, chunk)
            return carry
        lax.fori_loop(0, N_DOWN_TILES, down_tile, 0)
        gather_upto(nxt, jnp.where(s + 1 < ns, nbk_ref[nxt] * MOE_BLOCK, 0))

    @pl.when(s == ns - 1)
    def _():
        gather_wait()
        y_drain(0)
        y_drain(1)
        ystage[0, pl.ds(0, MOE_BLOCK), :] = jnp.zeros((MOE_BLOCK, Y_WORDS), jnp.uint32)

        def zero_copy(b, j):
            return y_copy(0, 0, b * MOE_BLOCK, j * Y_WORDS)

        def start_body(b, carry):
            for j in range(N_DOWN_TILES):
                zero_copy(b, j).start()
            return carry

        def wait_body(b, carry):
            for j in range(N_DOWN_TILES):
                zero_copy(0, 0).wait()
            return carry
        lax.fori_loop(used_ref[0], n_blocks_total, start_body, 0)
        lax.fori_loop(used_ref[0], n_blocks_total, wait_body, 0)


def _experts(sb_e, sb_rb0, sb_nb, n_used, row_tok, n2p, w_gu, b_gu, w_down, b_down):
    R = row_tok.shape[0]
    S = sb_e.shape[0]
    sb_rows = SB_BLOCKS * MOE_BLOCK

    grid_spec = pltpu.PrefetchScalarGridSpec(
        num_scalar_prefetch=5,
        grid=(S,),
        in_specs=[
            pl.BlockSpec(memory_space=pl.ANY),
            pl.BlockSpec(memory_space=pl.ANY),
            pl.BlockSpec(memory_space=pl.ANY),
            pl.BlockSpec((1, 2 * N_FF_TILES, FF_TILE), lambda s, e, r0, nbk, u, tok: (e[s], 0, 0)),
            pl.BlockSpec((1, N_DOWN_TILES, DOWN_TILE), lambda s, e, r0, nbk, u, tok: (e[s], 0, 0)),
        ],
        out_specs=pl.BlockSpec(memory_space=pl.ANY),
        scratch_shapes=[
            pltpu.VMEM((sb_rows, D_MODEL // 2), jnp.uint32),
            pltpu.VMEM((N_FF_TILES, sb_rows, FF_TILE), BF16),
            pltpu.VMEM((2, 2, D_MODEL, FF_TILE), F32),
            pltpu.VMEM((2, D_FF, DOWN_TILE), F32),
            pltpu.VMEM((2, CHUNK_BLOCKS * MOE_BLOCK, Y_WORDS), jnp.uint32),
            pltpu.SemaphoreType.DMA((1,)),
            pltpu.SemaphoreType.DMA((2,)),
            pltpu.SemaphoreType.DMA((2,)),
            pltpu.SemaphoreType.DMA((2,)),
            pltpu.SMEM((2,), jnp.int32),
            pltpu.SMEM((1,), jnp.int32),
            pltpu.SMEM((1,), jnp.int32),
        ],
    )
    return pl.pallas_call(
        _experts_kernel,
        grid_spec=grid_spec,
        out_shape=jax.ShapeDtypeStruct((R, D_MODEL // 2), jnp.uint32),
        compiler_params=_params(("arbitrary",)),
        name="moe_experts",
    )(sb_e, sb_rb0, sb_nb, n_used, row_tok, n2p, w_gu, w_down,
      b_gu.reshape(N_EXPERTS, 2 * N_FF_TILES, FF_TILE), b_down.reshape(N_EXPERTS, N_DOWN_TILES, DOWN_TILE))


def _combine_kernel(tb, dest_ref, piece_ref, y_hbm, h_ref, p_ref, g_ref, out_hbm, buf, ostage, sem, osem, opend):
    i = pl.program_id(0)
    nb = pl.num_programs(0)
    rows = tb * TOP_K
    n_pieces = tb // N_META

    def out_copy(slot, piece, dst):
        return pltpu.make_async_copy(ostage.at[slot, pl.ds(piece * N_META, N_META), :],
                                     out_hbm.at[pl.ds(pl.multiple_of(dst, N_META), N_META), :], osem.at[slot])

    def out_drain(slot):
        for piece in range(n_pieces):
            @pl.when(opend[slot] > piece)
            def _():
                out_copy(slot, 0, 0).wait()
        opend[slot] = 0

    @pl.when(i == 0)
    def _():
        opend[0] = 0
        opend[1] = 0
        _row_gather_start(y_hbm, dest_ref, 0, rows, buf, 0, sem)

    def step(slot):
        @pl.when(i + 1 < nb)
        def _():
            _row_gather_start(y_hbm, dest_ref, (i + 1) * rows, rows, buf, 1 - slot, sem)

        _row_gather_wait(y_hbm, rows, buf, slot, sem)
        acc = h_ref[...]
        p = p_ref[...]
        for kk in range(TOP_K):
            w = buf[slot, pl.ds(kk * tb, tb), :]
            lo = lax.bitcast_convert_type(w << 16, F32)
            hi = lax.bitcast_convert_type(w & jnp.uint32(0xFFFF0000), F32)
            pieces = []
            for j in range(N_DOWN_TILES):
                pieces += [lo[:, j * Y_WORDS:(j + 1) * Y_WORDS], hi[:, j * Y_WORDS:(j + 1) * Y_WORDS]]
            acc = acc + p[:, kk:kk + 1] * jnp.concatenate(pieces, axis=1)
        out_drain(slot)
        ostage[slot] = _rms(acc, g_ref[...])
        started = 0
        for piece in range(n_pieces):
            dst = piece_ref[i * n_pieces + piece]

            @pl.when(dst >= 0)
            def _():
                out_copy(slot, piece, dst).start()
            started = started + (dst >= 0).astype(jnp.int32)
        opend[slot] = started

    for slot in range(2):
        pl.when(lax.rem(i, 2) == slot)(functools.partial(step, slot))

    @pl.when(i == nb - 1)
    def _():
        out_drain(0)
        out_drain(1)


def _combine(dest_km, pieces, y, h, probs, g, tb, n_out):
    T = h.shape[0]
    grid_spec = pltpu.PrefetchScalarGridSpec(
        num_scalar_prefetch=2,
        grid=(T // tb,),
        in_specs=[
            pl.BlockSpec(memory_space=pl.ANY),
            pl.BlockSpec((tb, D_MODEL), lambda i, d, q: (i, 0)),
            pl.BlockSpec((tb, N_EXPERTS), lambda i, d, q: (i, 0)),
            pl.BlockSpec((1, D_MODEL), lambda i, d, q: (0, 0)),
        ],
        out_specs=pl.BlockSpec(memory_space=pl.ANY),
        scratch_shapes=[
            pltpu.VMEM((2, tb * TOP_K, D_MODEL // 2), jnp.uint32),
            pltpu.VMEM((2, tb, D_MODEL), F32),
            pltpu.SemaphoreType.DMA((2,)),
            pltpu.SemaphoreType.DMA((2,)),
            pltpu.SMEM((2,), jnp.int32),
        ],
    )
    return pl.pallas_call(
        functools.partial(_combine_kernel, tb),
        grid_spec=grid_spec,
        out_shape=jax.ShapeDtypeStruct((n_out, D_MODEL), F32),
        compiler_params=_params(("arbitrary",)),
        name="moe_combine_norm",
    )(dest_km, pieces, y, h, probs, g)


def _routing(top_e, tb):
    T = top_e.shape[0]
    A = T * TOP_K
    n_blocks = -(-(A + N_EXPERTS * (MOE_BLOCK - 1)) // MOE_BLOCK)
    R = n_blocks * MOE_BLOCK
    n_sb = N_EXPERTS + n_blocks // SB_BLOCKS
    i32 = jnp.int32
    e_flat = top_e.reshape(A)
    onehot = (e_flat[:, None] == jnp.arange(N_EXPERTS, dtype=i32)[None, :]).astype(i32)
    csum = jnp.cumsum(onehot, axis=0)
    rank = jnp.sum(csum * onehot, axis=1) - 1
    counts = csum[-1]
    blocks = (counts + MOE_BLOCK - 1) // MOE_BLOCK
    blk_end = jnp.cumsum(blocks)
    blk_start = blk_end - blocks
    dest = (blk_start * MOE_BLOCK)[e_flat] + rank
    row_tok = jnp.zeros((R,), i32).at[dest].set(jnp.arange(A, dtype=i32) // TOP_K)
    dest_km = dest.reshape(T // tb, tb, TOP_K).transpose(0, 2, 1).reshape(A).astype(i32)

    sbs = (blocks + SB_BLOCKS - 1) // SB_BLOCKS
    sb_end = jnp.cumsum(sbs)
    sb_start = sb_end - sbs
    s_idx = jnp.arange(n_sb, dtype=i32)
    active = s_idx < sb_end[-1]
    e_s = jnp.minimum(jnp.sum((sb_end[None, :] <= s_idx[:, None]).astype(i32), axis=1), N_EXPERTS - 1)
    k_s = s_idx - sb_start[e_s]
    sb_rb0 = jnp.where(active, blk_start[e_s] + k_s * SB_BLOCKS, 0)
    sb_nb = jnp.where(active, jnp.minimum(blocks[e_s] - k_s * SB_BLOCKS, SB_BLOCKS), 0)
    sb_e = jnp.where(active, e_s, jnp.max(jnp.where(active, e_s, 0)))
    n_used = blk_end[-1:].astype(i32)
    return row_tok, dest_km, sb_e.astype(i32), sb_rb0.astype(i32), sb_nb.astype(i32), n_used


def kernel(x, meta_tokens, norm1_g, w_in, b_igate, b_fgate, w_pool_mix, pool_scale, w_out, norm2_g, w_router,
           b_router, w_gu, b_gu, w_down, b_down, norm_f_g):
    B, S, D = x.shape
    L = N_META + S
    T = B * L
    H = MLSTM_HEADS
    BM_IN = 688
    BM_OUT = 688
    BM_NORM1 = 688
    BM_NORM = 192
    TB = 192
    assert T % BM_IN == 0 and T % BM_OUT == 0 and T % BM_NORM1 == 0 and T % BM_NORM == 0 and T % TB == 0
    assert w_in.shape[0] == 1
    assert L % N_META == 0 and TB % N_META == 0

    pieces = jnp.asarray(_piece_table(B, S))
    x2 = x.reshape(B * S, D)
    meta = meta_tokens.astype(x.dtype)

    l = 0
    w_in_t = w_in.reshape(D, PROJ_COLS + 2 * H).T
    bg = jnp.pad(jnp.concatenate([b_igate[l], b_fgate[l]]), (0, LANES - 2 * H)).reshape(1, LANES)
    n1, gates = _norm1(pieces, x2, meta, norm1_g[l].reshape(1, D), w_in_t, bg, BM_NORM1)

    proj3 = _inproj(n1, w_in_t, PROJ_COLS, BM_IN, 1024).reshape(B, L, PROJ_COLS)
    pool_out = _pool(proj3, w_pool_mix[l].astype(BF16), pool_scale[l].reshape(1, POOL_WIDTH))
    mlstm_out = _mlstm(proj3, gates.reshape(B, L, LANES))

    h1 = _outproj(pieces, pool_out.reshape(T, POOL_WIDTH), mlstm_out.reshape(T, MLSTM_WIDTH),
                  w_out.reshape(D, D), x2, meta, BM_OUT, 512)

    n2p, top_e, probs = _router(h1, norm2_g[l].reshape(1, D), w_router.reshape(D, N_EXPERTS).T,
                                b_router.reshape(1, N_EXPERTS), BM_NORM)

    row_tok, dest_km, sb_e, sb_rb0, sb_nb, n_used = _routing(top_e[:, :TOP_K], TB)
    y = _experts(sb_e, sb_rb0, sb_nb, n_used, row_tok, n2p,
                 w_gu.reshape(N_EXPERTS, D, 2 * D_FF), b_gu.reshape(N_EXPERTS, 1, 2 * D_FF),
                 w_down.reshape(N_EXPERTS, D_FF, D), b_down.reshape(N_EXPERTS, 1, D))
    out = _combine(dest_km, pieces, y, h1, probs, norm_f_g.reshape(1, D), TB, B * S)
    return out.reshape(B, S, D)
```

```python
import functools

import jax
import jax.numpy as jnp
import numpy as np
from jax import lax
from jax.experimental import pallas as pl
from jax.experimental.pallas import tpu as pltpu

D_MODEL = 4096
N_META = 16
POOL_WINDOWS = (2, 4, 8, 16)
POOL_WIDTH = D_MODEL // 4
POOL_GROUP = POOL_WIDTH // len(POOL_WINDOWS)
MLSTM_WIDTH = D_MODEL - POOL_WIDTH
MLSTM_HEADS = 6
MLSTM_V_DIM = MLSTM_WIDTH // MLSTM_HEADS
MLSTM_QK_DIM = MLSTM_V_DIM // 2
GATE_SOFTCAP = 15.0
N_EXPERTS = 32
TOP_K = 4
D_FF = D_MODEL // 2
SWIGLU_ALPHA = 1.702
SWIGLU_LIMIT = 7.0
MOE_BLOCK = 128
EPS = 1e-6

LANES = 128
SEQ_CHUNK = 256
PROJ_COLS = POOL_WIDTH + 2 * MLSTM_HEADS * MLSTM_QK_DIM + 2 * MLSTM_WIDTH
VMEM_LIMIT = 56 * 1024 * 1024
SB_BLOCKS = 10
CHUNK_SIZES = (6, 3, 2, 1)
CHUNK_BLOCKS = max(CHUNK_SIZES)
FF_TILE = 256
DOWN_TILE = 1024
N_FF_TILES = D_FF // FF_TILE
N_DOWN_TILES = D_MODEL // DOWN_TILE
Y_WORDS = DOWN_TILE // 2

F32 = jnp.float32
BF16 = jnp.bfloat16


def _params(sem, vmem=VMEM_LIMIT):
    return pltpu.CompilerParams(dimension_semantics=sem, vmem_limit_bytes=vmem)


def _split3(a):
    hi = a.astype(BF16)
    r1 = a - hi.astype(F32)
    mid = r1.astype(BF16)
    lo = (r1 - mid.astype(F32)).astype(BF16)
    return hi, mid, lo


def _dot(a, b):
    return jnp.dot(a, b, preferred_element_type=F32)


def _dotw(a, w):
    return lax.dot_general(a, w, (((1,), (0,)), ((), ())), preferred_element_type=F32)


def _dot_nt(a, wt):
    return lax.dot_general(a, wt, (((1,), (1,)), ((), ())), preferred_element_type=F32)


def _dot3_nt(a, wt):
    a_hi = a.astype(BF16)
    a_lo = (a - a_hi.astype(F32)).astype(BF16)
    w_hi = wt.astype(BF16)
    w_lo = (wt - w_hi.astype(F32)).astype(BF16)
    n = wt.shape[0]
    if n == LANES:
        both = _dot_nt(a_hi, jnp.concatenate([w_hi, w_lo], axis=0))
        return both[:, :n] + (both[:, n:] + _dot_nt(a_lo, w_hi))
    return _dot_nt(a_hi, w_hi) + (_dot_nt(a_hi, w_lo) + _dot_nt(a_lo, w_hi))


def _rms(x, g):
    return x * lax.rsqrt(jnp.mean(x * x, axis=-1, keepdims=True) + EPS) * g


GATE_ROWS = 16


def _piece_table(batch, seq):
    L = N_META + seq
    first = np.arange(batch * L // N_META) * N_META
    pos = first % L
    return np.where(pos >= N_META, (first // L) * seq + pos - N_META, -1).astype(np.int32)


def _token_block_start(blk, seq, x_hbm, meta_hbm, cols, dst, sem):
    bm = dst.shape[0]
    per_seq = (N_META + seq) // bm
    b = blk // per_seq
    r = blk - b * per_seq

    @pl.when(r == 0)
    def _():
        pltpu.make_async_copy(meta_hbm.at[:, cols], dst.at[pl.ds(0, N_META), :], sem).start()
        pltpu.make_async_copy(x_hbm.at[pl.ds(pl.multiple_of(b * seq, N_META), bm - N_META), cols],
                              dst.at[pl.ds(N_META, bm - N_META), :], sem).start()

    @pl.when(r > 0)
    def _():
        pltpu.make_async_copy(x_hbm.at[pl.ds(pl.multiple_of(b * seq + r * bm - N_META, N_META), bm), cols],
                              dst, sem).start()


def _token_block_wait(x_hbm, cols, dst, sem):
    pltpu.make_async_copy(x_hbm.at[pl.ds(0, dst.shape[0]), cols], dst, sem).wait()


def _norm1_kernel(seq, x_hbm, meta_hbm, g_ref, wg_ref, bg_ref, n_ref, gate_ref, hbuf, sem):
    i = pl.program_id(0)
    all_cols = pl.ds(0, D_MODEL)

    @pl.when(i == 0)
    def _():
        _token_block_start(0, seq, x_hbm, meta_hbm, all_cols, hbuf.at[0], sem.at[0])

    def step(slot):
        @pl.when(i + 1 < pl.num_programs(0))
        def _():
            _token_block_start(i + 1, seq, x_hbm, meta_hbm, all_cols, hbuf.at[1 - slot], sem.at[1 - slot])
        _token_block_wait(x_hbm, all_cols, hbuf.at[slot], sem.at[slot])
        _norm1_body(hbuf[slot], g_ref, wg_ref, bg_ref, n_ref, gate_ref)

    for slot in range(2):
        pl.when(lax.rem(i, 2) == slot)(functools.partial(step, slot))


def _norm1_body(h, g_ref, wg_ref, bg_ref, n_ref, gate_ref):
    y = _rms(h, g_ref[...])
    n_ref[...] = y.astype(BF16)
    sub = lax.broadcasted_iota(jnp.int32, (GATE_ROWS, D_MODEL), 0)
    wg = jnp.where(sub < 2 * MLSTM_HEADS, wg_ref[...], 0.0)
    wg = jnp.concatenate([wg, jnp.zeros((LANES - GATE_ROWS, D_MODEL), F32)], axis=0)
    gate_ref[...] = _dot3_nt(y, wg) + bg_ref[...]


def _norm1(x2, meta, batch, g, w_in_t, bg, bm):
    seq = x2.shape[0] // batch
    T = batch * (N_META + seq)
    assert (N_META + seq) % bm == 0
    return pl.pallas_call(
        functools.partial(_norm1_kernel, seq),
        grid=(T // bm,),
        in_specs=[
            pl.BlockSpec(memory_space=pl.ANY),
            pl.BlockSpec(memory_space=pl.ANY),
            pl.BlockSpec((1, D_MODEL), lambda i: (0, 0)),
            pl.BlockSpec((GATE_ROWS, D_MODEL), lambda i: (PROJ_COLS // GATE_ROWS, 0)),
            pl.BlockSpec((1, LANES), lambda i: (0, 0)),
        ],
        out_specs=[
            pl.BlockSpec((bm, D_MODEL), lambda i: (i, 0)),
            pl.BlockSpec((bm, LANES), lambda i: (i, 0)),
        ],
        out_shape=[
            jax.ShapeDtypeStruct((T, D_MODEL), BF16),
            jax.ShapeDtypeStruct((T, LANES), F32),
        ],
        scratch_shapes=[pltpu.VMEM((2, bm, D_MODEL), F32), pltpu.SemaphoreType.DMA((2,))],
        compiler_params=_params(("arbitrary",)),
        name="norm1_gates",
    )(x2, meta, g, w_in_t, bg)


def _inproj_kernel(x_ref, wt_ref, o_ref):
    o_ref[...] = _dot_nt(x_ref[...], wt_ref[...]).astype(o_ref.dtype)


def _inproj(x, wt, n_cols, bm, bn):
    M, K = x.shape
    return pl.pallas_call(
        _inproj_kernel,
        grid=(n_cols // bn, M // bm),
        in_specs=[
            pl.BlockSpec((bm, K), lambda j, i: (i, 0)),
            pl.BlockSpec((bn, K), lambda j, i: (j, 0)),
        ],
        out_specs=pl.BlockSpec((bm, bn), lambda j, i: (i, j)),
        out_shape=jax.ShapeDtypeStruct((M, n_cols), F32),
        compiler_params=_params(("parallel", "parallel")),
        name="in_proj",
    )(x, wt)


def _outproj_kernel(seq, p_ref, m_ref, w_ref, x_hbm, meta_hbm, o_ref, rbuf, sem):
    j = pl.program_id(0)
    i = pl.program_id(1)
    ni = pl.num_programs(1)
    bn = rbuf.shape[2]
    t = j * ni + i

    def fetch(step, slot):
        cols = pl.ds(pl.multiple_of((step // ni) * bn, bn), bn)
        _token_block_start(lax.rem(step, ni), seq, x_hbm, meta_hbm, cols, rbuf.at[slot], sem.at[slot])

    @pl.when(t == 0)
    def _():
        fetch(0, 0)

    def step(slot):
        @pl.when(t + 1 < pl.num_programs(0) * ni)
        def _():
            fetch(t + 1, 1 - slot)
        acc = _dotw(p_ref[...], w_ref[:POOL_WIDTH, :]) + _dotw(m_ref[...], w_ref[POOL_WIDTH:, :])
        _token_block_wait(x_hbm, pl.ds(0, bn), rbuf.at[slot], sem.at[slot])
        o_ref[...] = rbuf[slot] + acc

    for slot in range(2):
        pl.when(lax.rem(t, 2) == slot)(functools.partial(step, slot))


def _outproj(p, m, w, x2, meta, batch, bm, bn):
    M = p.shape[0]
    K, N = w.shape
    seq = x2.shape[0] // batch
    assert (N_META + seq) % bm == 0
    return pl.pallas_call(
        functools.partial(_outproj_kernel, seq),
        grid=(N // bn, M // bm),
        in_specs=[
            pl.BlockSpec((bm, POOL_WIDTH), lambda j, i: (i, 0)),
            pl.BlockSpec((bm, MLSTM_WIDTH), lambda j, i: (i, 0)),
            pl.BlockSpec((K, bn), lambda j, i: (0, j)),
            pl.BlockSpec(memory_space=pl.ANY),
            pl.BlockSpec(memory_space=pl.ANY),
        ],
        out_specs=pl.BlockSpec((bm, bn), lambda j, i: (i, j)),
        out_shape=jax.ShapeDtypeStruct((M, N), F32),
        scratch_shapes=[pltpu.VMEM((2, bm, bn), F32), pltpu.SemaphoreType.DMA((2,))],
        compiler_params=_params(("arbitrary", "arbitrary")),
        name="out_proj",
    )(p, m, w, x2, meta)


def _pool_kernel(u_ref, w_ref, s_ref, o_ref, carry_ref):
    c = pl.program_id(1)

    @pl.when(c == 0)
    def _():
        carry_ref[...] = jnp.zeros_like(carry_ref)

    u = u_ref[0]
    ext = jnp.concatenate([carry_ref[...], u], axis=0)
    carry_ref[...] = u[SEQ_CHUNK - 16:, :]
    pos = c * SEQ_CHUNK + lax.broadcasted_iota(jnp.int32, (SEQ_CHUNK, 1), 0)
    for g, win in enumerate(POOL_WINDOWS):
        cols = slice(g * POOL_GROUP, (g + 1) * POOL_GROUP)
        s = ext[:, cols]
        span = 1
        while span < win:
            s = s + pltpu.roll(s, span, axis=0)
            span *= 2
        cnt = jnp.minimum(pos + 1, win).astype(F32)
        d = s[16:, :] / cnt - u[:, cols]
        y = _dot(d.astype(BF16), w_ref[g]) * s_ref[:, cols]
        o_ref[0, :, cols] = y.astype(o_ref.dtype)


def _pool(proj3, w_mix, scale):
    B, L, _ = proj3.shape
    nc = pl.cdiv(L, SEQ_CHUNK)
    return pl.pallas_call(
        _pool_kernel,
        grid=(B, nc),
        in_specs=[
            pl.BlockSpec((1, SEQ_CHUNK, POOL_WIDTH), lambda b, c: (b, c, 0)),
            pl.BlockSpec((len(POOL_WINDOWS), POOL_GROUP, POOL_GROUP), lambda b, c: (0, 0, 0)),
            pl.BlockSpec((1, POOL_WIDTH), lambda b, c: (0, 0)),
        ],
        out_specs=pl.BlockSpec((1, SEQ_CHUNK, POOL_WIDTH), lambda b, c: (b, c, 0)),
        out_shape=jax.ShapeDtypeStruct((B, L, POOL_WIDTH), BF16),
        scratch_shapes=[pltpu.VMEM((16, POOL_WIDTH), F32)],
        compiler_params=_params(("parallel", "arbitrary")),
        name="pool_mixer",
    )(proj3, w_mix, scale)


def _soft_cap(a):
    return GATE_SOFTCAP * jnp.tanh(a / GATE_SOFTCAP)


def _log_sigmoid(a):
    return jnp.minimum(a, 0.0) - jnp.log1p(jnp.exp(-jnp.abs(a)))


HEADS_PER_STEP = 2


def _mlstm_kernel(seq_len, q_ref, k_ref, v_ref, o_ref, gate_ref, out_ref, c_ref, n_ref, m_ref):
    c = pl.program_id(2)
    Lc = SEQ_CHUNK
    dk, dv = MLSTM_QK_DIM, MLSTM_V_DIM

    @pl.when(c == 0)
    def _():
        c_ref[...] = jnp.zeros_like(c_ref)
        n_ref[...] = jnp.zeros_like(n_ref)
        m_ref[...] = jnp.zeros_like(m_ref)

    row = lax.broadcasted_iota(jnp.int32, (Lc, 1), 0)
    col = lax.broadcasted_iota(jnp.int32, (1, Lc), 1)
    ok_col = (c * Lc + row) < seq_len
    causal = col <= row
    tri = causal.astype(BF16)
    tri_t = (row <= col).astype(BF16)
    lane = lax.broadcasted_iota(jnp.int32, (1, LANES), 1)
    capped = _soft_cap(gate_ref[0])
    gc = jnp.where(ok_col, jnp.where(lane < MLSTM_HEADS, capped, _log_sigmoid(capped)), 0.0)
    gt = gc.T
    for i in range(HEADS_PER_STEP):
        _mlstm_head(pl.program_id(1) * HEADS_PER_STEP + i, ok_col, causal, tri, tri_t, gc, gt,
                    q_ref[0, :, i * dk:(i + 1) * dk], k_ref[0, :, i * dk:(i + 1) * dk],
                    v_ref[0, :, i * dv:(i + 1) * dv], o_ref[0, :, i * dv:(i + 1) * dv],
                    out_ref.at[0, :, i * dv:(i + 1) * dv], c_ref.at[i], n_ref.at[i], m_ref.at[i])


def _mlstm_head(h, ok_col, causal, tri, tri_t, gc, gt, q_in, k_in, v_in, o_in, out_ref, c_ref, n_ref, m_ref):
    Lc = SEQ_CHUNK
    q = jnp.where(ok_col, q_in, 0.0) * (MLSTM_QK_DIM ** -0.5)
    k = jnp.where(ok_col, k_in, 0.0)
    v = jnp.where(ok_col, v_in, 0.0)

    lane = lax.broadcasted_iota(jnp.int32, (1, LANES), 1)
    sub = lax.broadcasted_iota(jnp.int32, (LANES, 1), 0)
    i_c = jnp.sum(jnp.where(lane == h, gc, 0.0), axis=1, keepdims=True)
    f_c = jnp.sum(jnp.where(lane == h + MLSTM_HEADS, gc, 0.0), axis=1, keepdims=True)
    i_r = jnp.sum(jnp.where(sub == h, gt, 0.0), axis=0, keepdims=True)
    f_r = jnp.sum(jnp.where(sub == h + MLSTM_HEADS, gt, 0.0), axis=0, keepdims=True)

    fb_c = jnp.broadcast_to(f_c, (Lc, LANES))
    b_c = sum(_dot(tri, p) for p in _split3(fb_c))[:, 0:1]
    fb_r = jnp.broadcast_to(f_r, (8, Lc))
    b_r8 = sum(_dot(p, tri_t) for p in _split3(fb_r))
    b_r = b_r8[0:1, :]
    g_tot = b_r8[0:1, Lc - 1:Lc]

    m_prev = m_ref[...]
    dlog = jnp.where(causal, b_c - b_r + i_r, -jnp.inf)
    inter_log = b_c + m_prev
    m_out = jnp.maximum(inter_log, jnp.max(dlog, axis=1, keepdims=True))
    wts = jnp.exp(dlog - m_out)
    inter_w = jnp.exp(inter_log - m_out)

    qb = q.astype(BF16)
    kb = k.astype(BF16)
    vb = v.astype(BF16)
    s = lax.dot_general(qb, kb, (((1,), (1,)), ((), ())), preferred_element_type=F32) * wts
    num = _dot(s.astype(BF16), vb) + inter_w * _dot(qb, c_ref[...].astype(BF16))
    den = jnp.sum(s, axis=1, keepdims=True) + inter_w * jnp.sum(q * n_ref[...], axis=1, keepdims=True)
    hh = num / jnp.maximum(jnp.abs(den), jnp.exp(-m_out))
    out_ref[...] = (jax.nn.sigmoid(o_in) * hh).astype(out_ref.dtype)

    a_c = g_tot - b_c + i_c
    m_new = jnp.maximum(g_tot + m_prev, jnp.max(a_c, axis=0, keepdims=True))
    wk = k * jnp.exp(a_c - m_new)
    decay = jnp.exp(g_tot + m_prev - m_new)
    c_ref[...] = decay * c_ref[...] + lax.dot_general(
        wk.astype(BF16), vb, (((0,), (0,)), ((), ())), preferred_element_type=F32)
    n_ref[...] = decay * n_ref[...] + jnp.sum(wk, axis=0, keepdims=True)
    m_ref[...] = m_new


def _mlstm(proj3, gates3):
    B, L, _ = proj3.shape
    nc = pl.cdiv(L, SEQ_CHUNK)
    G = HEADS_PER_STEP
    dk, dv, H = G * MLSTM_QK_DIM, G * MLSTM_V_DIM, MLSTM_HEADS // G
    q0 = POOL_WIDTH // dk
    k0 = q0 + H
    v0 = (POOL_WIDTH + 2 * MLSTM_HEADS * MLSTM_QK_DIM) // dv
    o0 = v0 + H
    return pl.pallas_call(
        functools.partial(_mlstm_kernel, L),
        grid=(B, H, nc),
        in_specs=[
            pl.BlockSpec((1, SEQ_CHUNK, dk), lambda b, h, c: (b, c, q0 + h)),
            pl.BlockSpec((1, SEQ_CHUNK, dk), lambda b, h, c: (b, c, k0 + h)),
            pl.BlockSpec((1, SEQ_CHUNK, dv), lambda b, h, c: (b, c, v0 + h)),
            pl.BlockSpec((1, SEQ_CHUNK, dv), lambda b, h, c: (b, c, o0 + h)),
            pl.BlockSpec((1, SEQ_CHUNK, LANES), lambda b, h, c: (b, c, 0)),
        ],
        out_specs=pl.BlockSpec((1, SEQ_CHUNK, dv), lambda b, h, c: (b, c, h)),
        out_shape=jax.ShapeDtypeStruct((B, L, MLSTM_WIDTH), BF16),
        scratch_shapes=[
            pltpu.VMEM((G, MLSTM_QK_DIM, MLSTM_V_DIM), F32),
            pltpu.VMEM((G, 1, MLSTM_QK_DIM), F32),
            pltpu.VMEM((G, 1, 1), F32),
        ],
        compiler_params=_params(("parallel", "parallel", "arbitrary")),
        name="mlstm",
    )(proj3, proj3, proj3, proj3, gates3)


def _pack_bf16_pair(lo, hi):
    lo_b = lax.bitcast_convert_type(lo.astype(BF16).astype(F32), jnp.uint32)
    hi_b = lax.bitcast_convert_type(hi.astype(BF16).astype(F32), jnp.uint32)
    return (lo_b >> 16) | hi_b


def _unpack_bf16_pair(w):
    lo = lax.bitcast_convert_type(w << 16, F32).astype(BF16)
    hi = lax.bitcast_convert_type(w & jnp.uint32(0xFFFF0000), F32).astype(BF16)
    return lo, hi


def _router_kernel(h_ref, g_ref, wr_ref, br_ref, n_ref, e_ref, p_ref):
    y = _rms(h_ref[...], g_ref[...])
    half = D_MODEL // 2
    n_ref[...] = _pack_bf16_pair(y[:, :half], y[:, half:])
    logits = _dot3_nt(y, wr_ref[...]) + br_ref[...]
    lane = lax.broadcasted_iota(jnp.int32, logits.shape, 1).astype(F32)
    l = jnp.where(lane < N_EXPERTS, logits, -jnp.inf)
    vals, idxs = [], []
    for _ in range(TOP_K):
        m = jnp.max(l, axis=1, keepdims=True)
        idx = jnp.min(jnp.where(l == m, lane, float(LANES)), axis=1, keepdims=True)
        vals.append(m)
        idxs.append(idx)
        l = jnp.where(lane == idx, -jnp.inf, l)
    ex = [jnp.exp(v - vals[0]) for v in vals]
    tot = ex[0] + ex[1] + ex[2] + ex[3]
    e_out = jnp.zeros(logits.shape, F32)
    p_out = jnp.zeros(logits.shape, F32)
    for kk in range(TOP_K):
        e_out = jnp.where(lane == kk, idxs[kk], e_out)
        p_out = jnp.where(lane == kk, ex[kk] / tot, p_out)
    e_ref[...] = e_out.astype(jnp.int32)
    p_ref[...] = p_out


def _router(h, g, wr, br, bm):
    T = h.shape[0]
    return pl.pallas_call(
        _router_kernel,
        grid=(T // bm,),
        in_specs=[
            pl.BlockSpec((bm, D_MODEL), lambda i: (i, 0)),
            pl.BlockSpec((1, D_MODEL), lambda i: (0, 0)),
            pl.BlockSpec((N_EXPERTS, D_MODEL), lambda i: (0, 0)),
            pl.BlockSpec((1, N_EXPERTS), lambda i: (0, 0)),
        ],
        out_specs=[
            pl.BlockSpec((bm, D_MODEL // 2), lambda i: (i, 0)),
            pl.BlockSpec((bm, N_EXPERTS), lambda i: (i, 0)),
            pl.BlockSpec((bm, N_EXPERTS), lambda i: (i, 0)),
        ],
        out_shape=[
            jax.ShapeDtypeStruct((T, D_MODEL // 2), jnp.uint32),
            jax.ShapeDtypeStruct((T, N_EXPERTS), jnp.int32),
            jax.ShapeDtypeStruct((T, N_EXPERTS), F32),
        ],
        compiler_params=_params(("parallel",)),
        name="norm2_router",
    )(h, g, wr, br)


def _row_gather_start(src_hbm, idx_ref, base, n_rows, buf, slot, sem):
    for r in range(n_rows):
        tok = idx_ref[base + r]
        pltpu.make_async_copy(src_hbm.at[pl.ds(tok, 1), :], buf.at[slot, pl.ds(r, 1), :], sem.at[slot]).start()


def _row_gather_wait(src_hbm, n_rows, buf, slot, sem):
    pltpu.make_async_copy(src_hbm.at[pl.ds(0, n_rows), :], buf.at[slot], sem.at[slot]).wait()


def _experts_kernel(sbe_ref, rb0_ref, nbk_ref, used_ref, tok_ref,
                    x_hbm, wgu_hbm, wd_hbm, bgu_ref, bd_ref,
                    y_hbm,
                    x_res, hmid, wgl_buf, wd_buf, ystage, gsem, wsem, dsem, ysem, ypend, ycnt, gcnt):
    s = pl.program_id(0)
    ns = pl.num_programs(0)
    nb = nbk_ref[s]
    expert = sbe_ref[s]
    nxt = jnp.minimum(s + 1, ns - 1)
    next_active = jnp.logical_and(s + 1 < ns, nbk_ref[nxt] > 0)
    half = D_MODEL // 2
    n_blocks_total = y_hbm.shape[0] // MOE_BLOCK

    def gateup_copies(e, j, slot):
        gate_cols = pl.ds(pl.multiple_of(j * FF_TILE, FF_TILE), FF_TILE)
        lin_cols = pl.ds(pl.multiple_of(D_FF + j * FF_TILE, FF_TILE), FF_TILE)
        return (pltpu.make_async_copy(wgu_hbm.at[e, :, gate_cols], wgl_buf.at[slot, 0], wsem.at[slot]),
                pltpu.make_async_copy(wgu_hbm.at[e, :, lin_cols], wgl_buf.at[slot, 1], wsem.at[slot]))

    def down_copy(e, j, slot):
        cols = pl.ds(pl.multiple_of(j * DOWN_TILE, DOWN_TILE), DOWN_TILE)
        return pltpu.make_async_copy(wd_hbm.at[e, :, cols], wd_buf.at[slot], dsem.at[slot])

    def x_block_copy(tok, r, n):
        return pltpu.make_async_copy(x_hbm.at[pl.ds(tok, n), :], x_res.at[pl.ds(r, n), :], gsem.at[0])

    def gather_row(sb_base, r):
        tok = tok_ref[jnp.minimum(sb_base + r, tok_ref.shape[0] - 1)]
        x_block_copy(tok, r, 1).start()

    def gather_upto(sb, n_rows):
        base = rb0_ref[sb] * MOE_BLOCK

        def body(r, carry):
            gather_row(base, r)
            return carry
        lax.fori_loop(gcnt[0], n_rows, body, 0)
        gcnt[0] = jnp.maximum(gcnt[0], n_rows)

    def gather_some(sb, n):
        base = rb0_ref[sb] * MOE_BLOCK
        cur = gcnt[0]
        for i in range(n):
            gather_row(base, cur + i)
        gcnt[0] = cur + n

    def gather_wait():
        n = gcnt[0]
        n_full = n // MOE_BLOCK

        def block_body(r, carry):
            x_block_copy(0, 0, MOE_BLOCK).wait()
            return carry

        def row_body(r, carry):
            x_block_copy(0, 0, 1).wait()
            return carry
        lax.fori_loop(0, n_full, block_body, 0)
        lax.fori_loop(n_full * MOE_BLOCK, n, row_body, 0)
        gcnt[0] = 0

    def y_copy(slot, piece, row0, word0):
        return pltpu.make_async_copy(
            ystage.at[slot, pl.ds(piece * MOE_BLOCK, MOE_BLOCK), :],
            y_hbm.at[pl.ds(pl.multiple_of(row0, MOE_BLOCK), MOE_BLOCK), pl.ds(pl.multiple_of(word0, Y_WORDS), Y_WORDS)],
            ysem.at[slot])

    def y_drain(slot):
        for piece in range(CHUNK_BLOCKS):
            @pl.when(ypend[slot] > piece)
            def _():
                y_copy(slot, 0, 0, 0).wait()
        ypend[slot] = 0

    def y_emit(val, n_pieces, slot, row0, word0):
        ystage[slot, pl.ds(0, n_pieces * MOE_BLOCK), :] = _pack_bf16_pair(val[:, :Y_WORDS], val[:, Y_WORDS:])
        for piece in range(n_pieces):
            y_copy(slot, piece, row0 + piece * MOE_BLOCK, word0).start()
        ypend[slot] = n_pieces

    def for_chunks(n_blocks, fn):
        done = 0
        for pos, k in enumerate(CHUNK_SIZES):
            if pos == 0:
                n_big = n_blocks // k

                def body(c, carry, k=k):
                    fn(c * k, k)
                    return carry
                lax.fori_loop(0, n_big, body, 0)
                done = n_big * k
            else:
                fits = n_blocks - done >= k

                @pl.when(fits)
                def _(k=k, done=done):
                    fn(done, k)
                done = jnp.where(fits, done + k, done)

    @pl.when(s == 0)
    def _():
        ypend[0] = 0
        ypend[1] = 0
        ycnt[0] = 0
        gcnt[0] = 0

        @pl.when(nb > 0)
        def _():
            for c in gateup_copies(expert, 0, 0):
                c.start()
        gather_upto(0, nb * MOE_BLOCK)

    @pl.when(nb > 0)
    def _():
        gather_wait()
        row_base = rb0_ref[s] * MOE_BLOCK

        def gateup_tile(j, carry):
            slot = lax.rem(j, 2)
            for c in gateup_copies(expert, j, slot):
                c.wait()

            @pl.when(j + 1 < N_FF_TILES)
            def _():
                for c in gateup_copies(expert, j + 1, 1 - slot):
                    c.start()

            @pl.when(j + 1 == N_FF_TILES)
            def _():
                down_copy(expert, 0, 0).start()

            bg = bgu_ref[0, pl.ds(j, 1), :]
            bl = bgu_ref[0, pl.ds(N_FF_TILES + j, 1), :]

            def chunk(b0, k):
                rows = pl.ds(pl.multiple_of(b0 * MOE_BLOCK, MOE_BLOCK), k * MOE_BLOCK)
                x_lo, x_hi = _unpack_bf16_pair(x_res[rows, :])
                gate = _dotw(x_lo, wgl_buf[slot, 0, :half, :]) + _dotw(x_hi, wgl_buf[slot, 0, half:, :]) + bg
                lin = _dotw(x_lo, wgl_buf[slot, 1, :half, :]) + _dotw(x_hi, wgl_buf[slot, 1, half:, :]) + bl
                gate = jnp.minimum(gate, SWIGLU_LIMIT)
                lin = jnp.clip(lin, -SWIGLU_LIMIT, SWIGLU_LIMIT)
                hmid[j, rows, :] = (gate * jax.nn.sigmoid(SWIGLU_ALPHA * gate) * (lin + 1.0)).astype(BF16)
            for_chunks(nb, chunk)
            return carry
        lax.fori_loop(0, N_FF_TILES, gateup_tile, 0)

        def down_tile(j, carry):
            slot = lax.rem(j, 2)
            down_copy(expert, j, slot).wait()

            @pl.when(j + 1 < N_DOWN_TILES)
            def _():
                down_copy(expert, j + 1, 1 - slot).start()

            @pl.when(jnp.logical_and(j + 1 == N_DOWN_TILES, next_active))
            def _():
                for c in gateup_copies(sbe_ref[nxt], 0, 0):
                    c.start()

            bd = bd_ref[0, pl.ds(j, 1), :]
            word0 = j * Y_WORDS

            def chunk(b0, k):
                yslot = lax.rem(ycnt[0], 2)
                ycnt[0] = ycnt[0] + 1
                y_drain(yslot)
                gather_some(nxt, k * (MOE_BLOCK // N_DOWN_TILES))
                rows = pl.ds(pl.multiple_of(b0 * MOE_BLOCK, MOE_BLOCK), k * MOE_BLOCK)
                acc = bd + _dotw(hmid[0, rows, :], wd_buf[slot, 0:FF_TILE, :])
                for jj in range(1, N_FF_TILES):
                    acc = acc + _dotw(hmid[jj, rows, :], wd_buf[slot, jj * FF_TILE:(jj + 1) * FF_TILE, :])
                y_emit(acc, k, yslot, row_base + b0 * MOE_BLOCK, word0)
            for_chunks(nb, chunk)
            return carry
        lax.fori_loop(0, N_DOWN_TILES, down_tile, 0)
        gather_upto(nxt, jnp.where(s + 1 < ns, nbk_ref[nxt] * MOE_BLOCK, 0))

    @pl.when(s == ns - 1)
    def _():
        gather_wait()
        y_drain(0)
        y_drain(1)
        ystage[0, pl.ds(0, MOE_BLOCK), :] = jnp.zeros((MOE_BLOCK, Y_WORDS), jnp.uint32)

        def zero_copy(b, j):
            return y_copy(0, 0, b * MOE_BLOCK, j * Y_WORDS)

        def start_body(b, carry):
            for j in range(N_DOWN_TILES):
                zero_copy(b, j).start()
            return carry

        def wait_body(b, carry):
            for j in range(N_DOWN_TILES):
                zero_copy(0, 0).wait()
            return carry
        lax.fori_loop(used_ref[0], n_blocks_total, start_body, 0)
        lax.fori_loop(used_ref[0], n_blocks_total, wait_body, 0)


def _experts(sb_e, sb_rb0, sb_nb, n_used, row_tok, n2p, w_gu, b_gu, w_down, b_down):
    R = row_tok.shape[0]
    S = sb_e.shape[0]
    sb_rows = SB_BLOCKS * MOE_BLOCK

    grid_spec = pltpu.PrefetchScalarGridSpec(
        num_scalar_prefetch=5,
        grid=(S,),
        in_specs=[
            pl.BlockSpec(memory_space=pl.ANY),
            pl.BlockSpec(memory_space=pl.ANY),
            pl.BlockSpec(memory_space=pl.ANY),
            pl.BlockSpec((1, 2 * N_FF_TILES, FF_TILE), lambda s, e, r0, nbk, u, tok: (e[s], 0, 0)),
            pl.BlockSpec((1, N_DOWN_TILES, DOWN_TILE), lambda s, e, r0, nbk, u, tok: (e[s], 0, 0)),
        ],
        out_specs=pl.BlockSpec(memory_space=pl.ANY),
        scratch_shapes=[
            pltpu.VMEM((sb_rows, D_MODEL // 2), jnp.uint32),
            pltpu.VMEM((N_FF_TILES, sb_rows, FF_TILE), BF16),
            pltpu.VMEM((2, 2, D_MODEL, FF_TILE), F32),
            pltpu.VMEM((2, D_FF, DOWN_TILE), F32),
            pltpu.VMEM((2, CHUNK_BLOCKS * MOE_BLOCK, Y_WORDS), jnp.uint32),
            pltpu.SemaphoreType.DMA((1,)),
            pltpu.SemaphoreType.DMA((2,)),
            pltpu.SemaphoreType.DMA((2,)),
            pltpu.SemaphoreType.DMA((2,)),
            pltpu.SMEM((2,), jnp.int32),
            pltpu.SMEM((1,), jnp.int32),
            pltpu.SMEM((1,), jnp.int32),
        ],
    )
    return pl.pallas_call(
        _experts_kernel,
        grid_spec=grid_spec,
        out_shape=jax.ShapeDtypeStruct((R, D_MODEL // 2), jnp.uint32),
        compiler_params=_params(("arbitrary",)),
        name="moe_experts",
    )(sb_e, sb_rb0, sb_nb, n_used, row_tok, n2p, w_gu, w_down,
      b_gu.reshape(N_EXPERTS, 2 * N_FF_TILES, FF_TILE), b_down.reshape(N_EXPERTS, N_DOWN_TILES, DOWN_TILE))


def _combine_kernel(tb, dest_ref, piece_ref, y_hbm, h_ref, p_ref, g_ref, out_hbm, buf, ostage, sem, osem, opend):
    i = pl.program_id(0)
    nb = pl.num_programs(0)
    rows = tb * TOP_K
    n_pieces = tb // N_META

    def out_copy(slot, piece, dst):
        return pltpu.make_async_copy(ostage.at[slot, pl.ds(piece * N_META, N_META), :],
                                     out_hbm.at[pl.ds(pl.multiple_of(dst, N_META), N_META), :], osem.at[slot])

    def out_drain(slot):
        for piece in range(n_pieces):
            @pl.when(opend[slot] > piece)
            def _():
                out_copy(slot, 0, 0).wait()
        opend[slot] = 0

    @pl.when(i == 0)
    def _():
        opend[0] = 0
        opend[1] = 0
        _row_gather_start(y_hbm, dest_ref, 0, rows, buf, 0, sem)

    def step(slot):
        @pl.when(i + 1 < nb)
        def _():
            _row_gather_start(y_hbm, dest_ref, (i + 1) * rows, rows, buf, 1 - slot, sem)

        _row_gather_wait(y_hbm, rows, buf, slot, sem)
        acc = h_ref[...]
        p = p_ref[...]
        for kk in range(TOP_K):
            w = buf[slot, pl.ds(kk * tb, tb), :]
            lo = lax.bitcast_convert_type(w << 16, F32)
            hi = lax.bitcast_convert_type(w & jnp.uint32(0xFFFF0000), F32)
            pieces = []
            for j in range(N_DOWN_TILES):
                pieces += [lo[:, j * Y_WORDS:(j + 1) * Y_WORDS], hi[:, j * Y_WORDS:(j + 1) * Y_WORDS]]
            acc = acc + p[:, kk:kk + 1] * jnp.concatenate(pieces, axis=1)
        out_drain(slot)
        ostage[slot] = _rms(acc, g_ref[...])
        started = 0
        for piece in range(n_pieces):
            dst = piece_ref[i * n_pieces + piece]

            @pl.when(dst >= 0)
            def _():
                out_copy(slot, piece, dst).start()
            started = started + (dst >= 0).astype(jnp.int32)
        opend[slot] = started

    for slot in range(2):
        pl.when(lax.rem(i, 2) == slot)(functools.partial(step, slot))

    @pl.when(i == nb - 1)
    def _():
        out_drain(0)
        out_drain(1)


def _combine(dest_km, pieces, y, h, probs, g, tb, n_out):
    T = h.shape[0]
    grid_spec = pltpu.PrefetchScalarGridSpec(
        num_scalar_prefetch=2,
        grid=(T // tb,),
        in_specs=[
            pl.BlockSpec(memory_space=pl.ANY),
            pl.BlockSpec((tb, D_MODEL), lambda i, d, q: (i, 0)),
            pl.BlockSpec((tb, N_EXPERTS), lambda i, d, q: (i, 0)),
            pl.BlockSpec((1, D_MODEL), lambda i, d, q: (0, 0)),
        ],
        out_specs=pl.BlockSpec(memory_space=pl.ANY),
        scratch_shapes=[
            pltpu.VMEM((2, tb * TOP_K, D_MODEL // 2), jnp.uint32),
            pltpu.VMEM((2, tb, D_MODEL), F32),
            pltpu.SemaphoreType.DMA((2,)),
            pltpu.SemaphoreType.DMA((2,)),
            pltpu.SMEM((2,), jnp.int32),
        ],
    )
    return pl.pallas_call(
        functools.partial(_combine_kernel, tb),
        grid_spec=grid_spec,
        out_shape=jax.ShapeDtypeStruct((n_out, D_MODEL), F32),
        compiler_params=_params(("arbitrary",)),
        name="moe_combine_norm",
    )(dest_km, pieces, y, h, probs, g)


def _routing(top_e, tb):
    T = top_e.shape[0]
    A = T * TOP_K
    n_blocks = -(-(A + N_EXPERTS * (MOE_BLOCK - 1)) // MOE_BLOCK)
    R = n_blocks * MOE_BLOCK
    n_sb = N_EXPERTS + n_blocks // SB_BLOCKS
    i32 = jnp.int32
    e_flat = top_e.reshape(A)
    onehot = (e_flat[:, None] == jnp.arange(N_EXPERTS, dtype=i32)[None, :]).astype(i32)
    csum = jnp.cumsum(onehot, axis=0)
    rank = jnp.sum(csum * onehot, axis=1) - 1
    counts = csum[-1]
    blocks = (counts + MOE_BLOCK - 1) // MOE_BLOCK
    blk_end = jnp.cumsum(blocks)
    blk_start = blk_end - blocks
    dest = (blk_start * MOE_BLOCK)[e_flat] + rank
    row_tok = jnp.zeros((R,), i32).at[dest].set(jnp.arange(A, dtype=i32) // TOP_K)
    dest_km = dest.reshape(T // tb, tb, TOP_K).transpose(0, 2, 1).reshape(A).astype(i32)

    sbs = (blocks + SB_BLOCKS - 1) // SB_BLOCKS
    sb_end = jnp.cumsum(sbs)
    sb_start = sb_end - sbs
    s_idx = jnp.arange(n_sb, dtype=i32)
    active = s_idx < sb_end[-1]
    e_s = jnp.minimum(jnp.sum((sb_end[None, :] <= s_idx[:, None]).astype(i32), axis=1), N_EXPERTS - 1)
    k_s = s_idx - sb_start[e_s]
    sb_rb0 = jnp.where(active, blk_start[e_s] + k_s * SB_BLOCKS, 0)
    sb_nb = jnp.where(active, jnp.minimum(blocks[e_s] - k_s * SB_BLOCKS, SB_BLOCKS), 0)
    sb_e = jnp.where(active, e_s, jnp.max(jnp.where(active, e_s, 0)))
    n_used = blk_end[-1:].astype(i32)
    return row_tok, dest_km, sb_e.astype(i32), sb_rb0.astype(i32), sb_nb.astype(i32), n_used


def kernel(x, meta_tokens, norm1_g, w_in, b_igate, b_fgate, w_pool_mix, pool_scale, w_out, norm2_g, w_router,
           b_router, w_gu, b_gu, w_down, b_down, norm_f_g):
    B, S, D = x.shape
    L = N_META + S
    T = B * L
    H = MLSTM_HEADS
    BM_IN = 688
    BM_OUT = 688
    BM_NORM1 = 688
    BM_NORM = 192
    TB = 192
    assert T % BM_IN == 0 and T % BM_OUT == 0 and T % BM_NORM1 == 0 and T % BM_NORM == 0 and T % TB == 0
    assert w_in.shape[0] == 1
    assert L % N_META == 0 and TB % N_META == 0

    pieces = jnp.asarray(_piece_table(B, S))
    x2 = x.reshape(B * S, D)
    meta = meta_tokens.astype(x.dtype)

    l = 0
    w_in_t = w_in.reshape(D, PROJ_COLS + 2 * H).T
    bg = jnp.pad(jnp.concatenate([b_igate[l], b_fgate[l]]), (0, LANES - 2 * H)).reshape(1, LANES)
    n1, gates = _norm1(x2, meta, B, norm1_g[l].reshape(1, D), w_in_t, bg, BM_NORM1)

    proj3 = _inproj(n1, w_in_t, PROJ_COLS, BM_IN, 1024).reshape(B, L, PROJ_COLS)
    pool_out = _pool(proj3, w_pool_mix[l].astype(BF16), pool_scale[l].reshape(1, POOL_WIDTH))
    mlstm_out = _mlstm(proj3, gates.reshape(B, L, LANES))

    h1 = _outproj(pool_out.reshape(T, POOL_WIDTH), mlstm_out.reshape(T, MLSTM_WIDTH),
                  w_out.reshape(D, D), x2, meta, B, BM_OUT, 512)

    n2p, top_e, probs = _router(h1, norm2_g[l].reshape(1, D), w_router.reshape(D, N_EXPERTS).T,
                                b_router.reshape(1, N_EXPERTS), BM_NORM)

    row_tok, dest_km, sb_e, sb_rb0, sb_nb, n_used = _routing(top_e[:, :TOP_K], TB)
    y = _experts(sb_e, sb_rb0, sb_nb, n_used, row_tok, n2p,
                 w_gu.reshape(N_EXPERTS, D, 2 * D_FF), b_gu.reshape(N_EXPERTS, 1, 2 * D_FF),
                 w_down.reshape(N_EXPERTS, D_FF, D), b_down.reshape(N_EXPERTS, 1, D))
    out = _combine(dest_km, pieces, y, h1, probs, norm_f_g.reshape(1, D), TB, B * S)
    return out.reshape(B, S, D)
```

```python
import functools

import jax
import jax.numpy as jnp
import numpy as np
from jax import lax
from jax.experimental import pallas as pl
from jax.experimental.pallas import tpu as pltpu

D_MODEL = 4096
N_META = 16
POOL_WINDOWS = (2, 4, 8, 16)
POOL_WIDTH = D_MODEL // 4
POOL_GROUP = POOL_WIDTH // len(POOL_WINDOWS)
MLSTM_WIDTH = D_MODEL - POOL_WIDTH
MLSTM_HEADS = 6
MLSTM_V_DIM = MLSTM_WIDTH // MLSTM_HEADS
MLSTM_QK_DIM = MLSTM_V_DIM // 2
GATE_SOFTCAP = 15.0
N_EXPERTS = 32
TOP_K = 4
D_FF = D_MODEL // 2
SWIGLU_ALPHA = 1.702
SWIGLU_LIMIT = 7.0
MOE_BLOCK = 128
EPS = 1e-6

LANES = 128
SEQ_CHUNK = 256
PROJ_COLS = POOL_WIDTH + 2 * MLSTM_HEADS * MLSTM_QK_DIM + 2 * MLSTM_WIDTH
VMEM_LIMIT = 56 * 1024 * 1024
SB_BLOCKS = 10
CHUNK_SIZES = (6, 3, 2, 1)
CHUNK_BLOCKS = max(CHUNK_SIZES)
FF_TILE = 256
DOWN_TILE = 1024
N_FF_TILES = D_FF // FF_TILE
N_DOWN_TILES = D_MODEL // DOWN_TILE
Y_WORDS = DOWN_TILE // 2

F32 = jnp.float32
BF16 = jnp.bfloat16


def _params(sem, vmem=VMEM_LIMIT):
    return pltpu.CompilerParams(dimension_semantics=sem, vmem_limit_bytes=vmem)


def _split3(a):
    hi = a.astype(BF16)
    r1 = a - hi.astype(F32)
    mid = r1.astype(BF16)
    lo = (r1 - mid.astype(F32)).astype(BF16)
    return hi, mid, lo


def _dot(a, b):
    return jnp.dot(a, b, preferred_element_type=F32)


def _dotw(a, w):
    return lax.dot_general(a, w, (((1,), (0,)), ((), ())), preferred_element_type=F32)


def _dot_nt(a, wt):
    return lax.dot_general(a, wt, (((1,), (1,)), ((), ())), preferred_element_type=F32)


def _dot3_nt(a, wt):
    a_hi = a.astype(BF16)
    a_lo = (a - a_hi.astype(F32)).astype(BF16)
    w_hi = wt.astype(BF16)
    w_lo = (wt - w_hi.astype(F32)).astype(BF16)
    n = wt.shape[0]
    if n == LANES:
        both = _dot_nt(a_hi, jnp.concatenate([w_hi, w_lo], axis=0))
        return both[:, :n] + (both[:, n:] + _dot_nt(a_lo, w_hi))
    return _dot_nt(a_hi, w_hi) + (_dot_nt(a_hi, w_lo) + _dot_nt(a_lo, w_hi))


def _rms(x, g):
    return x * lax.rsqrt(jnp.mean(x * x, axis=-1, keepdims=True) + EPS) * g


GATE_ROWS = 16


def _piece_table(batch, seq):
    L = N_META + seq
    first = np.arange(batch * L // N_META) * N_META
    pos = first % L
    return np.where(pos >= N_META, (first // L) * seq + pos - N_META, -1).astype(np.int32)


def _token_block_start(blk, seq, x_hbm, meta_hbm, cols, dst, sem):
    bm = dst.shape[0]
    per_seq = (N_META + seq) // bm
    b = blk // per_seq
    r = blk - b * per_seq

    @pl.when(r == 0)
    def _():
        pltpu.make_async_copy(meta_hbm.at[:, cols], dst.at[pl.ds(0, N_META), :], sem).start()
        pltpu.make_async_copy(x_hbm.at[pl.ds(pl.multiple_of(b * seq, N_META), bm - N_META), cols],
                              dst.at[pl.ds(N_META, bm - N_META), :], sem).start()

    @pl.when(r > 0)
    def _():
        pltpu.make_async_copy(x_hbm.at[pl.ds(pl.multiple_of(b * seq + r * bm - N_META, N_META), bm), cols],
                              dst, sem).start()


def _token_block_wait(x_hbm, cols, dst, sem):
    pltpu.make_async_copy(x_hbm.at[pl.ds(0, dst.shape[0]), cols], dst, sem).wait()


def _norm1_kernel(seq, x_hbm, meta_hbm, g_ref, wg_ref, bg_ref, n_ref, gate_ref, hbuf, sem):
    i = pl.program_id(0)
    all_cols = pl.ds(0, D_MODEL)

    @pl.when(i == 0)
    def _():
        _token_block_start(0, seq, x_hbm, meta_hbm, all_cols, hbuf.at[0], sem.at[0])

    def step(slot):
        @pl.when(i + 1 < pl.num_programs(0))
        def _():
            _token_block_start(i + 1, seq, x_hbm, meta_hbm, all_cols, hbuf.at[1 - slot], sem.at[1 - slot])
        _token_block_wait(x_hbm, all_cols, hbuf.at[slot], sem.at[slot])
        _norm1_body(hbuf[slot], g_ref, wg_ref, bg_ref, n_ref, gate_ref)

    for slot in range(2):
        pl.when(lax.rem(i, 2) == slot)(functools.partial(step, slot))


def _norm1_body(h, g_ref, wg_ref, bg_ref, n_ref, gate_ref):
    y = _rms(h, g_ref[...])
    n_ref[...] = y.astype(BF16)
    sub = lax.broadcasted_iota(jnp.int32, (GATE_ROWS, D_MODEL), 0)
    wg = jnp.where(sub < 2 * MLSTM_HEADS, wg_ref[...], 0.0)
    wg = jnp.concatenate([wg, jnp.zeros((LANES - GATE_ROWS, D_MODEL), F32)], axis=0)
    gate_ref[...] = _dot3_nt(y, wg) + bg_ref[...]


def _norm1(x2, meta, batch, g, w_in_t, bg, bm):
    seq = x2.shape[0] // batch
    T = batch * (N_META + seq)
    assert (N_META + seq) % bm == 0
    return pl.pallas_call(
        functools.partial(_norm1_kernel, seq),
        grid=(T // bm,),
        in_specs=[
            pl.BlockSpec(memory_space=pl.ANY),
            pl.BlockSpec(memory_space=pl.ANY),
            pl.BlockSpec((1, D_MODEL), lambda i: (0, 0)),
            pl.BlockSpec((GATE_ROWS, D_MODEL), lambda i: (PROJ_COLS // GATE_ROWS, 0)),
            pl.BlockSpec((1, LANES), lambda i: (0, 0)),
        ],
        out_specs=[
            pl.BlockSpec((bm, D_MODEL), lambda i: (i, 0)),
            pl.BlockSpec((bm, LANES), lambda i: (i, 0)),
        ],
        out_shape=[
            jax.ShapeDtypeStruct((T, D_MODEL), BF16),
            jax.ShapeDtypeStruct((T, LANES), F32),
        ],
        scratch_shapes=[pltpu.VMEM((2, bm, D_MODEL), F32), pltpu.SemaphoreType.DMA((2,))],
        compiler_params=_params(("arbitrary",)),
        name="norm1_gates",
    )(x2, meta, g, w_in_t, bg)


def _inproj_kernel(x_ref, wt_ref, o_ref):
    o_ref[...] = _dot_nt(x_ref[...], wt_ref[...]).astype(o_ref.dtype)


def _inproj(x, wt, n_cols, bm, bn):
    M, K = x.shape
    return pl.pallas_call(
        _inproj_kernel,
        grid=(n_cols // bn, M // bm),
        in_specs=[
            pl.BlockSpec((bm, K), lambda j, i: (i, 0)),
            pl.BlockSpec((bn, K), lambda j, i: (j, 0)),
        ],
        out_specs=pl.BlockSpec((bm, bn), lambda j, i: (i, j)),
        out_shape=jax.ShapeDtypeStruct((M, n_cols), F32),
        compiler_params=_params(("parallel", "parallel")),
        name="in_proj",
    )(x, wt)


def _outproj_kernel(seq, p_ref, m_ref, w_ref, x_hbm, meta_hbm, o_ref, rbuf, sem):
    j = pl.program_id(0)
    i = pl.program_id(1)
    ni = pl.num_programs(1)
    bn = rbuf.shape[2]
    t = j * ni + i

    def fetch(step, slot):
        cols = pl.ds(pl.multiple_of((step // ni) * bn, bn), bn)
        _token_block_start(lax.rem(step, ni), seq, x_hbm, meta_hbm, cols, rbuf.at[slot], sem.at[slot])

    @pl.when(t == 0)
    def _():
        fetch(0, 0)

    def step(slot):
        @pl.when(t + 1 < pl.num_programs(0) * ni)
        def _():
            fetch(t + 1, 1 - slot)
        acc = _dotw(p_ref[...], w_ref[:POOL_WIDTH, :]) + _dotw(m_ref[...], w_ref[POOL_WIDTH:, :])
        _token_block_wait(x_hbm, pl.ds(0, bn), rbuf.at[slot], sem.at[slot])
        o_ref[...] = rbuf[slot] + acc

    for slot in range(2):
        pl.when(lax.rem(t, 2) == slot)(functools.partial(step, slot))


def _outproj(p, m, w, x2, meta, batch, bm, bn):
    M = p.shape[0]
    K, N = w.shape
    seq = x2.shape[0] // batch
    assert (N_META + seq) % bm == 0
    return pl.pallas_call(
        functools.partial(_outproj_kernel, seq),
        grid=(N // bn, M // bm),
        in_specs=[
            pl.BlockSpec((bm, POOL_WIDTH), lambda j, i: (i, 0)),
            pl.BlockSpec((bm, MLSTM_WIDTH), lambda j, i: (i, 0)),
            pl.BlockSpec((K, bn), lambda j, i: (0, j)),
            pl.BlockSpec(memory_space=pl.ANY),
            pl.BlockSpec(memory_space=pl.ANY),
        ],
        out_specs=pl.BlockSpec((bm, bn), lambda j, i: (i, j)),
        out_shape=jax.ShapeDtypeStruct((M, N), F32),
        scratch_shapes=[pltpu.VMEM((2, bm, bn), F32), pltpu.SemaphoreType.DMA((2,))],
        compiler_params=_params(("arbitrary", "arbitrary")),
        name="out_proj",
    )(p, m, w, x2, meta)


def _pool_kernel(u_ref, w_ref, s_ref, o_ref, carry_ref):
    c = pl.program_id(1)

    @pl.when(c == 0)
    def _():
        carry_ref[...] = jnp.zeros_like(carry_ref)

    u = u_ref[0]
    ext = jnp.concatenate([carry_ref[...], u], axis=0)
    carry_ref[...] = u[SEQ_CHUNK - 16:, :]
    pos = c * SEQ_CHUNK + lax.broadcasted_iota(jnp.int32, (SEQ_CHUNK, 1), 0)
    for g, win in enumerate(POOL_WINDOWS):
        cols = slice(g * POOL_GROUP, (g + 1) * POOL_GROUP)
        s = ext[:, cols]
        span = 1
        while span < win:
            s = s + pltpu.roll(s, span, axis=0)
            span *= 2
        cnt = jnp.minimum(pos + 1, win).astype(F32)
        d = s[16:, :] / cnt - u[:, cols]
        y = _dot(d.astype(BF16), w_ref[g]) * s_ref[:, cols]
        o_ref[0, :, cols] = y.astype(o_ref.dtype)


def _pool(proj3, w_mix, scale):
    B, L, _ = proj3.shape
    nc = pl.cdiv(L, SEQ_CHUNK)
    return pl.pallas_call(
        _pool_kernel,
        grid=(B, nc),
        in_specs=[
            pl.BlockSpec((1, SEQ_CHUNK, POOL_WIDTH), lambda b, c: (b, c, 0)),
            pl.BlockSpec((len(POOL_WINDOWS), POOL_GROUP, POOL_GROUP), lambda b, c: (0, 0, 0)),
            pl.BlockSpec((1, POOL_WIDTH), lambda b, c: (0, 0)),
        ],
        out_specs=pl.BlockSpec((1, SEQ_CHUNK, POOL_WIDTH), lambda b, c: (b, c, 0)),
        out_shape=jax.ShapeDtypeStruct((B, L, POOL_WIDTH), BF16),
        scratch_shapes=[pltpu.VMEM((16, POOL_WIDTH), F32)],
        compiler_params=_params(("parallel", "arbitrary")),
        name="pool_mixer",
    )(proj3, w_mix, scale)


def _soft_cap(a):
    return GATE_SOFTCAP * jnp.tanh(a / GATE_SOFTCAP)


def _log_sigmoid(a):
    return jnp.minimum(a, 0.0) - jnp.log1p(jnp.exp(-jnp.abs(a)))


HEADS_PER_STEP = 2


def _mlstm_kernel(seq_len, q_ref, k_ref, v_ref, o_ref, gate_ref, out_ref, c_ref, n_ref, m_ref):
    c = pl.program_id(2)
    Lc = SEQ_CHUNK
    dk, dv = MLSTM_QK_DIM, MLSTM_V_DIM

    @pl.when(c == 0)
    def _():
        c_ref[...] = jnp.zeros_like(c_ref)
        n_ref[...] = jnp.zeros_like(n_ref)
        m_ref[...] = jnp.zeros_like(m_ref)

    row = lax.broadcasted_iota(jnp.int32, (Lc, 1), 0)
    col = lax.broadcasted_iota(jnp.int32, (1, Lc), 1)
    ok_col = (c * Lc + row) < seq_len
    causal = col <= row
    tri = causal.astype(BF16)
    tri_t = (row <= col).astype(BF16)
    lane = lax.broadcasted_iota(jnp.int32, (1, LANES), 1)
    capped = _soft_cap(gate_ref[0])
    gc = jnp.where(ok_col, jnp.where(lane < MLSTM_HEADS, capped, _log_sigmoid(capped)), 0.0)
    gt = gc.T
    for i in range(HEADS_PER_STEP):
        _mlstm_head(pl.program_id(1) * HEADS_PER_STEP + i, ok_col, causal, tri, tri_t, gc, gt,
                    q_ref[0, :, i * dk:(i + 1) * dk], k_ref[0, :, i * dk:(i + 1) * dk],
                    v_ref[0, :, i * dv:(i + 1) * dv], o_ref[0, :, i * dv:(i + 1) * dv],
                    out_ref.at[0, :, i * dv:(i + 1) * dv], c_ref.at[i], n_ref.at[i], m_ref.at[i])


def _mlstm_head(h, ok_col, causal, tri, tri_t, gc, gt, q_in, k_in, v_in, o_in, out_ref, c_ref, n_ref, m_ref):
    Lc = SEQ_CHUNK
    q = jnp.where(ok_col, q_in, 0.0) * (MLSTM_QK_DIM ** -0.5)
    k = jnp.where(ok_col, k_in, 0.0)
    v = jnp.where(ok_col, v_in, 0.0)

    lane = lax.broadcasted_iota(jnp.int32, (1, LANES), 1)
    sub = lax.broadcasted_iota(jnp.int32, (LANES, 1), 0)
    i_c = jnp.sum(jnp.where(lane == h, gc, 0.0), axis=1, keepdims=True)
    f_c = jnp.sum(jnp.where(lane == h + MLSTM_HEADS, gc, 0.0), axis=1, keepdims=True)
    i_r = jnp.sum(jnp.where(sub == h, gt, 0.0), axis=0, keepdims=True)
    f_r = jnp.sum(jnp.where(sub == h + MLSTM_HEADS, gt, 0.0), axis=0, keepdims=True)

    fb_c = jnp.broadcast_to(f_c, (Lc, LANES))
    b_c = sum(_dot(tri, p) for p in _split3(fb_c))[:, 0:1]
    fb_r = jnp.broadcast_to(f_r, (8, Lc))
    b_r8 = sum(_dot(p, tri_t) for p in _split3(fb_r))
    b_r = b_r8[0:1, :]
    g_tot = b_r8[0:1, Lc - 1:Lc]

    m_prev = m_ref[...]
    dlog = jnp.where(causal, b_c - b_r + i_r, -jnp.inf)
    inter_log = b_c + m_prev
    m_out = jnp.maximum(inter_log, jnp.max(dlog, axis=1, keepdims=True))
    wts = jnp.exp(dlog - m_out)
    inter_w = jnp.exp(inter_log - m_out)

    qb = q.astype(BF16)
    kb = k.astype(BF16)
    vb = v.astype(BF16)
    s = lax.dot_general(qb, kb, (((1,), (1,)), ((), ())), preferred_element_type=F32) * wts
    num = _dot(s.astype(BF16), vb) + inter_w * _dot(qb, c_ref[...].astype(BF16))
    den = jnp.sum(s, axis=1, keepdims=True) + inter_w * jnp.sum(q * n_ref[...], axis=1, keepdims=True)
    hh = num / jnp.maximum(jnp.abs(den), jnp.exp(-m_out))
    out_ref[...] = (jax.nn.sigmoid(o_in) * hh).astype(out_ref.dtype)

    a_c = g_tot - b_c + i_c
    m_new = jnp.maximum(g_tot + m_prev, jnp.max(a_c, axis=0, keepdims=True))
    wk = k * jnp.exp(a_c - m_new)
    decay = jnp.exp(g_tot + m_prev - m_new)
    c_ref[...] = decay * c_ref[...] + lax.dot_general(
        wk.astype(BF16), vb, (((0,), (0,)), ((), ())), preferred_element_type=F32)
    n_ref[...] = decay * n_ref[...] + jnp.sum(wk, axis=0, keepdims=True)
    m_ref[...] = m_new


def _mlstm(proj3, gates3):
    B, L, _ = proj3.shape
    nc = pl.cdiv(L, SEQ_CHUNK)
    G = HEADS_PER_STEP
    dk, dv, H = G * MLSTM_QK_DIM, G * MLSTM_V_DIM, MLSTM_HEADS // G
    q0 = POOL_WIDTH // dk
    k0 = q0 + H
    v0 = (POOL_WIDTH + 2 * MLSTM_HEADS * MLSTM_QK_DIM) // dv
    o0 = v0 + H
    return pl.pallas_call(
        functools.partial(_mlstm_kernel, L),
        grid=(B, H, nc),
        in_specs=[
            pl.BlockSpec((1, SEQ_CHUNK, dk), lambda b, h, c: (b, c, q0 + h)),
            pl.BlockSpec((1, SEQ_CHUNK, dk), lambda b, h, c: (b, c, k0 + h)),
            pl.BlockSpec((1, SEQ_CHUNK, dv), lambda b, h, c: (b, c, v0 + h)),
            pl.BlockSpec((1, SEQ_CHUNK, dv), lambda b, h, c: (b, c, o0 + h)),
            pl.BlockSpec((1, SEQ_CHUNK, LANES), lambda b, h, c: (b, c, 0)),
        ],
        out_specs=pl.BlockSpec((1, SEQ_CHUNK, dv), lambda b, h, c: (b, c, h)),
        out_shape=jax.ShapeDtypeStruct((B, L, MLSTM_WIDTH), BF16),
        scratch_shapes=[
            pltpu.VMEM((G, MLSTM_QK_DIM, MLSTM_V_DIM), F32),
            pltpu.VMEM((G, 1, MLSTM_QK_DIM), F32),
            pltpu.VMEM((G, 1, 1), F32),
        ],
        compiler_params=_params(("parallel", "parallel", "arbitrary")),
        name="mlstm",
    )(proj3, proj3, proj3, proj3, gates3)


def _pack_bf16_pair(lo, hi):
    lo_b = lax.bitcast_convert_type(lo.astype(BF16).astype(F32), jnp.uint32)
    hi_b = lax.bitcast_convert_type(hi.astype(BF16).astype(F32), jnp.uint32)
    return (lo_b >> 16) | hi_b


def _unpack_bf16_pair(w):
    lo = lax.bitcast_convert_type(w << 16, F32).astype(BF16)
    hi = lax.bitcast_convert_type(w & jnp.uint32(0xFFFF0000), F32).astype(BF16)
    return lo, hi


def _router_kernel(h_ref, g_ref, wr_ref, br_ref, n_ref, e_ref, p_ref):
    y = _rms(h_ref[...], g_ref[...])
    half = D_MODEL // 2
    n_ref[...] = _pack_bf16_pair(y[:, :half], y[:, half:])
    logits = _dot3_nt(y, wr_ref[...]) + br_ref[...]
    lane = lax.broadcasted_iota(jnp.int32, logits.shape, 1).astype(F32)
    l = jnp.where(lane < N_EXPERTS, logits, -jnp.inf)
    vals, idxs = [], []
    for _ in range(TOP_K):
        m = jnp.max(l, axis=1, keepdims=True)
        idx = jnp.min(jnp.where(l == m, lane, float(LANES)), axis=1, keepdims=True)
        vals.append(m)
        idxs.append(idx)
        l = jnp.where(lane == idx, -jnp.inf, l)
    ex = [jnp.exp(v - vals[0]) for v in vals]
    tot = ex[0] + ex[1] + ex[2] + ex[3]
    e_out = jnp.zeros(logits.shape, F32)
    p_out = jnp.zeros(logits.shape, F32)
    for kk in range(TOP_K):
        e_out = jnp.where(lane == kk, idxs[kk], e_out)
        p_out = jnp.where(lane == kk, ex[kk] / tot, p_out)
    e_ref[...] = e_out.astype(jnp.int32)
    p_ref[...] = p_out


def _router(h, g, wr, br, bm):
    T = h.shape[0]
    return pl.pallas_call(
        _router_kernel,
        grid=(T // bm,),
        in_specs=[
            pl.BlockSpec((bm, D_MODEL), lambda i: (i, 0)),
            pl.BlockSpec((1, D_MODEL), lambda i: (0, 0)),
            pl.BlockSpec((N_EXPERTS, D_MODEL), lambda i: (0, 0)),
            pl.BlockSpec((1, N_EXPERTS), lambda i: (0, 0)),
        ],
        out_specs=[
            pl.BlockSpec((bm, D_MODEL // 2), lambda i: (i, 0)),
            pl.BlockSpec((bm, N_EXPERTS), lambda i: (i, 0)),
            pl.BlockSpec((bm, N_EXPERTS), lambda i: (i, 0)),
        ],
        out_shape=[
            jax.ShapeDtypeStruct((T, D_MODEL // 2), jnp.uint32),
            jax.ShapeDtypeStruct((T, N_EXPERTS), jnp.int32),
            jax.ShapeDtypeStruct((T, N_EXPERTS), F32),
        ],
        compiler_params=_params(("parallel",)),
        name="norm2_router",
    )(h, g, wr, br)


def _row_gather_start(src_hbm, idx_ref, base, n_rows, buf, slot, sem):
    for r in range(n_rows):
        tok = idx_ref[base + r]
        pltpu.make_async_copy(src_hbm.at[pl.ds(tok, 1), :], buf.at[slot, pl.ds(r, 1), :], sem.at[slot]).start()


def _row_gather_wait(src_hbm, n_rows, buf, slot, sem):
    pltpu.make_async_copy(src_hbm.at[pl.ds(0, n_rows), :], buf.at[slot], sem.at[slot]).wait()


def _experts_kernel(sbe_ref, rb0_ref, nbk_ref, used_ref, tok_ref,
                    x_hbm, wgu_hbm, wd_hbm, bgu_ref, bd_ref,
                    y_hbm,
                    x_res, hmid, wgl_buf, wd_buf, ystage, gsem, wsem, dsem, ysem, ypend, ycnt, gcnt):
    s = pl.program_id(0)
    ns = pl.num_programs(0)
    nb = nbk_ref[s]
    expert = sbe_ref[s]
    nxt = jnp.minimum(s + 1, ns - 1)
    next_active = jnp.logical_and(s + 1 < ns, nbk_ref[nxt] > 0)
    half = D_MODEL // 2
    n_blocks_total = y_hbm.shape[0] // MOE_BLOCK

    def gateup_copies(e, j, slot):
        gate_cols = pl.ds(pl.multiple_of(j * FF_TILE, FF_TILE), FF_TILE)
        lin_cols = pl.ds(pl.multiple_of(D_FF + j * FF_TILE, FF_TILE), FF_TILE)
        return (pltpu.make_async_copy(wgu_hbm.at[e, :, gate_cols], wgl_buf.at[slot, 0], wsem.at[slot]),
                pltpu.make_async_copy(wgu_hbm.at[e, :, lin_cols], wgl_buf.at[slot, 1], wsem.at[slot]))

    def down_copy(e, j, slot):
        cols = pl.ds(pl.multiple_of(j * DOWN_TILE, DOWN_TILE), DOWN_TILE)
        return pltpu.make_async_copy(wd_hbm.at[e, :, cols], wd_buf.at[slot], dsem.at[slot])

    def x_block_copy(tok, r, n):
        return pltpu.make_async_copy(x_hbm.at[pl.ds(tok, n), :], x_res.at[pl.ds(r, n), :], gsem.at[0])

    def gather_row(sb_base, r):
        tok = tok_ref[jnp.minimum(sb_base + r, tok_ref.shape[0] - 1)]
        x_block_copy(tok, r, 1).start()

    def gather_upto(sb, n_rows):
        base = rb0_ref[sb] * MOE_BLOCK

        def body(r, carry):
            gather_row(base, r)
            return carry
        lax.fori_loop(gcnt[0], n_rows, body, 0)
        gcnt[0] = jnp.maximum(gcnt[0], n_rows)

    def gather_some(sb, n):
        base = rb0_ref[sb] * MOE_BLOCK
        cur = gcnt[0]
        for i in range(n):
            gather_row(base, cur + i)
        gcnt[0] = cur + n

    def gather_wait():
        n = gcnt[0]
        n_full = n // MOE_BLOCK

        def block_body(r, carry):
            x_block_copy(0, 0, MOE_BLOCK).wait()
            return carry

        def row_body(r, carry):
            x_block_copy(0, 0, 1).wait()
            return carry
        lax.fori_loop(0, n_full, block_body, 0)
        lax.fori_loop(n_full * MOE_BLOCK, n, row_body, 0)
        gcnt[0] = 0

    def y_copy(slot, piece, row0, word0):
        return pltpu.make_async_copy(
            ystage.at[slot, pl.ds(piece * MOE_BLOCK, MOE_BLOCK), :],
            y_hbm.at[pl.ds(pl.multiple_of(row0, MOE_BLOCK), MOE_BLOCK), pl.ds(pl.multiple_of(word0, Y_WORDS), Y_WORDS)],
            ysem.at[slot])

    def y_drain(slot):
        for piece in range(CHUNK_BLOCKS):
            @pl.when(ypend[slot] > piece)
            def _():
                y_copy(slot, 0, 0, 0).wait()
        ypend[slot] = 0

    def y_emit(val, n_pieces, slot, row0, word0):
        ystage[slot, pl.ds(0, n_pieces * MOE_BLOCK), :] = _pack_bf16_pair(val[:, :Y_WORDS], val[:, Y_WORDS:])
        for piece in range(n_pieces):
            y_copy(slot, piece, row0 + piece * MOE_BLOCK, word0).start()
        ypend[slot] = n_pieces

    def for_chunks(n_blocks, fn):
        done = 0
        for pos, k in enumerate(CHUNK_SIZES):
            if pos == 0:
                n_big = n_blocks // k

                def body(c, carry, k=k):
                    fn(c * k, k)
                    return carry
                lax.fori_loop(0, n_big, body, 0)
                done = n_big * k
            else:
                fits = n_blocks - done >= k

                @pl.when(fits)
                def _(k=k, done=done):
                    fn(done, k)
                done = jnp.where(fits, done + k, done)

    @pl.when(s == 0)
    def _():
        ypend[0] = 0
        ypend[1] = 0
        ycnt[0] = 0
        gcnt[0] = 0

        @pl.when(nb > 0)
        def _():
            for c in gateup_copies(expert, 0, 0):
                c.start()
        gather_upto(0, nb * MOE_BLOCK)

    @pl.when(nb > 0)
    def _():
        gather_wait()
        row_base = rb0_ref[s] * MOE_BLOCK

        def gateup_tile(j, carry):
            slot = lax.rem(j, 2)
            for c in gateup_copies(expert, j, slot):
                c.wait()

            @pl.when(j + 1 < N_FF_TILES)
            def _():
                for c in gateup_copies(expert, j + 1, 1 - slot):
                    c.start()

            @pl.when(j + 1 == N_FF_TILES)
            def _():
                down_copy(expert, 0, 0).start()

            bg = bgu_ref[0, pl.ds(j, 1), :]
            bl = bgu_ref[0, pl.ds(N_FF_TILES + j, 1), :]

            def chunk(b0, k):
                rows = pl.ds(pl.multiple_of(b0 * MOE_BLOCK, MOE_BLOCK), k * MOE_BLOCK)
                x_lo, x_hi = _unpack_bf16_pair(x_res[rows, :])
                gate = _dotw(x_lo, wgl_buf[slot, 0, :half, :]) + _dotw(x_hi, wgl_buf[slot, 0, half:, :]) + bg
                lin = _dotw(x_lo, wgl_buf[slot, 1, :half, :]) + _dotw(x_hi, wgl_buf[slot, 1, half:, :]) + bl
                gate = jnp.minimum(gate, SWIGLU_LIMIT)
                lin = jnp.clip(lin, -SWIGLU_LIMIT, SWIGLU_LIMIT)
                hmid[j, rows, :] = (gate * jax.nn.sigmoid(SWIGLU_ALPHA * gate) * (lin + 1.0)).astype(BF16)
            for_chunks(nb, chunk)
            return carry
        lax.fori_loop(0, N_FF_TILES, gateup_tile, 0)

        def down_tile(j, carry):
            slot = lax.rem(j, 2)
            down_copy(expert, j, slot).wait()

            @pl.when(j + 1 < N_DOWN_TILES)
            def _():
                down_copy(expert, j + 1, 1 - slot).start()

            @pl.when(jnp.logical_and(j + 1 == N_DOWN_TILES, next_active))
            def _():
                for c in gateup_copies(sbe_ref[nxt], 0, 0):
                    c.start()

            bd = bd_ref[0, pl.ds(j, 1), :]
            word0 = j * Y_WORDS

            def chunk(b0, k):
                yslot = lax.rem(ycnt[0], 2)
                ycnt[0] = ycnt[0] + 1
                y_drain(yslot)
                gather_some(nxt, k * (MOE_BLOCK // N_DOWN_TILES))
                rows = pl.ds(pl.multiple_of(b0 * MOE_BLOCK, MOE_BLOCK), k * MOE_BLOCK)
                acc = bd + _dotw(hmid[0, rows, :], wd_buf[slot, 0:FF_TILE, :])
                for jj in range(1, N_FF_TILES):
                    acc = acc + _dotw(hmid[jj, rows, :], wd_buf[slot, jj * FF_TILE:(jj + 1) * FF_TILE, :])
                y_emit(acc, k, yslot, row_base + b0 * MOE_BLOCK, word0)
            for_chunks(nb, chunk)
            return carry
        lax.fori_loop(0, N_DOWN_TILES, down_tile, 0)
        gather_upto(nxt, jnp.where(s + 1 < ns, nbk_ref[nxt] * MOE_BLOCK, 0))

    @pl.when(s == ns - 1)
    def _():
        gather_wait()
        y_drain(0)
        y_drain(1)
        ystage[0, pl.ds(0, MOE_BLOCK), :] = jnp.zeros((MOE_BLOCK, Y_WORDS), jnp.uint32)

        def zero_copy(b, j):
            return y_copy(0, 0, b * MOE_BLOCK, j * Y_WORDS)

        def start_body(b, carry):
            for j in range(N_DOWN_TILES):
                zero_copy(b, j).start()
            return carry

        def wait_body(b, carry):
            for j in range(N_DOWN_TILES):
                zero_copy(0, 0).wait()
            return carry
        lax.fori_loop(used_ref[0], n_blocks_total, start_body, 0)
        lax.fori_loop(used_ref[0], n_blocks_total, wait_body, 0)


def _experts(sb_e, sb_rb0, sb_nb, n_used, row_tok, n2p, w_gu, b_gu, w_down, b_down):
    R = row_tok.shape[0]
    S = sb_e.shape[0]
    sb_rows = SB_BLOCKS * MOE_BLOCK

    grid_spec = pltpu.PrefetchScalarGridSpec(
        num_scalar_prefetch=5,
        grid=(S,),
        in_specs=[
            pl.BlockSpec(memory_space=pl.ANY),
            pl.BlockSpec(memory_space=pl.ANY),
            pl.BlockSpec(memory_space=pl.ANY),
            pl.BlockSpec((1, 2 * N_FF_TILES, FF_TILE), lambda s, e, r0, nbk, u, tok: (e[s], 0, 0)),
            pl.BlockSpec((1, N_DOWN_TILES, DOWN_TILE), lambda s, e, r0, nbk, u, tok: (e[s], 0, 0)),
        ],
        out_specs=pl.BlockSpec(memory_space=pl.ANY),
        scratch_shapes=[
            pltpu.VMEM((sb_rows, D_MODEL // 2), jnp.uint32),
            pltpu.VMEM((N_FF_TILES, sb_rows, FF_TILE), BF16),
            pltpu.VMEM((2, 2, D_MODEL, FF_TILE), F32),
            pltpu.VMEM((2, D_FF, DOWN_TILE), F32),
            pltpu.VMEM((2, CHUNK_BLOCKS * MOE_BLOCK, Y_WORDS), jnp.uint32),
            pltpu.SemaphoreType.DMA((1,)),
            pltpu.SemaphoreType.DMA((2,)),
            pltpu.SemaphoreType.DMA((2,)),
            pltpu.SemaphoreType.DMA((2,)),
            pltpu.SMEM((2,), jnp.int32),
            pltpu.SMEM((1,), jnp.int32),
            pltpu.SMEM((1,), jnp.int32),
        ],
    )
    return pl.pallas_call(
        _experts_kernel,
        grid_spec=grid_spec,
        out_shape=jax.ShapeDtypeStruct((R, D_MODEL // 2), jnp.uint32),
        compiler_params=_params(("arbitrary",)),
        name="moe_experts",
    )(sb_e, sb_rb0, sb_nb, n_used, row_tok, n2p, w_gu, w_down,
      b_gu.reshape(N_EXPERTS, 2 * N_FF_TILES, FF_TILE), b_down.reshape(N_EXPERTS, N_DOWN_TILES, DOWN_TILE))


def _combine_kernel(tb, dest_ref, piece_ref, y_hbm, h_ref, p_ref, g_ref, out_hbm, buf, ostage, sem, osem, opend):
    i = pl.program_id(0)
    nb = pl.num_programs(0)
    rows = tb * TOP_K
    n_pieces = tb // N_META

    def out_copy(slot, piece, dst):
        return pltpu.make_async_copy(ostage.at[slot, pl.ds(piece * N_META, N_META), :],
                                     out_hbm.at[pl.ds(pl.multiple_of(dst, N_META), N_META), :], osem.at[slot])

    def out_drain(slot):
        for piece in range(n_pieces):
            @pl.when(opend[slot] > piece)
            def _():
                out_copy(slot, 0, 0).wait()
        opend[slot] = 0

    @pl.when(i == 0)
    def _():
        opend[0] = 0
        opend[1] = 0
        _row_gather_start(y_hbm, dest_ref, 0, rows, buf, 0, sem)

    def step(slot):
        @pl.when(i + 1 < nb)
        def _():
            _row_gather_start(y_hbm, dest_ref, (i + 1) * rows, rows, buf, 1 - slot, sem)

        _row_gather_wait(y_hbm, rows, buf, slot, sem)
        acc = h_ref[...]
        p = p_ref[...]
        for kk in range(TOP_K):
            w = buf[slot, pl.ds(kk * tb, tb), :]
            lo = lax.bitcast_convert_type(w << 16, F32)
            hi = lax.bitcast_convert_type(w & jnp.uint32(0xFFFF0000), F32)
            pieces = []
            for j in range(N_DOWN_TILES):
                pieces += [lo[:, j * Y_WORDS:(j + 1) * Y_WORDS], hi[:, j * Y_WORDS:(j + 1) * Y_WORDS]]
            acc = acc + p[:, kk:kk + 1] * jnp.concatenate(pieces, axis=1)
        out_drain(slot)
        ostage[slot] = _rms(acc, g_ref[...])
        started = 0
        for piece in range(n_pieces):
            dst = piece_ref[i * n_pieces + piece]

            @pl.when(dst >= 0)
            def _():
                out_copy(slot, piece, dst).start()
            started = started + (dst >= 0).astype(jnp.int32)
        opend[slot] = started

    for slot in range(2):
        pl.when(lax.rem(i, 2) == slot)(functools.partial(step, slot))

    @pl.when(i == nb - 1)
    def _():
        out_drain(0)
        out_drain(1)


def _combine(dest_km, pieces, y, h, probs, g, tb, n_out):
    T = h.shape[0]
    grid_spec = pltpu.PrefetchScalarGridSpec(
        num_scalar_prefetch=2,
        grid=(T // tb,),
        in_specs=[
            pl.BlockSpec(memory_space=pl.ANY),
            pl.BlockSpec((tb, D_MODEL), lambda i, d, q: (i, 0)),
            pl.BlockSpec((tb, N_EXPERTS), lambda i, d, q: (i, 0)),
            pl.BlockSpec((1, D_MODEL), lambda i, d, q: (0, 0)),
        ],
        out_specs=pl.BlockSpec(memory_space=pl.ANY),
        scratch_shapes=[
            pltpu.VMEM((2, tb * TOP_K, D_MODEL // 2), jnp.uint32),
            pltpu.VMEM((2, tb, D_MODEL), F32),
            pltpu.SemaphoreType.DMA((2,)),
            pltpu.SemaphoreType.DMA((2,)),
            pltpu.SMEM((2,), jnp.int32),
        ],
    )
    return pl.pallas_call(
        functools.partial(_combine_kernel, tb),
        grid_spec=grid_spec,
        out_shape=jax.ShapeDtypeStruct((n_out, D_MODEL), F32),
        compiler_params=_params(("arbitrary",)),
        name="moe_combine_norm",
    )(dest_km, pieces, y, h, probs, g)


def _routing(top_e, tb):
    T = top_e.shape[0]
    A = T * TOP_K
    n_blocks = -(-(A + N_EXPERTS * (MOE_BLOCK - 1)) // MOE_BLOCK)
    R = n_blocks * MOE_BLOCK
    n_sb = N_EXPERTS + n_blocks // SB_BLOCKS
    i32 = jnp.int32
    e_flat = top_e.reshape(A)
    onehot = (e_flat[:, None] == jnp.arange(N_EXPERTS, dtype=i32)[None, :]).astype(i32)
    csum = jnp.cumsum(onehot, axis=0)
    rank = jnp.sum(csum * onehot, axis=1) - 1
    counts = csum[-1]
    blocks = (counts + MOE_BLOCK - 1) // MOE_BLOCK
    blk_end = jnp.cumsum(blocks)
    blk_start = blk_end - blocks
    dest = (blk_start * MOE_BLOCK)[e_flat] + rank
    row_tok = jnp.zeros((R,), i32).at[dest].set(jnp.arange(A, dtype=i32) // TOP_K, unique_indices=True)
    dest_km = dest.reshape(T // tb, tb, TOP_K).transpose(0, 2, 1).reshape(A).astype(i32)

    sbs = (blocks + SB_BLOCKS - 1) // SB_BLOCKS
    sb_end = jnp.cumsum(sbs)
    sb_start = sb_end - sbs
    s_idx = jnp.arange(n_sb, dtype=i32)
    active = s_idx < sb_end[-1]
    e_s = jnp.minimum(jnp.sum((sb_end[None, :] <= s_idx[:, None]).astype(i32), axis=1), N_EXPERTS - 1)
    k_s = s_idx - sb_start[e_s]
    sb_rb0 = jnp.where(active, blk_start[e_s] + k_s * SB_BLOCKS, 0)
    sb_nb = jnp.where(active, jnp.minimum(blocks[e_s] - k_s * SB_BLOCKS, SB_BLOCKS), 0)
    sb_e = jnp.where(active, e_s, jnp.max(jnp.where(active, e_s, 0)))
    n_used = blk_end[-1:].astype(i32)
    return row_tok, dest_km, sb_e.astype(i32), sb_rb0.astype(i32), sb_nb.astype(i32), n_used


def kernel(x, meta_tokens, norm1_g, w_in, b_igate, b_fgate, w_pool_mix, pool_scale, w_out, norm2_g, w_router,
           b_router, w_gu, b_gu, w_down, b_down, norm_f_g):
    B, S, D = x.shape
    L = N_META + S
    T = B * L
    H = MLSTM_HEADS
    BM_IN = 688
    BM_OUT = 688
    BM_NORM1 = 688
    BM_NORM = 688
    TB = 192
    assert T % BM_IN == 0 and T % BM_OUT == 0 and T % BM_NORM1 == 0 and T % BM_NORM == 0 and T % TB == 0
    assert w_in.shape[0] == 1
    assert L % N_META == 0 and TB % N_META == 0

    pieces = jnp.asarray(_piece_table(B, S))
    x2 = x.reshape(B * S, D)
    meta = meta_tokens.astype(x.dtype)

    l = 0
    w_in_t = w_in.reshape(D, PROJ_COLS + 2 * H).T
    bg = jnp.pad(jnp.concatenate([b_igate[l], b_fgate[l]]), (0, LANES - 2 * H)).reshape(1, LANES)
    n1, gates = _norm1(x2, meta, B, norm1_g[l].reshape(1, D), w_in_t, bg, BM_NORM1)

    proj3 = _inproj(n1, w_in_t, PROJ_COLS, BM_IN, 1024).reshape(B, L, PROJ_COLS)
    pool_out = _pool(proj3, w_pool_mix[l].astype(BF16), pool_scale[l].reshape(1, POOL_WIDTH))
    mlstm_out = _mlstm(proj3, gates.reshape(B, L, LANES))

    h1 = _outproj(pool_out.reshape(T, POOL_WIDTH), mlstm_out.reshape(T, MLSTM_WIDTH),
                  w_out.reshape(D, D), x2, meta, B, BM_OUT, 512)

    n2p, top_e, probs = _router(h1, norm2_g[l].reshape(1, D), w_router.reshape(D, N_EXPERTS).T,
                                b_router.reshape(1, N_EXPERTS), BM_NORM)

    row_tok, dest_km, sb_e, sb_rb0, sb_nb, n_used = _routing(top_e[:, :TOP_K], TB)
    y = _experts(sb_e, sb_rb0, sb_nb, n_used, row_tok, n2p,
                 w_gu.reshape(N_EXPERTS, D, 2 * D_FF), b_gu.reshape(N_EXPERTS, 1, 2 * D_FF),
                 w_down.reshape(N_EXPERTS, D_FF, D), b_down.reshape(N_EXPERTS, 1, D))
    out = _combine(dest_km, pieces, y, h1, probs, norm_f_g.reshape(1, D), TB, B * S)
    return out.reshape(B, S, D)
```

```python
import functools

import jax
import jax.numpy as jnp
import numpy as np
from jax import lax
from jax.experimental import pallas as pl
from jax.experimental.pallas import tpu as pltpu

D_MODEL = 4096
N_META = 16
POOL_WINDOWS = (2, 4, 8, 16)
POOL_WIDTH = D_MODEL // 4
POOL_GROUP = POOL_WIDTH // len(POOL_WINDOWS)
MLSTM_WIDTH = D_MODEL - POOL_WIDTH
MLSTM_HEADS = 6
MLSTM_V_DIM = MLSTM_WIDTH // MLSTM_HEADS
MLSTM_QK_DIM = MLSTM_V_DIM // 2
GATE_SOFTCAP = 15.0
N_EXPERTS = 32
TOP_K = 4
D_FF = D_MODEL // 2
SWIGLU_ALPHA = 1.702
SWIGLU_LIMIT = 7.0
MOE_BLOCK = 128
EPS = 1e-6

LANES = 128
SUBLANES = 8
SEQ_CHUNK = 256
PROJ_COLS = POOL_WIDTH + 2 * MLSTM_HEADS * MLSTM_QK_DIM + 2 * MLSTM_WIDTH
VMEM_LIMIT = 56 * 1024 * 1024
SB_BLOCKS = 10
CHUNK_SIZES = (6, 3, 2, 1)
CHUNK_BLOCKS = max(CHUNK_SIZES)
FF_TILE = 256
DOWN_TILE = 1024
N_FF_TILES = D_FF // FF_TILE
N_DOWN_TILES = D_MODEL // DOWN_TILE
Y_WORDS = DOWN_TILE // 2
GATHER_PER_VISIT = MOE_BLOCK // N_DOWN_TILES

F32 = jnp.float32
BF16 = jnp.bfloat16


def _params(sem, vmem=VMEM_LIMIT):
    return pltpu.CompilerParams(dimension_semantics=sem, vmem_limit_bytes=vmem)


def _split3(a):
    hi = a.astype(BF16)
    r1 = a - hi.astype(F32)
    mid = r1.astype(BF16)
    lo = (r1 - mid.astype(F32)).astype(BF16)
    return hi, mid, lo


def _dot(a, b):
    return jnp.dot(a, b, preferred_element_type=F32)


def _dotw(a, w):
    return lax.dot_general(a, w, (((1,), (0,)), ((), ())), preferred_element_type=F32)


def _dot_nt(a, wt):
    return lax.dot_general(a, wt, (((1,), (1,)), ((), ())), preferred_element_type=F32)


def _dot3_nt(a, wt):
    a_hi = a.astype(BF16)
    a_lo = (a - a_hi.astype(F32)).astype(BF16)
    w_hi = wt.astype(BF16)
    w_lo = (wt - w_hi.astype(F32)).astype(BF16)
    n = wt.shape[0]
    if n == LANES:
        both = _dot_nt(a_hi, jnp.concatenate([w_hi, w_lo], axis=0))
        return both[:, :n] + (both[:, n:] + _dot_nt(a_lo, w_hi))
    return _dot_nt(a_hi, w_hi) + (_dot_nt(a_hi, w_lo) + _dot_nt(a_lo, w_hi))


def _rms(x, g):
    return x * lax.rsqrt(jnp.mean(x * x, axis=-1, keepdims=True) + EPS) * g


GATE_ROWS = 16


def _piece_table(batch, seq):
    L = N_META + seq
    first = np.arange(batch * L // N_META) * N_META
    pos = first % L
    return np.where(pos >= N_META, (first // L) * seq + pos - N_META, -1).astype(np.int32)


def _token_block_start(blk, seq, x_hbm, meta_hbm, cols, dst, sem):
    bm = dst.shape[0]
    per_seq = (N_META + seq) // bm
    b = blk // per_seq
    r = blk - b * per_seq

    @pl.when(r == 0)
    def _():
        pltpu.make_async_copy(meta_hbm.at[:, cols], dst.at[pl.ds(0, N_META), :], sem).start()
        pltpu.make_async_copy(x_hbm.at[pl.ds(pl.multiple_of(b * seq, N_META), bm - N_META), cols],
                              dst.at[pl.ds(N_META, bm - N_META), :], sem).start()

    @pl.when(r > 0)
    def _():
        pltpu.make_async_copy(x_hbm.at[pl.ds(pl.multiple_of(b * seq + r * bm - N_META, N_META), bm), cols],
                              dst, sem).start()


def _token_block_wait(x_hbm, cols, dst, sem):
    pltpu.make_async_copy(x_hbm.at[pl.ds(0, dst.shape[0]), cols], dst, sem).wait()


def _norm1_kernel(seq, x_hbm, meta_hbm, g_ref, wg_ref, bg_ref, n_ref, gate_ref, hbuf, sem):
    i = pl.program_id(0)
    all_cols = pl.ds(0, D_MODEL)

    @pl.when(i == 0)
    def _():
        _token_block_start(0, seq, x_hbm, meta_hbm, all_cols, hbuf.at[0], sem.at[0])

    def step(slot):
        @pl.when(i + 1 < pl.num_programs(0))
        def _():
            _token_block_start(i + 1, seq, x_hbm, meta_hbm, all_cols, hbuf.at[1 - slot], sem.at[1 - slot])
        _token_block_wait(x_hbm, all_cols, hbuf.at[slot], sem.at[slot])
        _norm1_body(hbuf[slot], g_ref, wg_ref, bg_ref, n_ref, gate_ref)

    for slot in range(2):
        pl.when(lax.rem(i, 2) == slot)(functools.partial(step, slot))


def _norm1_body(h, g_ref, wg_ref, bg_ref, n_ref, gate_ref):
    y = _rms(h, g_ref[...])
    n_ref[...] = y.astype(BF16)
    sub = lax.broadcasted_iota(jnp.int32, (GATE_ROWS, D_MODEL), 0)
    wg = jnp.where(sub < 2 * MLSTM_HEADS, wg_ref[...], 0.0)
    wg = jnp.concatenate([wg, jnp.zeros((LANES - GATE_ROWS, D_MODEL), F32)], axis=0)
    gate_ref[...] = _dot3_nt(y, wg) + bg_ref[...]


def _norm1(x2, meta, batch, g, w_in_t, bg, bm):
    seq = x2.shape[0] // batch
    T = batch * (N_META + seq)
    assert (N_META + seq) % bm == 0
    return pl.pallas_call(
        functools.partial(_norm1_kernel, seq),
        grid=(T // bm,),
        in_specs=[
            pl.BlockSpec(memory_space=pl.ANY),
            pl.BlockSpec(memory_space=pl.ANY),
            pl.BlockSpec((1, D_MODEL), lambda i: (0, 0)),
            pl.BlockSpec((GATE_ROWS, D_MODEL), lambda i: (PROJ_COLS // GATE_ROWS, 0)),
            pl.BlockSpec((1, LANES), lambda i: (0, 0)),
        ],
        out_specs=[
            pl.BlockSpec((bm, D_MODEL), lambda i: (i, 0)),
            pl.BlockSpec((bm, LANES), lambda i: (i, 0)),
        ],
        out_shape=[
            jax.ShapeDtypeStruct((T, D_MODEL), BF16),
            jax.ShapeDtypeStruct((T, LANES), F32),
        ],
        scratch_shapes=[pltpu.VMEM((2, bm, D_MODEL), F32), pltpu.SemaphoreType.DMA((2,))],
        compiler_params=_params(("arbitrary",)),
        name="norm1_gates",
    )(x2, meta, g, w_in_t, bg)


def _inproj_kernel(x_ref, wt_ref, o_ref):
    o_ref[...] = _dot_nt(x_ref[...], wt_ref[...]).astype(o_ref.dtype)


def _inproj(x, wt, n_cols, bm, bn):
    M, K = x.shape
    return pl.pallas_call(
        _inproj_kernel,
        grid=(n_cols // bn, M // bm),
        in_specs=[
            pl.BlockSpec((bm, K), lambda j, i: (i, 0)),
            pl.BlockSpec((bn, K), lambda j, i: (j, 0)),
        ],
        out_specs=pl.BlockSpec((bm, bn), lambda j, i: (i, j)),
        out_shape=jax.ShapeDtypeStruct((M, n_cols), F32),
        compiler_params=_params(("parallel", "parallel")),
        name="in_proj",
    )(x, wt)


def _outproj_kernel(seq, p_ref, m_ref, w_ref, x_hbm, meta_hbm, o_ref, rbuf, sem):
    j = pl.program_id(0)
    i = pl.program_id(1)
    ni = pl.num_programs(1)
    bn = rbuf.shape[2]
    t = j * ni + i

    def fetch(step, slot):
        cols = pl.ds(pl.multiple_of((step // ni) * bn, bn), bn)
        _token_block_start(lax.rem(step, ni), seq, x_hbm, meta_hbm, cols, rbuf.at[slot], sem.at[slot])

    @pl.when(t == 0)
    def _():
        fetch(0, 0)

    def step(slot):
        @pl.when(t + 1 < pl.num_programs(0) * ni)
        def _():
            fetch(t + 1, 1 - slot)
        acc = _dotw(p_ref[...], w_ref[:POOL_WIDTH, :]) + _dotw(m_ref[...], w_ref[POOL_WIDTH:, :])
        _token_block_wait(x_hbm, pl.ds(0, bn), rbuf.at[slot], sem.at[slot])
        o_ref[...] = rbuf[slot] + acc

    for slot in range(2):
        pl.when(lax.rem(t, 2) == slot)(functools.partial(step, slot))


def _outproj(p, m, w, x2, meta, batch, bm, bn):
    M = p.shape[0]
    K, N = w.shape
    seq = x2.shape[0] // batch
    assert (N_META + seq) % bm == 0
    return pl.pallas_call(
        functools.partial(_outproj_kernel, seq),
        grid=(N // bn, M // bm),
        in_specs=[
            pl.BlockSpec((bm, POOL_WIDTH), lambda j, i: (i, 0)),
            pl.BlockSpec((bm, MLSTM_WIDTH), lambda j, i: (i, 0)),
            pl.BlockSpec((K, bn), lambda j, i: (0, j)),
            pl.BlockSpec(memory_space=pl.ANY),
            pl.BlockSpec(memory_space=pl.ANY),
        ],
        out_specs=pl.BlockSpec((bm, bn), lambda j, i: (i, j)),
        out_shape=jax.ShapeDtypeStruct((M, N), F32),
        scratch_shapes=[pltpu.VMEM((2, bm, bn), F32), pltpu.SemaphoreType.DMA((2,))],
        compiler_params=_params(("arbitrary", "arbitrary")),
        name="out_proj",
    )(p, m, w, x2, meta)


def _pool_kernel(u_ref, w_ref, s_ref, o_ref, carry_ref):
    c = pl.program_id(1)

    @pl.when(c == 0)
    def _():
        carry_ref[...] = jnp.zeros_like(carry_ref)

    u = u_ref[0]
    ext = jnp.concatenate([carry_ref[...], u], axis=0)
    carry_ref[...] = u[SEQ_CHUNK - 16:, :]
    pos = c * SEQ_CHUNK + lax.broadcasted_iota(jnp.int32, (SEQ_CHUNK, 1), 0)
    for g, win in enumerate(POOL_WINDOWS):
        cols = slice(g * POOL_GROUP, (g + 1) * POOL_GROUP)
        s = ext[:, cols]
        span = 1
        while span < win:
            s = s + pltpu.roll(s, span, axis=0)
            span *= 2
        cnt = jnp.minimum(pos + 1, win).astype(F32)
        d = s[16:, :] / cnt - u[:, cols]
        y = _dot(d.astype(BF16), w_ref[g]) * s_ref[:, cols]
        o_ref[0, :, cols] = y.astype(o_ref.dtype)


def _pool(proj3, w_mix, scale):
    B, L, _ = proj3.shape
    nc = pl.cdiv(L, SEQ_CHUNK)
    return pl.pallas_call(
        _pool_kernel,
        grid=(B, nc),
        in_specs=[
            pl.BlockSpec((1, SEQ_CHUNK, POOL_WIDTH), lambda b, c: (b, c, 0)),
            pl.BlockSpec((len(POOL_WINDOWS), POOL_GROUP, POOL_GROUP), lambda b, c: (0, 0, 0)),
            pl.BlockSpec((1, POOL_WIDTH), lambda b, c: (0, 0)),
        ],
        out_specs=pl.BlockSpec((1, SEQ_CHUNK, POOL_WIDTH), lambda b, c: (b, c, 0)),
        out_shape=jax.ShapeDtypeStruct((B, L, POOL_WIDTH), BF16),
        scratch_shapes=[pltpu.VMEM((16, POOL_WIDTH), F32)],
        compiler_params=_params(("parallel", "arbitrary")),
        name="pool_mixer",
    )(proj3, w_mix, scale)


def _soft_cap(a):
    return GATE_SOFTCAP * jnp.tanh(a / GATE_SOFTCAP)


def _log_sigmoid(a):
    return jnp.minimum(a, 0.0) - jnp.log1p(jnp.exp(-jnp.abs(a)))


HEADS_PER_STEP = 2


def _mlstm_kernel(seq_len, q_ref, k_ref, v_ref, o_ref, gate_ref, out_ref, c_ref, n_ref, m_ref):
    c = pl.program_id(2)
    Lc = SEQ_CHUNK
    dk, dv = MLSTM_QK_DIM, MLSTM_V_DIM

    @pl.when(c == 0)
    def _():
        c_ref[...] = jnp.zeros_like(c_ref)
        n_ref[...] = jnp.zeros_like(n_ref)
        m_ref[...] = jnp.zeros_like(m_ref)

    row = lax.broadcasted_iota(jnp.int32, (Lc, 1), 0)
    col = lax.broadcasted_iota(jnp.int32, (1, Lc), 1)
    ok_col = (c * Lc + row) < seq_len
    causal = col <= row
    tri = causal.astype(BF16)
    tri_t = (row <= col).astype(BF16)
    lane = lax.broadcasted_iota(jnp.int32, (1, LANES), 1)
    capped = _soft_cap(gate_ref[0])
    gc = jnp.where(ok_col, jnp.where(lane < MLSTM_HEADS, capped, _log_sigmoid(capped)), 0.0)
    gt = gc.T
    for i in range(HEADS_PER_STEP):
        _mlstm_head(pl.program_id(1) * HEADS_PER_STEP + i, ok_col, causal, tri, tri_t, gc, gt,
                    q_ref[0, :, i * dk:(i + 1) * dk], k_ref[0, :, i * dk:(i + 1) * dk],
                    v_ref[0, :, i * dv:(i + 1) * dv], o_ref[0, :, i * dv:(i + 1) * dv],
                    out_ref.at[0, :, i * dv:(i + 1) * dv], c_ref.at[i], n_ref.at[i], m_ref.at[i])


def _mlstm_head(h, ok_col, causal, tri, tri_t, gc, gt, q_in, k_in, v_in, o_in, out_ref, c_ref, n_ref, m_ref):
    Lc = SEQ_CHUNK
    q = jnp.where(ok_col, q_in, 0.0) * (MLSTM_QK_DIM ** -0.5)
    k = jnp.where(ok_col, k_in, 0.0)
    v = jnp.where(ok_col, v_in, 0.0)

    lane = lax.broadcasted_iota(jnp.int32, (1, LANES), 1)
    sub = lax.broadcasted_iota(jnp.int32, (LANES, 1), 0)
    i_c = jnp.sum(jnp.where(lane == h, gc, 0.0), axis=1, keepdims=True)
    f_c = jnp.sum(jnp.where(lane == h + MLSTM_HEADS, gc, 0.0), axis=1, keepdims=True)
    i_r = jnp.sum(jnp.where(sub == h, gt, 0.0), axis=0, keepdims=True)
    f_r = jnp.sum(jnp.where(sub == h + MLSTM_HEADS, gt, 0.0), axis=0, keepdims=True)

    fb_c = jnp.broadcast_to(f_c, (Lc, LANES))
    b_c = sum(_dot(tri, p) for p in _split3(fb_c))[:, 0:1]
    fb_r = jnp.broadcast_to(f_r, (8, Lc))
    b_r8 = sum(_dot(p, tri_t) for p in _split3(fb_r))
    b_r = b_r8[0:1, :]
    g_tot = b_r8[0:1, Lc - 1:Lc]

    m_prev = m_ref[...]
    dlog = jnp.where(causal, b_c - b_r + i_r, -jnp.inf)
    inter_log = b_c + m_prev
    m_out = jnp.maximum(inter_log, jnp.max(dlog, axis=1, keepdims=True))
    wts = jnp.exp(dlog - m_out)
    inter_w = jnp.exp(inter_log - m_out)

    qb = q.astype(BF16)
    kb = k.astype(BF16)
    vb = v.astype(BF16)
    s = lax.dot_general(qb, kb, (((1,), (1,)), ((), ())), preferred_element_type=F32) * wts
    num = _dot(s.astype(BF16), vb) + inter_w * _dot(qb, c_ref[...].astype(BF16))
    den = jnp.sum(s, axis=1, keepdims=True) + inter_w * jnp.sum(q * n_ref[...], axis=1, keepdims=True)
    hh = num / jnp.maximum(jnp.abs(den), jnp.exp(-m_out))
    out_ref[...] = (jax.nn.sigmoid(o_in) * hh).astype(out_ref.dtype)

    a_c = g_tot - b_c + i_c
    m_new = jnp.maximum(g_tot + m_prev, jnp.max(a_c, axis=0, keepdims=True))
    wk = k * jnp.exp(a_c - m_new)
    decay = jnp.exp(g_tot + m_prev - m_new)
    c_ref[...] = decay * c_ref[...] + lax.dot_general(
        wk.astype(BF16), vb, (((0,), (0,)), ((), ())), preferred_element_type=F32)
    n_ref[...] = decay * n_ref[...] + jnp.sum(wk, axis=0, keepdims=True)
    m_ref[...] = m_new


def _mlstm(proj3, gates3):
    B, L, _ = proj3.shape
    nc = pl.cdiv(L, SEQ_CHUNK)
    G = HEADS_PER_STEP
    dk, dv, H = G * MLSTM_QK_DIM, G * MLSTM_V_DIM, MLSTM_HEADS // G
    q0 = POOL_WIDTH // dk
    k0 = q0 + H
    v0 = (POOL_WIDTH + 2 * MLSTM_HEADS * MLSTM_QK_DIM) // dv
    o0 = v0 + H
    return pl.pallas_call(
        functools.partial(_mlstm_kernel, L),
        grid=(B, H, nc),
        in_specs=[
            pl.BlockSpec((1, SEQ_CHUNK, dk), lambda b, h, c: (b, c, q0 + h)),
            pl.BlockSpec((1, SEQ_CHUNK, dk), lambda b, h, c: (b, c, k0 + h)),
            pl.BlockSpec((1, SEQ_CHUNK, dv), lambda b, h, c: (b, c, v0 + h)),
            pl.BlockSpec((1, SEQ_CHUNK, dv), lambda b, h, c: (b, c, o0 + h)),
            pl.BlockSpec((1, SEQ_CHUNK, LANES), lambda b, h, c: (b, c, 0)),
        ],
        out_specs=pl.BlockSpec((1, SEQ_CHUNK, dv), lambda b, h, c: (b, c, h)),
        out_shape=jax.ShapeDtypeStruct((B, L, MLSTM_WIDTH), BF16),
        scratch_shapes=[
            pltpu.VMEM((G, MLSTM_QK_DIM, MLSTM_V_DIM), F32),
            pltpu.VMEM((G, 1, MLSTM_QK_DIM), F32),
            pltpu.VMEM((G, 1, 1), F32),
        ],
        compiler_params=_params(("parallel", "parallel", "arbitrary")),
        name="mlstm",
    )(proj3, proj3, proj3, proj3, gates3)


def _pack_bf16_pair(lo, hi):
    lo_b = lax.bitcast_convert_type(lo.astype(BF16).astype(F32), jnp.uint32)
    hi_b = lax.bitcast_convert_type(hi.astype(BF16).astype(F32), jnp.uint32)
    return (lo_b >> 16) | hi_b


def _unpack_bf16_pair(w):
    lo = lax.bitcast_convert_type(w << 16, F32).astype(BF16)
    hi = lax.bitcast_convert_type(w & jnp.uint32(0xFFFF0000), F32).astype(BF16)
    return lo, hi


def _router_kernel(h_ref, g_ref, wr_ref, br_ref, n_ref, e_ref, p_ref):
    y = _rms(h_ref[...], g_ref[...])
    half = D_MODEL // 2
    n_ref[...] = _pack_bf16_pair(y[:, :half], y[:, half:])
    logits = _dot3_nt(y, wr_ref[...]) + br_ref[...]
    lane = lax.broadcasted_iota(jnp.int32, logits.shape, 1).astype(F32)
    l = jnp.where(lane < N_EXPERTS, logits, -jnp.inf)
    vals, idxs = [], []
    for _ in range(TOP_K):
        m = jnp.max(l, axis=1, keepdims=True)
        idx = jnp.min(jnp.where(l == m, lane, float(LANES)), axis=1, keepdims=True)
        vals.append(m)
        idxs.append(idx)
        l = jnp.where(lane == idx, -jnp.inf, l)
    ex = [jnp.exp(v - vals[0]) for v in vals]
    tot = ex[0] + ex[1] + ex[2] + ex[3]
    e_out = jnp.zeros(logits.shape, F32)
    p_out = jnp.zeros(logits.shape, F32)
    for kk in range(TOP_K):
        e_out = jnp.where(lane == kk, idxs[kk], e_out)
        p_out = jnp.where(lane == kk, ex[kk] / tot, p_out)
    e_ref[...] = e_out.astype(jnp.int32)
    p_ref[...] = p_out


def _router(h, g, wr, br, bm):
    T = h.shape[0]
    return pl.pallas_call(
        _router_kernel,
        grid=(T // bm,),
        in_specs=[
            pl.BlockSpec((bm, D_MODEL), lambda i: (i, 0)),
            pl.BlockSpec((1, D_MODEL), lambda i: (0, 0)),
            pl.BlockSpec((N_EXPERTS, D_MODEL), lambda i: (0, 0)),
            pl.BlockSpec((1, N_EXPERTS), lambda i: (0, 0)),
        ],
        out_specs=[
            pl.BlockSpec((bm, D_MODEL // 2), lambda i: (i, 0)),
            pl.BlockSpec((bm, N_EXPERTS), lambda i: (i, 0)),
            pl.BlockSpec((bm, N_EXPERTS), lambda i: (i, 0)),
        ],
        out_shape=[
            jax.ShapeDtypeStruct((T, D_MODEL // 2), jnp.uint32),
            jax.ShapeDtypeStruct((T, N_EXPERTS), jnp.int32),
            jax.ShapeDtypeStruct((T, N_EXPERTS), F32),
        ],
        compiler_params=_params(("parallel",)),
        name="norm2_router",
    )(h, g, wr, br)


def _row_gather_start(src_hbm, idx_ref, base, n_rows, buf, slot, sem):
    for r in range(n_rows):
        tok = idx_ref[base + r]
        pltpu.make_async_copy(src_hbm.at[pl.ds(tok, 1), :], buf.at[slot, pl.ds(r, 1), :], sem.at[slot]).start()


def _row_gather_wait(src_hbm, n_rows, buf, slot, sem):
    pltpu.make_async_copy(src_hbm.at[pl.ds(0, n_rows), :], buf.at[slot], sem.at[slot]).wait()


def _experts_kernel(sbe_ref, rb0_ref, nbk_ref, used_ref, tok_ref,
                    x_hbm, wgu_hbm, wd_hbm, bgu_ref, bd_ref,
                    y_hbm,
                    x_res, hmid, wgl_buf, wd_buf, ystage, gsem, wsem, dsem, ysem, ypend, ycnt, gcnt):
    s = pl.program_id(0)
    ns = pl.num_programs(0)
    nb = nbk_ref[s]
    expert = sbe_ref[s]
    nxt = jnp.minimum(s + 1, ns - 1)
    next_active = jnp.logical_and(s + 1 < ns, nbk_ref[nxt] > 0)
    half = D_MODEL // 2
    n_blocks_total = y_hbm.shape[0] // MOE_BLOCK

    def gateup_copies(e, j, slot):
        gate_cols = pl.ds(pl.multiple_of(j * FF_TILE, FF_TILE), FF_TILE)
        lin_cols = pl.ds(pl.multiple_of(D_FF + j * FF_TILE, FF_TILE), FF_TILE)
        return (pltpu.make_async_copy(wgu_hbm.at[e, :, gate_cols], wgl_buf.at[slot, 0], wsem.at[slot]),
                pltpu.make_async_copy(wgu_hbm.at[e, :, lin_cols], wgl_buf.at[slot, 1], wsem.at[slot]))

    def down_copy(e, j, slot):
        cols = pl.ds(pl.multiple_of(j * DOWN_TILE, DOWN_TILE), DOWN_TILE)
        return pltpu.make_async_copy(wd_hbm.at[e, :, cols], wd_buf.at[slot], dsem.at[slot])

    def x_row_copy(tok, r8, rs):
        return pltpu.make_async_copy(x_hbm.at[tok >> 3, pl.ds(tok & 7, 1), :], x_res.at[r8, pl.ds(rs, 1), :], gsem.at[0])

    def x_block_wait():
        g = MOE_BLOCK // SUBLANES
        pltpu.make_async_copy(x_hbm.at[pl.ds(0, g)], x_res.at[pl.ds(0, g)], gsem.at[0]).wait()

    def gather_upto(sb, n_rows):
        base = rb0_ref[sb] * MOE_BLOCK

        def body(r, carry):
            x_row_copy(tok_ref[base + r], r >> 3, r & 7).start()
            return carry
        lax.fori_loop(gcnt[0], n_rows, body, 0)
        gcnt[0] = jnp.maximum(gcnt[0], n_rows)

    def gather_some(sb, n):
        base = rb0_ref[sb] * MOE_BLOCK
        cur = gcnt[0]
        cur8 = cur >> 3
        for i in range(n):
            x_row_copy(tok_ref[base + cur + i], cur8 + i // SUBLANES, i % SUBLANES).start()
        gcnt[0] = cur + n

    def gather_wait():
        n = gcnt[0]
        n_full = n // MOE_BLOCK

        def block_body(r, carry):
            x_block_wait()
            return carry

        def row_body(r, carry):
            x_row_copy(0, 0, 0).wait()
            return carry
        lax.fori_loop(0, n_full, block_body, 0)
        lax.fori_loop(n_full * MOE_BLOCK, n, row_body, 0)
        gcnt[0] = 0

    def y_copy(slot, piece, row0, word0):
        return pltpu.make_async_copy(
            ystage.at[slot, pl.ds(piece * MOE_BLOCK, MOE_BLOCK), :],
            y_hbm.at[pl.ds(pl.multiple_of(row0, MOE_BLOCK), MOE_BLOCK), pl.ds(pl.multiple_of(word0, Y_WORDS), Y_WORDS)],
            ysem.at[slot])

    def y_drain(slot):
        for piece in range(CHUNK_BLOCKS):
            @pl.when(ypend[slot] > piece)
            def _():
                y_copy(slot, 0, 0, 0).wait()
        ypend[slot] = 0

    def y_emit(val, n_pieces, slot, row0, word0):
        ystage[slot, pl.ds(0, n_pieces * MOE_BLOCK), :] = _pack_bf16_pair(val[:, :Y_WORDS], val[:, Y_WORDS:])
        for piece in range(n_pieces):
            y_copy(slot, piece, row0 + piece * MOE_BLOCK, word0).start()
        ypend[slot] = n_pieces

    def for_chunks(n_blocks, fn):
        done = 0
        for pos, k in enumerate(CHUNK_SIZES):
            if pos == 0:
                n_big = n_blocks // k

                def body(c, carry, k=k):
                    fn(c * k, k)
                    return carry
                lax.fori_loop(0, n_big, body, 0)
                done = n_big * k
            else:
                fits = n_blocks - done >= k

                @pl.when(fits)
                def _(k=k, done=done):
                    fn(done, k)
                done = jnp.where(fits, done + k, done)

    @pl.when(s == 0)
    def _():
        ypend[0] = 0
        ypend[1] = 0
        ycnt[0] = 0
        gcnt[0] = 0

        @pl.when(nb > 0)
        def _():
            for c in gateup_copies(expert, 0, 0):
                c.start()
        gather_upto(0, nb * MOE_BLOCK)

    @pl.when(nb > 0)
    def _():
        gather_wait()
        row_base = rb0_ref[s] * MOE_BLOCK

        def gateup_tile(j, carry):
            slot = lax.rem(j, 2)
            for c in gateup_copies(expert, j, slot):
                c.wait()

            @pl.when(j + 1 < N_FF_TILES)
            def _():
                for c in gateup_copies(expert, j + 1, 1 - slot):
                    c.start()

            @pl.when(j + 1 == N_FF_TILES)
            def _():
                down_copy(expert, 0, 0).start()

            bg = bgu_ref[0, pl.ds(j, 1), :]
            bl = bgu_ref[0, pl.ds(N_FF_TILES + j, 1), :]

            def chunk(b0, k):
                rows = pl.ds(pl.multiple_of(b0 * MOE_BLOCK, MOE_BLOCK), k * MOE_BLOCK)
                groups = pl.ds(pl.multiple_of(b0 * (MOE_BLOCK // SUBLANES), MOE_BLOCK // SUBLANES),
                               k * (MOE_BLOCK // SUBLANES))
                x_lo, x_hi = _unpack_bf16_pair(x_res[groups].reshape(k * MOE_BLOCK, half))
                gate = _dotw(x_lo, wgl_buf[slot, 0, :half, :]) + _dotw(x_hi, wgl_buf[slot, 0, half:, :]) + bg
                lin = _dotw(x_lo, wgl_buf[slot, 1, :half, :]) + _dotw(x_hi, wgl_buf[slot, 1, half:, :]) + bl
                gate = jnp.minimum(gate, SWIGLU_LIMIT)
                lin = jnp.clip(lin, -SWIGLU_LIMIT, SWIGLU_LIMIT)
                hmid[j, rows, :] = (gate * jax.nn.sigmoid(SWIGLU_ALPHA * gate) * (lin + 1.0)).astype(BF16)
            for_chunks(nb, chunk)
            return carry
        lax.fori_loop(0, N_FF_TILES, gateup_tile, 0)

        def down_tile(j, carry):
            slot = lax.rem(j, 2)
            down_copy(expert, j, slot).wait()

            @pl.when(j + 1 < N_DOWN_TILES)
            def _():
                down_copy(expert, j + 1, 1 - slot).start()

            @pl.when(jnp.logical_and(j + 1 == N_DOWN_TILES, next_active))
            def _():
                for c in gateup_copies(sbe_ref[nxt], 0, 0):
                    c.start()

            bd = bd_ref[0, pl.ds(j, 1), :]
            word0 = j * Y_WORDS

            def chunk(b0, k):
                yslot = lax.rem(ycnt[0], 2)
                ycnt[0] = ycnt[0] + 1
                y_drain(yslot)
                gather_some(nxt, k * GATHER_PER_VISIT)
                rows = pl.ds(pl.multiple_of(b0 * MOE_BLOCK, MOE_BLOCK), k * MOE_BLOCK)
                h = jnp.concatenate([hmid[jj, rows, :] for jj in range(N_FF_TILES)], axis=1)
                acc = bd + _dotw(h, wd_buf[slot])
                y_emit(acc, k, yslot, row_base + b0 * MOE_BLOCK, word0)
            for_chunks(nb, chunk)
            return carry
        lax.fori_loop(0, N_DOWN_TILES, down_tile, 0)
        gather_upto(nxt, jnp.where(s + 1 < ns, nbk_ref[nxt] * MOE_BLOCK, 0))

    @pl.when(s == ns - 1)
    def _():
        gather_wait()
        y_drain(0)
        y_drain(1)
        ystage[0, pl.ds(0, MOE_BLOCK), :] = jnp.zeros((MOE_BLOCK, Y_WORDS), jnp.uint32)

        def zero_copy(b, j):
            return y_copy(0, 0, b * MOE_BLOCK, j * Y_WORDS)

        def start_body(b, carry):
            for j in range(N_DOWN_TILES):
                zero_copy(b, j).start()
            return carry

        def wait_body(b, carry):
            for j in range(N_DOWN_TILES):
                zero_copy(0, 0).wait()
            return carry
        lax.fori_loop(used_ref[0], n_blocks_total, start_body, 0)
        lax.fori_loop(used_ref[0], n_blocks_total, wait_body, 0)


def _experts(sb_e, sb_rb0, sb_nb, n_used, row_tok, n2p, w_gu, b_gu, w_down, b_down):
    R = row_tok.shape[0] - SB_BLOCKS * MOE_BLOCK
    S = sb_e.shape[0]
    sb_rows = SB_BLOCKS * MOE_BLOCK

    grid_spec = pltpu.PrefetchScalarGridSpec(
        num_scalar_prefetch=5,
        grid=(S,),
        in_specs=[
            pl.BlockSpec(memory_space=pl.ANY),
            pl.BlockSpec(memory_space=pl.ANY),
            pl.BlockSpec(memory_space=pl.ANY),
            pl.BlockSpec((1, 2 * N_FF_TILES, FF_TILE), lambda s, e, r0, nbk, u, tok: (e[s], 0, 0)),
            pl.BlockSpec((1, N_DOWN_TILES, DOWN_TILE), lambda s, e, r0, nbk, u, tok: (e[s], 0, 0)),
        ],
        out_specs=pl.BlockSpec(memory_space=pl.ANY),
        scratch_shapes=[
            pltpu.VMEM((sb_rows // SUBLANES, SUBLANES, D_MODEL // 2), jnp.uint32),
            pltpu.VMEM((N_FF_TILES, sb_rows, FF_TILE), BF16),
            pltpu.VMEM((2, 2, D_MODEL, FF_TILE), F32),
            pltpu.VMEM((2, D_FF, DOWN_TILE), F32),
            pltpu.VMEM((2, CHUNK_BLOCKS * MOE_BLOCK, Y_WORDS), jnp.uint32),
            pltpu.SemaphoreType.DMA((1,)),
            pltpu.SemaphoreType.DMA((2,)),
            pltpu.SemaphoreType.DMA((2,)),
            pltpu.SemaphoreType.DMA((2,)),
            pltpu.SMEM((2,), jnp.int32),
            pltpu.SMEM((1,), jnp.int32),
            pltpu.SMEM((1,), jnp.int32),
        ],
    )
    return pl.pallas_call(
        _experts_kernel,
        grid_spec=grid_spec,
        out_shape=jax.ShapeDtypeStruct((R, D_MODEL // 2), jnp.uint32),
        compiler_params=_params(("arbitrary",)),
        name="moe_experts",
    )(sb_e, sb_rb0, sb_nb, n_used, row_tok, n2p.reshape(-1, SUBLANES, D_MODEL // 2), w_gu, w_down,
      b_gu.reshape(N_EXPERTS, 2 * N_FF_TILES, FF_TILE), b_down.reshape(N_EXPERTS, N_DOWN_TILES, DOWN_TILE))


def _combine_kernel(tb, dest_ref, piece_ref, y_hbm, h_ref, p_ref, g_ref, out_hbm, buf, ostage, sem, osem, opend):
    i = pl.program_id(0)
    nb = pl.num_programs(0)
    rows = tb * TOP_K
    n_pieces = tb // N_META

    def out_copy(slot, piece, dst):
        return pltpu.make_async_copy(ostage.at[slot, pl.ds(piece * N_META, N_META), :],
                                     out_hbm.at[pl.ds(pl.multiple_of(dst, N_META), N_META), :], osem.at[slot])

    def out_drain(slot):
        for piece in range(n_pieces):
            @pl.when(opend[slot] > piece)
            def _():
                out_copy(slot, 0, 0).wait()
        opend[slot] = 0

    @pl.when(i == 0)
    def _():
        opend[0] = 0
        opend[1] = 0
        _row_gather_start(y_hbm, dest_ref, 0, rows, buf, 0, sem)

    def step(slot):
        @pl.when(i + 1 < nb)
        def _():
            _row_gather_start(y_hbm, dest_ref, (i + 1) * rows, rows, buf, 1 - slot, sem)

        _row_gather_wait(y_hbm, rows, buf, slot, sem)
        acc = h_ref[...]
        p = p_ref[...]
        for kk in range(TOP_K):
            w = buf[slot, pl.ds(kk * tb, tb), :]
            lo = lax.bitcast_convert_type(w << 16, F32)
            hi = lax.bitcast_convert_type(w & jnp.uint32(0xFFFF0000), F32)
            pieces = []
            for j in range(N_DOWN_TILES):
                pieces += [lo[:, j * Y_WORDS:(j + 1) * Y_WORDS], hi[:, j * Y_WORDS:(j + 1) * Y_WORDS]]
            acc = acc + p[:, kk:kk + 1] * jnp.concatenate(pieces, axis=1)
        out_drain(slot)
        ostage[slot] = _rms(acc, g_ref[...])
        started = 0
        for piece in range(n_pieces):
            dst = piece_ref[i * n_pieces + piece]

            @pl.when(dst >= 0)
            def _():
                out_copy(slot, piece, dst).start()
            started = started + (dst >= 0).astype(jnp.int32)
        opend[slot] = started

    for slot in range(2):
        pl.when(lax.rem(i, 2) == slot)(functools.partial(step, slot))

    @pl.when(i == nb - 1)
    def _():
        out_drain(0)
        out_drain(1)


def _combine(dest_km, pieces, y, h, probs, g, tb, n_out):
    T = h.shape[0]
    grid_spec = pltpu.PrefetchScalarGridSpec(
        num_scalar_prefetch=2,
        grid=(T // tb,),
        in_specs=[
            pl.BlockSpec(memory_space=pl.ANY),
            pl.BlockSpec((tb, D_MODEL), lambda i, d, q: (i, 0)),
            pl.BlockSpec((tb, N_EXPERTS), lambda i, d, q: (i, 0)),
            pl.BlockSpec((1, D_MODEL), lambda i, d, q: (0, 0)),
        ],
        out_specs=pl.BlockSpec(memory_space=pl.ANY),
        scratch_shapes=[
            pltpu.VMEM((2, tb * TOP_K, D_MODEL // 2), jnp.uint32),
            pltpu.VMEM((2, tb, D_MODEL), F32),
            pltpu.SemaphoreType.DMA((2,)),
            pltpu.SemaphoreType.DMA((2,)),
            pltpu.SMEM((2,), jnp.int32),
        ],
    )
    return pl.pallas_call(
        functools.partial(_combine_kernel, tb),
        grid_spec=grid_spec,
        out_shape=jax.ShapeDtypeStruct((n_out, D_MODEL), F32),
        compiler_params=_params(("arbitrary",)),
        name="moe_combine_norm",
    )(dest_km, pieces, y, h, probs, g)


def _routing(top_e, tb):
    T = top_e.shape[0]
    A = T * TOP_K
    n_blocks = -(-(A + N_EXPERTS * (MOE_BLOCK - 1)) // MOE_BLOCK)
    R = n_blocks * MOE_BLOCK
    n_sb = N_EXPERTS + n_blocks // SB_BLOCKS
    i32 = jnp.int32
    e_flat = top_e.reshape(A)
    onehot = (e_flat[:, None] == jnp.arange(N_EXPERTS, dtype=i32)[None, :]).astype(i32)
    csum = jnp.cumsum(onehot, axis=0)
    rank = jnp.sum(csum * onehot, axis=1) - 1
    counts = csum[-1]
    blocks = (counts + MOE_BLOCK - 1) // MOE_BLOCK
    blk_end = jnp.cumsum(blocks)
    blk_start = blk_end - blocks
    dest = (blk_start * MOE_BLOCK)[e_flat] + rank
    row_tok = jnp.zeros((R + SB_BLOCKS * MOE_BLOCK,), i32).at[dest].set(
        jnp.arange(A, dtype=i32) // TOP_K, unique_indices=True)
    dest_km = dest.reshape(T // tb, tb, TOP_K).transpose(0, 2, 1).reshape(A).astype(i32)

    sbs = (blocks + SB_BLOCKS - 1) // SB_BLOCKS
    sb_end = jnp.cumsum(sbs)
    sb_start = sb_end - sbs
    s_idx = jnp.arange(n_sb, dtype=i32)
    active = s_idx < sb_end[-1]
    e_s = jnp.minimum(jnp.sum((sb_end[None, :] <= s_idx[:, None]).astype(i32), axis=1), N_EXPERTS - 1)
    k_s = s_idx - sb_start[e_s]
    sb_rb0 = jnp.where(active, blk_start[e_s] + k_s * SB_BLOCKS, 0)
    sb_nb = jnp.where(active, jnp.minimum(blocks[e_s] - k_s * SB_BLOCKS, SB_BLOCKS), 0)
    sb_e = jnp.where(active, e_s, jnp.max(jnp.where(active, e_s, 0)))
    n_used = blk_end[-1:].astype(i32)
    return row_tok, dest_km, sb_e.astype(i32), sb_rb0.astype(i32), sb_nb.astype(i32), n_used


def kernel(x, meta_tokens, norm1_g, w_in, b_igate, b_fgate, w_pool_mix, pool_scale, w_out, norm2_g, w_router,
           b_router, w_gu, b_gu, w_down, b_down, norm_f_g):
    B, S, D = x.shape
    L = N_META + S
    T = B * L
    H = MLSTM_HEADS
    BM_IN = 688
    BM_OUT = 688
    BM_NORM1 = 688
    BM_NORM = 688
    TB = 192
    assert T % BM_IN == 0 and T % BM_OUT == 0 and T % BM_NORM1 == 0 and T % BM_NORM == 0 and T % TB == 0
    assert w_in.shape[0] == 1
    assert L % N_META == 0 and TB % N_META == 0

    pieces = jnp.asarray(_piece_table(B, S))
    x2 = x.reshape(B * S, D)
    meta = meta_tokens.astype(x.dtype)

    l = 0
    w_in_t = w_in.reshape(D, PROJ_COLS + 2 * H).T
    bg = jnp.pad(jnp.concatenate([b_igate[l], b_fgate[l]]), (0, LANES - 2 * H)).reshape(1, LANES)
    n1, gates = _norm1(x2, meta, B, norm1_g[l].reshape(1, D), w_in_t, bg, BM_NORM1)

    proj3 = _inproj(n1, w_in_t, PROJ_COLS, BM_IN, 1024).reshape(B, L, PROJ_COLS)
    pool_out = _pool(proj3, w_pool_mix[l].astype(BF16), pool_scale[l].reshape(1, POOL_WIDTH))
    mlstm_out = _mlstm(proj3, gates.reshape(B, L, LANES))

    h1 = _outproj(pool_out.reshape(T, POOL_WIDTH), mlstm_out.reshape(T, MLSTM_WIDTH),
                  w_out.reshape(D, D), x2, meta, B, BM_OUT, 512)

    n2p, top_e, probs = _router(h1, norm2_g[l].reshape(1, D), w_router.reshape(D, N_EXPERTS).T,
                                b_router.reshape(1, N_EXPERTS), BM_NORM)

    row_tok, dest_km, sb_e, sb_rb0, sb_nb, n_used = _routing(top_e[:, :TOP_K], TB)
    y = _experts(sb_e, sb_rb0, sb_nb, n_used, row_tok, n2p,
                 w_gu.reshape(N_EXPERTS, D, 2 * D_FF), b_gu.reshape(N_EXPERTS, 1, 2 * D_FF),
                 w_down.reshape(N_EXPERTS, D_FF, D), b_down.reshape(N_EXPERTS, 1, D))
    out = _combine(dest_km, pieces, y, h1, probs, norm_f_g.reshape(1, D), TB, B * S)
    return out.reshape(B, S, D)
```

```python
import functools

import jax
import jax.numpy as jnp
import numpy as np
from jax import lax
from jax.experimental import pallas as pl
from jax.experimental.pallas import tpu as pltpu

D_MODEL = 4096
N_META = 16
POOL_WINDOWS = (2, 4, 8, 16)
POOL_WIDTH = D_MODEL // 4
POOL_GROUP = POOL_WIDTH // len(POOL_WINDOWS)
MLSTM_WIDTH = D_MODEL - POOL_WIDTH
MLSTM_HEADS = 6
MLSTM_V_DIM = MLSTM_WIDTH // MLSTM_HEADS
MLSTM_QK_DIM = MLSTM_V_DIM // 2
GATE_SOFTCAP = 15.0
N_EXPERTS = 32
TOP_K = 4
D_FF = D_MODEL // 2
SWIGLU_ALPHA = 1.702
SWIGLU_LIMIT = 7.0
MOE_BLOCK = 128
EPS = 1e-6

LANES = 128
SUBLANES = 8
SEQ_CHUNK = 256
PROJ_COLS = POOL_WIDTH + 2 * MLSTM_HEADS * MLSTM_QK_DIM + 2 * MLSTM_WIDTH
VMEM_LIMIT = 56 * 1024 * 1024
SB_BLOCKS = 10
CHUNK_SIZES = (6, 3, 2, 1)
CHUNK_BLOCKS = max(CHUNK_SIZES)
FF_TILE = 256
DOWN_TILE = 1024
N_FF_TILES = D_FF // FF_TILE
N_DOWN_TILES = D_MODEL // DOWN_TILE
Y_WORDS = DOWN_TILE // 2
GATHER_PER_VISIT = MOE_BLOCK // N_DOWN_TILES

F32 = jnp.float32
BF16 = jnp.bfloat16


def _params(sem, vmem=VMEM_LIMIT):
    return pltpu.CompilerParams(dimension_semantics=sem, vmem_limit_bytes=vmem)


def _split3(a):
    hi = a.astype(BF16)
    r1 = a - hi.astype(F32)
    mid = r1.astype(BF16)
    lo = (r1 - mid.astype(F32)).astype(BF16)
    return hi, mid, lo


def _dot(a, b):
    return jnp.dot(a, b, preferred_element_type=F32)


def _dotw(a, w):
    return lax.dot_general(a, w, (((1,), (0,)), ((), ())), preferred_element_type=F32)


def _dot_nt(a, wt):
    return lax.dot_general(a, wt, (((1,), (1,)), ((), ())), preferred_element_type=F32)


def _dot3_nt(a, wt):
    a_hi = a.astype(BF16)
    a_lo = (a - a_hi.astype(F32)).astype(BF16)
    w_hi = wt.astype(BF16)
    w_lo = (wt - w_hi.astype(F32)).astype(BF16)
    n = wt.shape[0]
    if n == LANES:
        both = _dot_nt(a_hi, jnp.concatenate([w_hi, w_lo], axis=0))
        return both[:, :n] + (both[:, n:] + _dot_nt(a_lo, w_hi))
    return _dot_nt(a_hi, w_hi) + (_dot_nt(a_hi, w_lo) + _dot_nt(a_lo, w_hi))


def _rms(x, g):
    return x * lax.rsqrt(jnp.mean(x * x, axis=-1, keepdims=True) + EPS) * g


GATE_ROWS = 16


def _piece_table(batch, seq):
    L = N_META + seq
    first = np.arange(batch * L // N_META) * N_META
    pos = first % L
    return np.where(pos >= N_META, (first // L) * seq + pos - N_META, -1).astype(np.int32)


def _token_block_start(blk, seq, x_hbm, meta_hbm, cols, dst, sem):
    bm = dst.shape[0]
    per_seq = (N_META + seq) // bm
    b = blk // per_seq
    r = blk - b * per_seq

    @pl.when(r == 0)
    def _():
        pltpu.make_async_copy(meta_hbm.at[:, cols], dst.at[pl.ds(0, N_META), :], sem).start()
        pltpu.make_async_copy(x_hbm.at[pl.ds(pl.multiple_of(b * seq, N_META), bm - N_META), cols],
                              dst.at[pl.ds(N_META, bm - N_META), :], sem).start()

    @pl.when(r > 0)
    def _():
        pltpu.make_async_copy(x_hbm.at[pl.ds(pl.multiple_of(b * seq + r * bm - N_META, N_META), bm), cols],
                              dst, sem).start()


def _token_block_wait(x_hbm, cols, dst, sem):
    pltpu.make_async_copy(x_hbm.at[pl.ds(0, dst.shape[0]), cols], dst, sem).wait()


def _norm1_kernel(seq, x_hbm, meta_hbm, g_ref, wg_ref, bg_ref, n_ref, gate_ref, hbuf, sem):
    i = pl.program_id(0)
    all_cols = pl.ds(0, D_MODEL)

    @pl.when(i == 0)
    def _():
        _token_block_start(0, seq, x_hbm, meta_hbm, all_cols, hbuf.at[0], sem.at[0])

    def step(slot):
        @pl.when(i + 1 < pl.num_programs(0))
        def _():
            _token_block_start(i + 1, seq, x_hbm, meta_hbm, all_cols, hbuf.at[1 - slot], sem.at[1 - slot])
        _token_block_wait(x_hbm, all_cols, hbuf.at[slot], sem.at[slot])
        _norm1_body(hbuf[slot], g_ref, wg_ref, bg_ref, n_ref, gate_ref)

    for slot in range(2):
        pl.when(lax.rem(i, 2) == slot)(functools.partial(step, slot))


def _norm1_body(h, g_ref, wg_ref, bg_ref, n_ref, gate_ref):
    y = _rms(h, g_ref[...])
    n_ref[...] = y.astype(BF16)
    sub = lax.broadcasted_iota(jnp.int32, (GATE_ROWS, D_MODEL), 0)
    wg = jnp.where(sub < 2 * MLSTM_HEADS, wg_ref[...], 0.0)
    wg = jnp.concatenate([wg, jnp.zeros((LANES - GATE_ROWS, D_MODEL), F32)], axis=0)
    gate_ref[...] = _dot3_nt(y, wg) + bg_ref[...]


def _norm1(x2, meta, batch, g, w_in_t, bg, bm):
    seq = x2.shape[0] // batch
    T = batch * (N_META + seq)
    assert (N_META + seq) % bm == 0
    return pl.pallas_call(
        functools.partial(_norm1_kernel, seq),
        grid=(T // bm,),
        in_specs=[
            pl.BlockSpec(memory_space=pl.ANY),
            pl.BlockSpec(memory_space=pl.ANY),
            pl.BlockSpec((1, D_MODEL), lambda i: (0, 0)),
            pl.BlockSpec((GATE_ROWS, D_MODEL), lambda i: (PROJ_COLS // GATE_ROWS, 0)),
            pl.BlockSpec((1, LANES), lambda i: (0, 0)),
        ],
        out_specs=[
            pl.BlockSpec((bm, D_MODEL), lambda i: (i, 0)),
            pl.BlockSpec((bm, LANES), lambda i: (i, 0)),
        ],
        out_shape=[
            jax.ShapeDtypeStruct((T, D_MODEL), BF16),
            jax.ShapeDtypeStruct((T, LANES), F32),
        ],
        scratch_shapes=[pltpu.VMEM((2, bm, D_MODEL), F32), pltpu.SemaphoreType.DMA((2,))],
        compiler_params=_params(("arbitrary",)),
        name="norm1_gates",
    )(x2, meta, g, w_in_t, bg)


def _inproj_kernel(x_ref, wt_ref, o_ref):
    o_ref[...] = _dot_nt(x_ref[...], wt_ref[...]).astype(o_ref.dtype)


def _inproj(x, wt, n_cols, bm, bn):
    M, K = x.shape
    return pl.pallas_call(
        _inproj_kernel,
        grid=(n_cols // bn, M // bm),
        in_specs=[
            pl.BlockSpec((bm, K), lambda j, i: (i, 0)),
            pl.BlockSpec((bn, K), lambda j, i: (j, 0)),
        ],
        out_specs=pl.BlockSpec((bm, bn), lambda j, i: (i, j)),
        out_shape=jax.ShapeDtypeStruct((M, n_cols), F32),
        compiler_params=_params(("parallel", "parallel")),
        name="in_proj",
    )(x, wt)


def _outproj_kernel(seq, p_ref, m_ref, w_ref, x_hbm, meta_hbm, o_ref, rbuf, sem):
    j = pl.program_id(0)
    i = pl.program_id(1)
    ni = pl.num_programs(1)
    bn = rbuf.shape[2]
    t = j * ni + i

    def fetch(step, slot):
        cols = pl.ds(pl.multiple_of((step // ni) * bn, bn), bn)
        _token_block_start(lax.rem(step, ni), seq, x_hbm, meta_hbm, cols, rbuf.at[slot], sem.at[slot])

    @pl.when(t == 0)
    def _():
        fetch(0, 0)

    def step(slot):
        @pl.when(t + 1 < pl.num_programs(0) * ni)
        def _():
            fetch(t + 1, 1 - slot)
        acc = _dotw(p_ref[...], w_ref[:POOL_WIDTH, :]) + _dotw(m_ref[...], w_ref[POOL_WIDTH:, :])
        _token_block_wait(x_hbm, pl.ds(0, bn), rbuf.at[slot], sem.at[slot])
        o_ref[...] = rbuf[slot] + acc

    for slot in range(2):
        pl.when(lax.rem(t, 2) == slot)(functools.partial(step, slot))


def _outproj(p, m, w, x2, meta, batch, bm, bn):
    M = p.shape[0]
    K, N = w.shape
    seq = x2.shape[0] // batch
    assert (N_META + seq) % bm == 0
    return pl.pallas_call(
        functools.partial(_outproj_kernel, seq),
        grid=(N // bn, M // bm),
        in_specs=[
            pl.BlockSpec((bm, POOL_WIDTH), lambda j, i: (i, 0)),
            pl.BlockSpec((bm, MLSTM_WIDTH), lambda j, i: (i, 0)),
            pl.BlockSpec((K, bn), lambda j, i: (0, j)),
            pl.BlockSpec(memory_space=pl.ANY),
            pl.BlockSpec(memory_space=pl.ANY),
        ],
        out_specs=pl.BlockSpec((bm, bn), lambda j, i: (i, j)),
        out_shape=jax.ShapeDtypeStruct((M, N), F32),
        scratch_shapes=[pltpu.VMEM((2, bm, bn), F32), pltpu.SemaphoreType.DMA((2,))],
        compiler_params=_params(("arbitrary", "arbitrary")),
        name="out_proj",
    )(p, m, w, x2, meta)


def _pool_kernel(u_ref, w_ref, s_ref, o_ref, carry_ref):
    c = pl.program_id(1)

    @pl.when(c == 0)
    def _():
        carry_ref[...] = jnp.zeros_like(carry_ref)

    u = u_ref[0]
    ext = jnp.concatenate([carry_ref[...], u], axis=0)
    carry_ref[...] = u[SEQ_CHUNK - 16:, :]
    pos = c * SEQ_CHUNK + lax.broadcasted_iota(jnp.int32, (SEQ_CHUNK, 1), 0)
    for g, win in enumerate(POOL_WINDOWS):
        cols = slice(g * POOL_GROUP, (g + 1) * POOL_GROUP)
        s = ext[:, cols]
        span = 1
        while span < win:
            s = s + pltpu.roll(s, span, axis=0)
            span *= 2
        cnt = jnp.minimum(pos + 1, win).astype(F32)
        d = s[16:, :] / cnt - u[:, cols]
        y = _dot(d.astype(BF16), w_ref[g]) * s_ref[:, cols]
        o_ref[0, :, cols] = y.astype(o_ref.dtype)


def _pool(proj3, w_mix, scale):
    B, L, _ = proj3.shape
    nc = pl.cdiv(L, SEQ_CHUNK)
    return pl.pallas_call(
        _pool_kernel,
        grid=(B, nc),
        in_specs=[
            pl.BlockSpec((1, SEQ_CHUNK, POOL_WIDTH), lambda b, c: (b, c, 0)),
            pl.BlockSpec((len(POOL_WINDOWS), POOL_GROUP, POOL_GROUP), lambda b, c: (0, 0, 0)),
            pl.BlockSpec((1, POOL_WIDTH), lambda b, c: (0, 0)),
        ],
        out_specs=pl.BlockSpec((1, SEQ_CHUNK, POOL_WIDTH), lambda b, c: (b, c, 0)),
        out_shape=jax.ShapeDtypeStruct((B, L, POOL_WIDTH), BF16),
        scratch_shapes=[pltpu.VMEM((16, POOL_WIDTH), F32)],
        compiler_params=_params(("parallel", "arbitrary")),
        name="pool_mixer",
    )(proj3, w_mix, scale)


def _soft_cap(a):
    return GATE_SOFTCAP * jnp.tanh(a / GATE_SOFTCAP)


def _log_sigmoid(a):
    return jnp.minimum(a, 0.0) - jnp.log1p(jnp.exp(-jnp.abs(a)))


HEADS_PER_STEP = 2


def _mlstm_kernel(seq_len, q_ref, k_ref, v_ref, o_ref, gate_ref, out_ref, c_ref, n_ref, m_ref):
    c = pl.program_id(2)
    Lc = SEQ_CHUNK
    dk, dv = MLSTM_QK_DIM, MLSTM_V_DIM

    @pl.when(c == 0)
    def _():
        c_ref[...] = jnp.zeros_like(c_ref)
        n_ref[...] = jnp.zeros_like(n_ref)
        m_ref[...] = jnp.zeros_like(m_ref)

    row = lax.broadcasted_iota(jnp.int32, (Lc, 1), 0)
    col = lax.broadcasted_iota(jnp.int32, (1, Lc), 1)
    ok_col = (c * Lc + row) < seq_len
    causal = col <= row
    tri = causal.astype(BF16)
    tri_t = (row <= col).astype(BF16)
    lane = lax.broadcasted_iota(jnp.int32, (1, LANES), 1)
    capped = _soft_cap(gate_ref[0])
    gc = jnp.where(ok_col, jnp.where(lane < MLSTM_HEADS, capped, _log_sigmoid(capped)), 0.0)
    gt = gc.T
    for i in range(HEADS_PER_STEP):
        _mlstm_head(pl.program_id(1) * HEADS_PER_STEP + i, ok_col, causal, tri, tri_t, gc, gt,
                    q_ref[0, :, i * dk:(i + 1) * dk], k_ref[0, :, i * dk:(i + 1) * dk],
                    v_ref[0, :, i * dv:(i + 1) * dv], o_ref[0, :, i * dv:(i + 1) * dv],
                    out_ref.at[0, :, i * dv:(i + 1) * dv], c_ref.at[i], n_ref.at[i], m_ref.at[i])


def _mlstm_head(h, ok_col, causal, tri, tri_t, gc, gt, q_in, k_in, v_in, o_in, out_ref, c_ref, n_ref, m_ref):
    Lc = SEQ_CHUNK
    q = jnp.where(ok_col, q_in, 0.0) * (MLSTM_QK_DIM ** -0.5)
    k = jnp.where(ok_col, k_in, 0.0)
    v = jnp.where(ok_col, v_in, 0.0)

    lane = lax.broadcasted_iota(jnp.int32, (1, LANES), 1)
    sub = lax.broadcasted_iota(jnp.int32, (LANES, 1), 0)
    i_c = jnp.sum(jnp.where(lane == h, gc, 0.0), axis=1, keepdims=True)
    f_c = jnp.sum(jnp.where(lane == h + MLSTM_HEADS, gc, 0.0), axis=1, keepdims=True)
    i_r = jnp.sum(jnp.where(sub == h, gt, 0.0), axis=0, keepdims=True)
    f_r = jnp.sum(jnp.where(sub == h + MLSTM_HEADS, gt, 0.0), axis=0, keepdims=True)

    fb_c = jnp.broadcast_to(f_c, (Lc, LANES))
    b_c = sum(_dot(tri, p) for p in _split3(fb_c))[:, 0:1]
    fb_r = jnp.broadcast_to(f_r, (8, Lc))
    b_r8 = sum(_dot(p, tri_t) for p in _split3(fb_r))
    b_r = b_r8[0:1, :]
    g_tot = b_r8[0:1, Lc - 1:Lc]

    m_prev = m_ref[...]
    dlog = jnp.where(causal, b_c - b_r + i_r, -jnp.inf)
    inter_log = b_c + m_prev
    m_out = jnp.maximum(inter_log, jnp.max(dlog, axis=1, keepdims=True))
    wts = jnp.exp(dlog - m_out)
    inter_w = jnp.exp(inter_log - m_out)

    qb = q.astype(BF16)
    kb = k.astype(BF16)
    vb = v.astype(BF16)
    s = lax.dot_general(qb, kb, (((1,), (1,)), ((), ())), preferred_element_type=F32) * wts
    num = _dot(s.astype(BF16), vb) + inter_w * _dot(qb, c_ref[...].astype(BF16))
    den = jnp.sum(s, axis=1, keepdims=True) + inter_w * jnp.sum(q * n_ref[...], axis=1, keepdims=True)
    hh = num / jnp.maximum(jnp.abs(den), jnp.exp(-m_out))
    out_ref[...] = (jax.nn.sigmoid(o_in) * hh).astype(out_ref.dtype)

    a_c = g_tot - b_c + i_c
    m_new = jnp.maximum(g_tot + m_prev, jnp.max(a_c, axis=0, keepdims=True))
    wk = k * jnp.exp(a_c - m_new)
    decay = jnp.exp(g_tot + m_prev - m_new)
    c_ref[...] = decay * c_ref[...] + lax.dot_general(
        wk.astype(BF16), vb, (((0,), (0,)), ((), ())), preferred_element_type=F32)
    n_ref[...] = decay * n_ref[...] + jnp.sum(wk, axis=0, keepdims=True)
    m_ref[...] = m_new


def _mlstm(proj3, gates3):
    B, L, _ = proj3.shape
    nc = pl.cdiv(L, SEQ_CHUNK)
    G = HEADS_PER_STEP
    dk, dv, H = G * MLSTM_QK_DIM, G * MLSTM_V_DIM, MLSTM_HEADS // G
    q0 = POOL_WIDTH // dk
    k0 = q0 + H
    v0 = (POOL_WIDTH + 2 * MLSTM_HEADS * MLSTM_QK_DIM) // dv
    o0 = v0 + H
    return pl.pallas_call(
        functools.partial(_mlstm_kernel, L),
        grid=(B, H, nc),
        in_specs=[
            pl.BlockSpec((1, SEQ_CHUNK, dk), lambda b, h, c: (b, c, q0 + h)),
            pl.BlockSpec((1, SEQ_CHUNK, dk), lambda b, h, c: (b, c, k0 + h)),
            pl.BlockSpec((1, SEQ_CHUNK, dv), lambda b, h, c: (b, c, v0 + h)),
            pl.BlockSpec((1, SEQ_CHUNK, dv), lambda b, h, c: (b, c, o0 + h)),
            pl.BlockSpec((1, SEQ_CHUNK, LANES), lambda b, h, c: (b, c, 0)),
        ],
        out_specs=pl.BlockSpec((1, SEQ_CHUNK, dv), lambda b, h, c: (b, c, h)),
        out_shape=jax.ShapeDtypeStruct((B, L, MLSTM_WIDTH), BF16),
        scratch_shapes=[
            pltpu.VMEM((G, MLSTM_QK_DIM, MLSTM_V_DIM), F32),
            pltpu.VMEM((G, 1, MLSTM_QK_DIM), F32),
            pltpu.VMEM((G, 1, 1), F32),
        ],
        compiler_params=_params(("parallel", "parallel", "arbitrary")),
        name="mlstm",
    )(proj3, proj3, proj3, proj3, gates3)


def _pack_bf16_pair(lo, hi):
    lo_b = lax.bitcast_convert_type(lo.astype(BF16).astype(F32), jnp.uint32)
    hi_b = lax.bitcast_convert_type(hi.astype(BF16).astype(F32), jnp.uint32)
    return (lo_b >> 16) | hi_b


def _unpack_bf16_pair(w):
    lo = lax.bitcast_convert_type(w << 16, F32).astype(BF16)
    hi = lax.bitcast_convert_type(w & jnp.uint32(0xFFFF0000), F32).astype(BF16)
    return lo, hi


def _router_kernel(h_ref, g_ref, wr_ref, br_ref, n_ref, e_ref, p_ref):
    y = _rms(h_ref[...], g_ref[...])
    half = D_MODEL // 2
    n_ref[...] = _pack_bf16_pair(y[:, :half], y[:, half:])
    logits = _dot3_nt(y, wr_ref[...]) + br_ref[...]
    lane = lax.broadcasted_iota(jnp.int32, logits.shape, 1).astype(F32)
    l = jnp.where(lane < N_EXPERTS, logits, -jnp.inf)
    vals, idxs = [], []
    for _ in range(TOP_K):
        m = jnp.max(l, axis=1, keepdims=True)
        idx = jnp.min(jnp.where(l == m, lane, float(LANES)), axis=1, keepdims=True)
        vals.append(m)
        idxs.append(idx)
        l = jnp.where(lane == idx, -jnp.inf, l)
    ex = [jnp.exp(v - vals[0]) for v in vals]
    tot = ex[0] + ex[1] + ex[2] + ex[3]
    e_out = jnp.zeros(logits.shape, F32)
    p_out = jnp.zeros(logits.shape, F32)
    for kk in range(TOP_K):
        e_out = jnp.where(lane == kk, idxs[kk], e_out)
        p_out = jnp.where(lane == kk, ex[kk] / tot, p_out)
    e_ref[...] = e_out.astype(jnp.int32)
    p_ref[...] = p_out


def _router(h, g, wr, br, bm):
    T = h.shape[0]
    return pl.pallas_call(
        _router_kernel,
        grid=(T // bm,),
        in_specs=[
            pl.BlockSpec((bm, D_MODEL), lambda i: (i, 0)),
            pl.BlockSpec((1, D_MODEL), lambda i: (0, 0)),
            pl.BlockSpec((N_EXPERTS, D_MODEL), lambda i: (0, 0)),
            pl.BlockSpec((1, N_EXPERTS), lambda i: (0, 0)),
        ],
        out_specs=[
            pl.BlockSpec((bm, D_MODEL // 2), lambda i: (i, 0)),
            pl.BlockSpec((bm, N_EXPERTS), lambda i: (i, 0)),
            pl.BlockSpec((bm, N_EXPERTS), lambda i: (i, 0)),
        ],
        out_shape=[
            jax.ShapeDtypeStruct((T, D_MODEL // 2), jnp.uint32),
            jax.ShapeDtypeStruct((T, N_EXPERTS), jnp.int32),
            jax.ShapeDtypeStruct((T, N_EXPERTS), F32),
        ],
        compiler_params=_params(("parallel",)),
        name="norm2_router",
    )(h, g, wr, br)


def _experts_kernel(sbe_ref, rb0_ref, nbk_ref, used_ref, tok_ref,
                    x_hbm, wgu_hbm, wd_hbm, bgu_ref, bd_ref,
                    y_hbm,
                    x_res, hmid, wgl_buf, wd_buf, ystage, gsem, wsem, dsem, ysem, ypend, ycnt, gcnt):
    s = pl.program_id(0)
    ns = pl.num_programs(0)
    nb = nbk_ref[s]
    expert = sbe_ref[s]
    nxt = jnp.minimum(s + 1, ns - 1)
    next_active = jnp.logical_and(s + 1 < ns, nbk_ref[nxt] > 0)
    half = D_MODEL // 2
    n_blocks_total = y_hbm.shape[0] // MOE_BLOCK

    def gateup_copies(e, j, slot):
        gate_cols = pl.ds(pl.multiple_of(j * FF_TILE, FF_TILE), FF_TILE)
        lin_cols = pl.ds(pl.multiple_of(D_FF + j * FF_TILE, FF_TILE), FF_TILE)
        return (pltpu.make_async_copy(wgu_hbm.at[e, :, gate_cols], wgl_buf.at[slot, 0], wsem.at[slot]),
                pltpu.make_async_copy(wgu_hbm.at[e, :, lin_cols], wgl_buf.at[slot, 1], wsem.at[slot]))

    def down_copy(e, j, slot):
        cols = pl.ds(pl.multiple_of(j * DOWN_TILE, DOWN_TILE), DOWN_TILE)
        return pltpu.make_async_copy(wd_hbm.at[e, :, cols], wd_buf.at[slot], dsem.at[slot])

    def x_row_copy(tok, r8, rs):
        return pltpu.make_async_copy(x_hbm.at[tok >> 3, pl.ds(tok & 7, 1), :], x_res.at[r8, pl.ds(rs, 1), :], gsem.at[0])

    def x_block_wait():
        g = MOE_BLOCK // SUBLANES
        pltpu.make_async_copy(x_hbm.at[pl.ds(0, g)], x_res.at[pl.ds(0, g)], gsem.at[0]).wait()

    def gather_upto(sb, n_rows):
        base = rb0_ref[sb] * MOE_BLOCK

        def body(r, carry):
            x_row_copy(tok_ref[base + r], r >> 3, r & 7).start()
            return carry
        lax.fori_loop(gcnt[0], n_rows, body, 0)
        gcnt[0] = jnp.maximum(gcnt[0], n_rows)

    def gather_some(sb, n):
        base = rb0_ref[sb] * MOE_BLOCK
        cur = gcnt[0]
        cur8 = cur >> 3
        for i in range(n):
            x_row_copy(tok_ref[base + cur + i], cur8 + i // SUBLANES, i % SUBLANES).start()
        gcnt[0] = cur + n

    def gather_wait():
        n = gcnt[0]
        n_full = n // MOE_BLOCK

        def block_body(r, carry):
            x_block_wait()
            return carry

        def row_body(r, carry):
            x_row_copy(0, 0, 0).wait()
            return carry
        lax.fori_loop(0, n_full, block_body, 0)
        lax.fori_loop(n_full * MOE_BLOCK, n, row_body, 0)
        gcnt[0] = 0

    def y_copy(slot, piece, row0, word0):
        return pltpu.make_async_copy(
            ystage.at[slot, pl.ds(piece * MOE_BLOCK, MOE_BLOCK), :],
            y_hbm.at[pl.ds(pl.multiple_of(row0, MOE_BLOCK), MOE_BLOCK), pl.ds(pl.multiple_of(word0, Y_WORDS), Y_WORDS)],
            ysem.at[slot])

    def y_drain(slot):
        for piece in range(CHUNK_BLOCKS):
            @pl.when(ypend[slot] > piece)
            def _():
                y_copy(slot, 0, 0, 0).wait()
        ypend[slot] = 0

    def y_emit(val, n_pieces, slot, row0, word0):
        ystage[slot, pl.ds(0, n_pieces * MOE_BLOCK), :] = _pack_bf16_pair(val[:, :Y_WORDS], val[:, Y_WORDS:])
        for piece in range(n_pieces):
            y_copy(slot, piece, row0 + piece * MOE_BLOCK, word0).start()
        ypend[slot] = n_pieces

    def for_chunks(n_blocks, fn):
        done = 0
        for pos, k in enumerate(CHUNK_SIZES):
            if pos == 0:
                n_big = n_blocks // k

                def body(c, carry, k=k):
                    fn(c * k, k)
                    return carry
                lax.fori_loop(0, n_big, body, 0)
                done = n_big * k
            else:
                fits = n_blocks - done >= k

                @pl.when(fits)
                def _(k=k, done=done):
                    fn(done, k)
                done = jnp.where(fits, done + k, done)

    @pl.when(s == 0)
    def _():
        ypend[0] = 0
        ypend[1] = 0
        ycnt[0] = 0
        gcnt[0] = 0

        @pl.when(nb > 0)
        def _():
            for c in gateup_copies(expert, 0, 0):
                c.start()
        gather_upto(0, nb * MOE_BLOCK)

    @pl.when(nb > 0)
    def _():
        gather_wait()
        row_base = rb0_ref[s] * MOE_BLOCK

        def gateup_tile(j, carry):
            slot = lax.rem(j, 2)
            for c in gateup_copies(expert, j, slot):
                c.wait()

            @pl.when(j + 1 < N_FF_TILES)
            def _():
                for c in gateup_copies(expert, j + 1, 1 - slot):
                    c.start()

            @pl.when(j + 1 == N_FF_TILES)
            def _():
                down_copy(expert, 0, 0).start()

            bg = bgu_ref[0, pl.ds(j, 1), :]
            bl = bgu_ref[0, pl.ds(N_FF_TILES + j, 1), :]

            def chunk(b0, k):
                rows = pl.ds(pl.multiple_of(b0 * MOE_BLOCK, MOE_BLOCK), k * MOE_BLOCK)
                groups = pl.ds(pl.multiple_of(b0 * (MOE_BLOCK // SUBLANES), MOE_BLOCK // SUBLANES),
                               k * (MOE_BLOCK // SUBLANES))
                x_lo, x_hi = _unpack_bf16_pair(x_res[groups].reshape(k * MOE_BLOCK, half))
                gate = _dotw(x_lo, wgl_buf[slot, 0, :half, :]) + _dotw(x_hi, wgl_buf[slot, 0, half:, :]) + bg
                lin = _dotw(x_lo, wgl_buf[slot, 1, :half, :]) + _dotw(x_hi, wgl_buf[slot, 1, half:, :]) + bl
                gate = jnp.minimum(gate, SWIGLU_LIMIT)
                lin = jnp.clip(lin, -SWIGLU_LIMIT, SWIGLU_LIMIT)
                hmid[j, rows, :] = (gate * jax.nn.sigmoid(SWIGLU_ALPHA * gate) * (lin + 1.0)).astype(BF16)
            for_chunks(nb, chunk)
            return carry
        lax.fori_loop(0, N_FF_TILES, gateup_tile, 0)

        def down_tile(j, carry):
            slot = lax.rem(j, 2)
            down_copy(expert, j, slot).wait()

            @pl.when(j + 1 < N_DOWN_TILES)
            def _():
                down_copy(expert, j + 1, 1 - slot).start()

            @pl.when(jnp.logical_and(j + 1 == N_DOWN_TILES, next_active))
            def _():
                for c in gateup_copies(sbe_ref[nxt], 0, 0):
                    c.start()

            bd = bd_ref[0, pl.ds(j, 1), :]
            word0 = j * Y_WORDS

            def chunk(b0, k):
                yslot = lax.rem(ycnt[0], 2)
                ycnt[0] = ycnt[0] + 1
                y_drain(yslot)
                gather_some(nxt, k * GATHER_PER_VISIT)
                rows = pl.ds(pl.multiple_of(b0 * MOE_BLOCK, MOE_BLOCK), k * MOE_BLOCK)
                h = jnp.concatenate([hmid[jj, rows, :] for jj in range(N_FF_TILES)], axis=1)
                acc = bd + _dotw(h, wd_buf[slot])
                y_emit(acc, k, yslot, row_base + b0 * MOE_BLOCK, word0)
            for_chunks(nb, chunk)
            return carry
        lax.fori_loop(0, N_DOWN_TILES, down_tile, 0)
        gather_upto(nxt, jnp.where(s + 1 < ns, nbk_ref[nxt] * MOE_BLOCK, 0))

    @pl.when(s == ns - 1)
    def _():
        gather_wait()
        y_drain(0)
        y_drain(1)
        ystage[0, pl.ds(0, MOE_BLOCK), :] = jnp.zeros((MOE_BLOCK, Y_WORDS), jnp.uint32)

        def zero_copy(b, j):
            return y_copy(0, 0, b * MOE_BLOCK, j * Y_WORDS)

        def start_body(b, carry):
            for j in range(N_DOWN_TILES):
                zero_copy(b, j).start()
            return carry

        def wait_body(b, carry):
            for j in range(N_DOWN_TILES):
                zero_copy(0, 0).wait()
            return carry
        lax.fori_loop(used_ref[0], n_blocks_total, start_body, 0)
        lax.fori_loop(used_ref[0], n_blocks_total, wait_body, 0)


def _experts(sb_e, sb_rb0, sb_nb, n_used, row_tok, n2p, w_gu, b_gu, w_down, b_down):
    R = row_tok.shape[0] - SB_BLOCKS * MOE_BLOCK
    S = sb_e.shape[0]
    sb_rows = SB_BLOCKS * MOE_BLOCK

    grid_spec = pltpu.PrefetchScalarGridSpec(
        num_scalar_prefetch=5,
        grid=(S,),
        in_specs=[
            pl.BlockSpec(memory_space=pl.ANY),
            pl.BlockSpec(memory_space=pl.ANY),
            pl.BlockSpec(memory_space=pl.ANY),
            pl.BlockSpec((1, 2 * N_FF_TILES, FF_TILE), lambda s, e, r0, nbk, u, tok: (e[s], 0, 0)),
            pl.BlockSpec((1, N_DOWN_TILES, DOWN_TILE), lambda s, e, r0, nbk, u, tok: (e[s], 0, 0)),
        ],
        out_specs=pl.BlockSpec(memory_space=pl.ANY),
        scratch_shapes=[
            pltpu.VMEM((sb_rows // SUBLANES, SUBLANES, D_MODEL // 2), jnp.uint32),
            pltpu.VMEM((N_FF_TILES, sb_rows, FF_TILE), BF16),
            pltpu.VMEM((2, 2, D_MODEL, FF_TILE), F32),
            pltpu.VMEM((2, D_FF, DOWN_TILE), F32),
            pltpu.VMEM((2, CHUNK_BLOCKS * MOE_BLOCK, Y_WORDS), jnp.uint32),
            pltpu.SemaphoreType.DMA((1,)),
            pltpu.SemaphoreType.DMA((2,)),
            pltpu.SemaphoreType.DMA((2,)),
            pltpu.SemaphoreType.DMA((2,)),
            pltpu.SMEM((2,), jnp.int32),
            pltpu.SMEM((1,), jnp.int32),
            pltpu.SMEM((1,), jnp.int32),
        ],
    )
    return pl.pallas_call(
        _experts_kernel,
        grid_spec=grid_spec,
        out_shape=jax.ShapeDtypeStruct((R, D_MODEL // 2), jnp.uint32),
        compiler_params=_params(("arbitrary",)),
        name="moe_experts",
    )(sb_e, sb_rb0, sb_nb, n_used, row_tok, n2p.reshape(-1, SUBLANES, D_MODEL // 2), w_gu, w_down,
      b_gu.reshape(N_EXPERTS, 2 * N_FF_TILES, FF_TILE), b_down.reshape(N_EXPERTS, N_DOWN_TILES, DOWN_TILE))


def _combine_kernel(tb, dest_ref, piece_ref, y_hbm, h_ref, p_ref, g_ref, out_hbm, buf0, buf1, ostage, sem, osem,
                    opend):
    i = pl.program_id(0)
    nb = pl.num_programs(0)
    rows = tb * TOP_K
    n_pieces = tb // N_META
    bufs = (buf0, buf1)

    def gather_start(step, slot, first, n):
        for r in range(first, first + n):
            row = dest_ref[step * rows + r]
            pltpu.make_async_copy(y_hbm.at[pl.ds(row, 1), :], bufs[slot].at[pl.ds(r, 1), :], sem.at[slot]).start()

    def gather_wait(slot):
        pltpu.make_async_copy(y_hbm.at[pl.ds(0, rows), :], bufs[slot], sem.at[slot]).wait()

    def out_copy(slot, piece, dst):
        return pltpu.make_async_copy(ostage.at[slot, pl.ds(piece * N_META, N_META), :],
                                     out_hbm.at[pl.ds(pl.multiple_of(dst, N_META), N_META), :], osem.at[slot])

    def out_drain(slot):
        for piece in range(n_pieces):
            @pl.when(opend[slot] > piece)
            def _():
                out_copy(slot, 0, 0).wait()
        opend[slot] = 0

    @pl.when(i == 0)
    def _():
        opend[0] = 0
        opend[1] = 0
        gather_start(0, 0, 0, rows)

    def step(slot):
        gather_wait(slot)
        nxt = jnp.minimum(i + 1, nb - 1)
        acc = h_ref[...]
        p = p_ref[...]
        for kk in range(TOP_K):
            gather_start(nxt, 1 - slot, kk * tb, tb)
            w = bufs[slot][pl.ds(kk * tb, tb), :]
            lo = lax.bitcast_convert_type(w << 16, F32)
            hi = lax.bitcast_convert_type(w & jnp.uint32(0xFFFF0000), F32)
            pieces = []
            for j in range(N_DOWN_TILES):
                pieces += [lo[:, j * Y_WORDS:(j + 1) * Y_WORDS], hi[:, j * Y_WORDS:(j + 1) * Y_WORDS]]
            acc = acc + p[:, kk:kk + 1] * jnp.concatenate(pieces, axis=1)
        out_drain(slot)
        ostage[slot] = _rms(acc, g_ref[...])
        started = 0
        for piece in range(n_pieces):
            dst = piece_ref[i * n_pieces + piece]

            @pl.when(dst >= 0)
            def _():
                out_copy(slot, piece, dst).start()
            started = started + (dst >= 0).astype(jnp.int32)
        opend[slot] = started

    for slot in range(2):
        pl.when(lax.rem(i, 2) == slot)(functools.partial(step, slot))

    @pl.when(i == nb - 1)
    def _():
        for slot in range(2):
            @pl.when(lax.rem(i, 2) == slot)
            def _():
                gather_wait(1 - slot)
        out_drain(0)
        out_drain(1)


def _combine(dest_km, pieces, y, h, probs, g, tb, n_out):
    T = h.shape[0]
    grid_spec = pltpu.PrefetchScalarGridSpec(
        num_scalar_prefetch=2,
        grid=(T // tb,),
        in_specs=[
            pl.BlockSpec(memory_space=pl.ANY),
            pl.BlockSpec((tb, D_MODEL), lambda i, d, q: (i, 0)),
            pl.BlockSpec((tb, N_EXPERTS), lambda i, d, q: (i, 0)),
            pl.BlockSpec((1, D_MODEL), lambda i, d, q: (0, 0)),
        ],
        out_specs=pl.BlockSpec(memory_space=pl.ANY),
        scratch_shapes=[
            pltpu.VMEM((tb * TOP_K, D_MODEL // 2), jnp.uint32),
            pltpu.VMEM((tb * TOP_K, D_MODEL // 2), jnp.uint32),
            pltpu.VMEM((2, tb, D_MODEL), F32),
            pltpu.SemaphoreType.DMA((2,)),
            pltpu.SemaphoreType.DMA((2,)),
            pltpu.SMEM((2,), jnp.int32),
        ],
    )
    return pl.pallas_call(
        functools.partial(_combine_kernel, tb),
        grid_spec=grid_spec,
        out_shape=jax.ShapeDtypeStruct((n_out, D_MODEL), F32),
        compiler_params=_params(("arbitrary",)),
        name="moe_combine_norm",
    )(dest_km, pieces, y, h, probs, g)


def _routing(top_e, tb):
    T = top_e.shape[0]
    A = T * TOP_K
    n_blocks = -(-(A + N_EXPERTS * (MOE_BLOCK - 1)) // MOE_BLOCK)
    R = n_blocks * MOE_BLOCK
    n_sb = N_EXPERTS + n_blocks // SB_BLOCKS
    i32 = jnp.int32
    e_flat = top_e.reshape(A)
    onehot = (e_flat[:, None] == jnp.arange(N_EXPERTS, dtype=i32)[None, :]).astype(i32)
    csum = jnp.cumsum(onehot, axis=0)
    rank = jnp.sum(csum * onehot, axis=1) - 1
    counts = csum[-1]
    blocks = (counts + MOE_BLOCK - 1) // MOE_BLOCK
    blk_end = jnp.cumsum(blocks)
    blk_start = blk_end - blocks
    dest = (blk_start * MOE_BLOCK)[e_flat] + rank
    row_tok = jnp.zeros((R + SB_BLOCKS * MOE_BLOCK,), i32).at[dest].set(
        jnp.arange(A, dtype=i32) // TOP_K, unique_indices=True)
    dest_km = dest.reshape(T // tb, tb, TOP_K).transpose(0, 2, 1).reshape(A).astype(i32)

    sbs = (blocks + SB_BLOCKS - 1) // SB_BLOCKS
    sb_end = jnp.cumsum(sbs)
    sb_start = sb_end - sbs
    s_idx = jnp.arange(n_sb, dtype=i32)
    active = s_idx < sb_end[-1]
    e_s = jnp.minimum(jnp.sum((sb_end[None, :] <= s_idx[:, None]).astype(i32), axis=1), N_EXPERTS - 1)
    k_s = s_idx - sb_start[e_s]
    sb_rb0 = jnp.where(active, blk_start[e_s] + k_s * SB_BLOCKS, 0)
    sb_nb = jnp.where(active, jnp.minimum(blocks[e_s] - k_s * SB_BLOCKS, SB_BLOCKS), 0)
    sb_e = jnp.where(active, e_s, jnp.max(jnp.where(active, e_s, 0)))
    n_used = blk_end[-1:].astype(i32)
    return row_tok, dest_km, sb_e.astype(i32), sb_rb0.astype(i32), sb_nb.astype(i32), n_used


def kernel(x, meta_tokens, norm1_g, w_in, b_igate, b_fgate, w_pool_mix, pool_scale, w_out, norm2_g, w_router,
           b_router, w_gu, b_gu, w_down, b_down, norm_f_g):
    B, S, D = x.shape
    L = N_META + S
    T = B * L
    H = MLSTM_HEADS
    BM_IN = 688
    BM_OUT = 688
    BM_NORM1 = 688
    BM_NORM = 688
    TB = 192
    assert T % BM_IN == 0 and T % BM_OUT == 0 and T % BM_NORM1 == 0 and T % BM_NORM == 0 and T % TB == 0
    assert w_in.shape[0] == 1
    assert L % N_META == 0 and TB % N_META == 0

    pieces = jnp.asarray(_piece_table(B, S))
    x2 = x.reshape(B * S, D)
    meta = meta_tokens.astype(x.dtype)

    l = 0
    w_in_t = w_in.reshape(D, PROJ_COLS + 2 * H).T
    bg = jnp.pad(jnp.concatenate([b_igate[l], b_fgate[l]]), (0, LANES - 2 * H)).reshape(1, LANES)
    n1, gates = _norm1(x2, meta, B, norm1_g[l].reshape(1, D), w_in_t, bg, BM_NORM1)

    proj3 = _inproj(n1, w_in_t, PROJ_COLS, BM_IN, 1024).reshape(B, L, PROJ_COLS)
    pool_out = _pool(proj3, w_pool_mix[l].astype(BF16), pool_scale[l].reshape(1, POOL_WIDTH))
    mlstm_out = _mlstm(proj3, gates.reshape(B, L, LANES))

    h1 = _outproj(pool_out.reshape(T, POOL_WIDTH), mlstm_out.reshape(T, MLSTM_WIDTH),
                  w_out.reshape(D, D), x2, meta, B, BM_OUT, 512)

    n2p, top_e, probs = _router(h1, norm2_g[l].reshape(1, D), w_router.reshape(D, N_EXPERTS).T,
                                b_router.reshape(1, N_EXPERTS), BM_NORM)

    row_tok, dest_km, sb_e, sb_rb0, sb_nb, n_used = _routing(top_e[:, :TOP_K], TB)
    y = _experts(sb_e, sb_rb0, sb_nb, n_used, row_tok, n2p,
                 w_gu.reshape(N_EXPERTS, D, 2 * D_FF), b_gu.reshape(N_EXPERTS, 1, 2 * D_FF),
                 w_down.reshape(N_EXPERTS, D_FF, D), b_down.reshape(N_EXPERTS, 1, D))
    out = _combine(dest_km, pieces, y, h1, probs, norm_f_g.reshape(1, D), TB, B * S)
    return out.reshape(B, S, D)
```

```python
import functools

import jax
import jax.numpy as jnp
import numpy as np
from jax import lax
from jax.experimental import pallas as pl
from jax.experimental.pallas import tpu as pltpu

D_MODEL = 4096
N_META = 16
POOL_WINDOWS = (2, 4, 8, 16)
POOL_WIDTH = D_MODEL // 4
POOL_GROUP = POOL_WIDTH // len(POOL_WINDOWS)
MLSTM_WIDTH = D_MODEL - POOL_WIDTH
MLSTM_HEADS = 6
MLSTM_V_DIM = MLSTM_WIDTH // MLSTM_HEADS
MLSTM_QK_DIM = MLSTM_V_DIM // 2
GATE_SOFTCAP = 15.0
N_EXPERTS = 32
TOP_K = 4
D_FF = D_MODEL // 2
SWIGLU_ALPHA = 1.702
SWIGLU_LIMIT = 7.0
MOE_BLOCK = 128
EPS = 1e-6

LANES = 128
SUBLANES = 8
SEQ_CHUNK = 256
PROJ_COLS = POOL_WIDTH + 2 * MLSTM_HEADS * MLSTM_QK_DIM + 2 * MLSTM_WIDTH
VMEM_LIMIT = 56 * 1024 * 1024
SB_BLOCKS = 10
CHUNK_SIZES = (6, 3, 2, 1)
CHUNK_BLOCKS = max(CHUNK_SIZES)
FF_TILE = 256
DOWN_TILE = 1024
N_FF_TILES = D_FF // FF_TILE
N_DOWN_TILES = D_MODEL // DOWN_TILE
Y_WORDS = DOWN_TILE // 2
GATHER_TILES = N_DOWN_TILES // 2
GATHER_PER_VISIT = MOE_BLOCK // GATHER_TILES

F32 = jnp.float32
BF16 = jnp.bfloat16


def _params(sem, vmem=VMEM_LIMIT):
    return pltpu.CompilerParams(dimension_semantics=sem, vmem_limit_bytes=vmem)


def _split3(a):
    hi = a.astype(BF16)
    r1 = a - hi.astype(F32)
    mid = r1.astype(BF16)
    lo = (r1 - mid.astype(F32)).astype(BF16)
    return hi, mid, lo


def _dot(a, b):
    return jnp.dot(a, b, preferred_element_type=F32)


def _dotw(a, w):
    return lax.dot_general(a, w, (((1,), (0,)), ((), ())), preferred_element_type=F32)


def _dot_nt(a, wt):
    return lax.dot_general(a, wt, (((1,), (1,)), ((), ())), preferred_element_type=F32)


def _dot3_nt(a, wt):
    a_hi = a.astype(BF16)
    a_lo = (a - a_hi.astype(F32)).astype(BF16)
    w_hi = wt.astype(BF16)
    w_lo = (wt - w_hi.astype(F32)).astype(BF16)
    n = wt.shape[0]
    if n == LANES:
        both = _dot_nt(a_hi, jnp.concatenate([w_hi, w_lo], axis=0))
        return both[:, :n] + (both[:, n:] + _dot_nt(a_lo, w_hi))
    return _dot_nt(a_hi, w_hi) + (_dot_nt(a_hi, w_lo) + _dot_nt(a_lo, w_hi))


def _rms(x, g):
    return x * lax.rsqrt(jnp.mean(x * x, axis=-1, keepdims=True) + EPS) * g


GATE_ROWS = 16


def _piece_table(batch, seq):
    L = N_META + seq
    first = np.arange(batch * L // N_META) * N_META
    pos = first % L
    return np.where(pos >= N_META, (first // L) * seq + pos - N_META, -1).astype(np.int32)


def _token_block_start(blk, seq, x_hbm, meta_hbm, cols, dst, sem):
    bm = dst.shape[0]
    per_seq = (N_META + seq) // bm
    b = blk // per_seq
    r = blk - b * per_seq

    @pl.when(r == 0)
    def _():
        pltpu.make_async_copy(meta_hbm.at[:, cols], dst.at[pl.ds(0, N_META), :], sem).start()
        pltpu.make_async_copy(x_hbm.at[pl.ds(pl.multiple_of(b * seq, N_META), bm - N_META), cols],
                              dst.at[pl.ds(N_META, bm - N_META), :], sem).start()

    @pl.when(r > 0)
    def _():
        pltpu.make_async_copy(x_hbm.at[pl.ds(pl.multiple_of(b * seq + r * bm - N_META, N_META), bm), cols],
                              dst, sem).start()


def _token_block_wait(x_hbm, cols, dst, sem):
    pltpu.make_async_copy(x_hbm.at[pl.ds(0, dst.shape[0]), cols], dst, sem).wait()


def _norm1_kernel(seq, x_hbm, meta_hbm, g_ref, wg_ref, bg_ref, n_ref, gate_ref, hbuf, sem):
    i = pl.program_id(0)
    all_cols = pl.ds(0, D_MODEL)

    @pl.when(i == 0)
    def _():
        _token_block_start(0, seq, x_hbm, meta_hbm, all_cols, hbuf.at[0], sem.at[0])

    def step(slot):
        @pl.when(i + 1 < pl.num_programs(0))
        def _():
            _token_block_start(i + 1, seq, x_hbm, meta_hbm, all_cols, hbuf.at[1 - slot], sem.at[1 - slot])
        _token_block_wait(x_hbm, all_cols, hbuf.at[slot], sem.at[slot])
        _norm1_body(hbuf[slot], g_ref, wg_ref, bg_ref, n_ref, gate_ref)

    for slot in range(2):
        pl.when(lax.rem(i, 2) == slot)(functools.partial(step, slot))


def _norm1_body(h, g_ref, wg_ref, bg_ref, n_ref, gate_ref):
    y = _rms(h, g_ref[...])
    n_ref[...] = y.astype(BF16)
    sub = lax.broadcasted_iota(jnp.int32, (GATE_ROWS, D_MODEL), 0)
    wg = jnp.where(sub < 2 * MLSTM_HEADS, wg_ref[...], 0.0)
    wg = jnp.concatenate([wg, jnp.zeros((LANES - GATE_ROWS, D_MODEL), F32)], axis=0)
    gate_ref[...] = _dot3_nt(y, wg) + bg_ref[...]


def _norm1(x2, meta, batch, g, w_in_t, bg, bm):
    seq = x2.shape[0] // batch
    T = batch * (N_META + seq)
    assert (N_META + seq) % bm == 0
    return pl.pallas_call(
        functools.partial(_norm1_kernel, seq),
        grid=(T // bm,),
        in_specs=[
            pl.BlockSpec(memory_space=pl.ANY),
            pl.BlockSpec(memory_space=pl.ANY),
            pl.BlockSpec((1, D_MODEL), lambda i: (0, 0)),
            pl.BlockSpec((GATE_ROWS, D_MODEL), lambda i: (PROJ_COLS // GATE_ROWS, 0)),
            pl.BlockSpec((1, LANES), lambda i: (0, 0)),
        ],
        out_specs=[
            pl.BlockSpec((bm, D_MODEL), lambda i: (i, 0)),
            pl.BlockSpec((bm, LANES), lambda i: (i, 0)),
        ],
        out_shape=[
            jax.ShapeDtypeStruct((T, D_MODEL), BF16),
            jax.ShapeDtypeStruct((T, LANES), F32),
        ],
        scratch_shapes=[pltpu.VMEM((2, bm, D_MODEL), F32), pltpu.SemaphoreType.DMA((2,))],
        compiler_params=_params(("arbitrary",)),
        name="norm1_gates",
    )(x2, meta, g, w_in_t, bg)


def _inproj_kernel(x_ref, wt_ref, o_ref):
    o_ref[...] = _dot_nt(x_ref[...], wt_ref[...]).astype(o_ref.dtype)


def _inproj(x, wt, n_cols, bm, bn):
    M, K = x.shape
    return pl.pallas_call(
        _inproj_kernel,
        grid=(n_cols // bn, M // bm),
        in_specs=[
            pl.BlockSpec((bm, K), lambda j, i: (i, 0)),
            pl.BlockSpec((bn, K), lambda j, i: (j, 0)),
        ],
        out_specs=pl.BlockSpec((bm, bn), lambda j, i: (i, j)),
        out_shape=jax.ShapeDtypeStruct((M, n_cols), F32),
        compiler_params=_params(("parallel", "parallel")),
        name="in_proj",
    )(x, wt)


def _outproj_kernel(seq, p_ref, m_ref, w_ref, x_hbm, meta_hbm, o_ref, rbuf, sem):
    j = pl.program_id(0)
    i = pl.program_id(1)
    ni = pl.num_programs(1)
    bn = rbuf.shape[2]
    t = j * ni + i

    def fetch(step, slot):
        cols = pl.ds(pl.multiple_of((step // ni) * bn, bn), bn)
        _token_block_start(lax.rem(step, ni), seq, x_hbm, meta_hbm, cols, rbuf.at[slot], sem.at[slot])

    @pl.when(t == 0)
    def _():
        fetch(0, 0)

    def step(slot):
        @pl.when(t + 1 < pl.num_programs(0) * ni)
        def _():
            fetch(t + 1, 1 - slot)
        acc = _dotw(p_ref[...], w_ref[:POOL_WIDTH, :]) + _dotw(m_ref[...], w_ref[POOL_WIDTH:, :])
        _token_block_wait(x_hbm, pl.ds(0, bn), rbuf.at[slot], sem.at[slot])
        o_ref[...] = rbuf[slot] + acc

    for slot in range(2):
        pl.when(lax.rem(t, 2) == slot)(functools.partial(step, slot))


def _outproj(p, m, w, x2, meta, batch, bm, bn):
    M = p.shape[0]
    K, N = w.shape
    seq = x2.shape[0] // batch
    assert (N_META + seq) % bm == 0
    return pl.pallas_call(
        functools.partial(_outproj_kernel, seq),
        grid=(N // bn, M // bm),
        in_specs=[
            pl.BlockSpec((bm, POOL_WIDTH), lambda j, i: (i, 0)),
            pl.BlockSpec((bm, MLSTM_WIDTH), lambda j, i: (i, 0)),
            pl.BlockSpec((K, bn), lambda j, i: (0, j)),
            pl.BlockSpec(memory_space=pl.ANY),
            pl.BlockSpec(memory_space=pl.ANY),
        ],
        out_specs=pl.BlockSpec((bm, bn), lambda j, i: (i, j)),
        out_shape=jax.ShapeDtypeStruct((M, N), F32),
        scratch_shapes=[pltpu.VMEM((2, bm, bn), F32), pltpu.SemaphoreType.DMA((2,))],
        compiler_params=_params(("arbitrary", "arbitrary")),
        name="out_proj",
    )(p, m, w, x2, meta)


def _pool_kernel(u_ref, w_ref, s_ref, o_ref, carry_ref):
    c = pl.program_id(1)

    @pl.when(c == 0)
    def _():
        carry_ref[...] = jnp.zeros_like(carry_ref)

    u = u_ref[0]
    ext = jnp.concatenate([carry_ref[...], u], axis=0)
    carry_ref[...] = u[SEQ_CHUNK - 16:, :]
    pos = c * SEQ_CHUNK + lax.broadcasted_iota(jnp.int32, (SEQ_CHUNK, 1), 0)
    for g, win in enumerate(POOL_WINDOWS):
        cols = slice(g * POOL_GROUP, (g + 1) * POOL_GROUP)
        s = ext[:, cols]
        span = 1
        while span < win:
            s = s + pltpu.roll(s, span, axis=0)
            span *= 2
        cnt = jnp.minimum(pos + 1, win).astype(F32)
        d = s[16:, :] / cnt - u[:, cols]
        y = _dot(d.astype(BF16), w_ref[g]) * s_ref[:, cols]
        o_ref[0, :, cols] = y.astype(o_ref.dtype)


def _pool(proj3, w_mix, scale):
    B, L, _ = proj3.shape
    nc = pl.cdiv(L, SEQ_CHUNK)
    return pl.pallas_call(
        _pool_kernel,
        grid=(B, nc),
        in_specs=[
            pl.BlockSpec((1, SEQ_CHUNK, POOL_WIDTH), lambda b, c: (b, c, 0)),
            pl.BlockSpec((len(POOL_WINDOWS), POOL_GROUP, POOL_GROUP), lambda b, c: (0, 0, 0)),
            pl.BlockSpec((1, POOL_WIDTH), lambda b, c: (0, 0)),
        ],
        out_specs=pl.BlockSpec((1, SEQ_CHUNK, POOL_WIDTH), lambda b, c: (b, c, 0)),
        out_shape=jax.ShapeDtypeStruct((B, L, POOL_WIDTH), BF16),
        scratch_shapes=[pltpu.VMEM((16, POOL_WIDTH), F32)],
        compiler_params=_params(("parallel", "arbitrary")),
        name="pool_mixer",
    )(proj3, w_mix, scale)


def _soft_cap(a):
    return GATE_SOFTCAP * jnp.tanh(a / GATE_SOFTCAP)


def _log_sigmoid(a):
    return jnp.minimum(a, 0.0) - jnp.log1p(jnp.exp(-jnp.abs(a)))


HEADS_PER_STEP = 2


def _mlstm_kernel(seq_len, q_ref, k_ref, v_ref, o_ref, gate_ref, out_ref, c_ref, n_ref, m_ref):
    c = pl.program_id(2)
    Lc = SEQ_CHUNK
    dk, dv = MLSTM_QK_DIM, MLSTM_V_DIM

    @pl.when(c == 0)
    def _():
        c_ref[...] = jnp.zeros_like(c_ref)
        n_ref[...] = jnp.zeros_like(n_ref)
        m_ref[...] = jnp.zeros_like(m_ref)

    row = lax.broadcasted_iota(jnp.int32, (Lc, 1), 0)
    col = lax.broadcasted_iota(jnp.int32, (1, Lc), 1)
    ok_col = (c * Lc + row) < seq_len
    causal = col <= row
    tri = causal.astype(BF16)
    tri_t = (row <= col).astype(BF16)
    lane = lax.broadcasted_iota(jnp.int32, (1, LANES), 1)
    capped = _soft_cap(gate_ref[0])
    gc = jnp.where(ok_col, jnp.where(lane < MLSTM_HEADS, capped, _log_sigmoid(capped)), 0.0)
    gt = gc.T
    for i in range(HEADS_PER_STEP):
        _mlstm_head(pl.program_id(1) * HEADS_PER_STEP + i, ok_col, causal, tri, tri_t, gc, gt,
                    q_ref[0, :, i * dk:(i + 1) * dk], k_ref[0, :, i * dk:(i + 1) * dk],
                    v_ref[0, :, i * dv:(i + 1) * dv], o_ref[0, :, i * dv:(i + 1) * dv],
                    out_ref.at[0, :, i * dv:(i + 1) * dv], c_ref.at[i], n_ref.at[i], m_ref.at[i])


def _mlstm_head(h, ok_col, causal, tri, tri_t, gc, gt, q_in, k_in, v_in, o_in, out_ref, c_ref, n_ref, m_ref):
    Lc = SEQ_CHUNK
    q = jnp.where(ok_col, q_in, 0.0) * (MLSTM_QK_DIM ** -0.5)
    k = jnp.where(ok_col, k_in, 0.0)
    v = jnp.where(ok_col, v_in, 0.0)

    lane = lax.broadcasted_iota(jnp.int32, (1, LANES), 1)
    sub = lax.broadcasted_iota(jnp.int32, (LANES, 1), 0)
    i_c = jnp.sum(jnp.where(lane == h, gc, 0.0), axis=1, keepdims=True)
    f_c = jnp.sum(jnp.where(lane == h + MLSTM_HEADS, gc, 0.0), axis=1, keepdims=True)
    i_r = jnp.sum(jnp.where(sub == h, gt, 0.0), axis=0, keepdims=True)
    f_r = jnp.sum(jnp.where(sub == h + MLSTM_HEADS, gt, 0.0), axis=0, keepdims=True)

    fb_c = jnp.broadcast_to(f_c, (Lc, LANES))
    b_c = sum(_dot(tri, p) for p in _split3(fb_c))[:, 0:1]
    fb_r = jnp.broadcast_to(f_r, (8, Lc))
    b_r8 = sum(_dot(p, tri_t) for p in _split3(fb_r))
    b_r = b_r8[0:1, :]
    g_tot = b_r8[0:1, Lc - 1:Lc]

    m_prev = m_ref[...]
    dlog = jnp.where(causal, b_c - b_r + i_r, -jnp.inf)
    inter_log = b_c + m_prev
    m_out = jnp.maximum(inter_log, jnp.max(dlog, axis=1, keepdims=True))
    wts = jnp.exp(dlog - m_out)
    inter_w = jnp.exp(inter_log - m_out)

    qb = q.astype(BF16)
    kb = k.astype(BF16)
    vb = v.astype(BF16)
    s = lax.dot_general(qb, kb, (((1,), (1,)), ((), ())), preferred_element_type=F32) * wts
    num = _dot(s.astype(BF16), vb) + inter_w * _dot(qb, c_ref[...].astype(BF16))
    den = jnp.sum(s, axis=1, keepdims=True) + inter_w * jnp.sum(q * n_ref[...], axis=1, keepdims=True)
    hh = num / jnp.maximum(jnp.abs(den), jnp.exp(-m_out))
    out_ref[...] = (jax.nn.sigmoid(o_in) * hh).astype(out_ref.dtype)

    a_c = g_tot - b_c + i_c
    m_new = jnp.maximum(g_tot + m_prev, jnp.max(a_c, axis=0, keepdims=True))
    wk = k * jnp.exp(a_c - m_new)
    decay = jnp.exp(g_tot + m_prev - m_new)
    c_ref[...] = decay * c_ref[...] + lax.dot_general(
        wk.astype(BF16), vb, (((0,), (0,)), ((), ())), preferred_element_type=F32)
    n_ref[...] = decay * n_ref[...] + jnp.sum(wk, axis=0, keepdims=True)
    m_ref[...] = m_new


def _mlstm(proj3, gates3):
    B, L, _ = proj3.shape
    nc = pl.cdiv(L, SEQ_CHUNK)
    G = HEADS_PER_STEP
    dk, dv, H = G * MLSTM_QK_DIM, G * MLSTM_V_DIM, MLSTM_HEADS // G
    q0 = POOL_WIDTH // dk
    k0 = q0 + H
    v0 = (POOL_WIDTH + 2 * MLSTM_HEADS * MLSTM_QK_DIM) // dv
    o0 = v0 + H
    return pl.pallas_call(
        functools.partial(_mlstm_kernel, L),
        grid=(B, H, nc),
        in_specs=[
            pl.BlockSpec((1, SEQ_CHUNK, dk), lambda b, h, c: (b, c, q0 + h)),
            pl.BlockSpec((1, SEQ_CHUNK, dk), lambda b, h, c: (b, c, k0 + h)),
            pl.BlockSpec((1, SEQ_CHUNK, dv), lambda b, h, c: (b, c, v0 + h)),
            pl.BlockSpec((1, SEQ_CHUNK, dv), lambda b, h, c: (b, c, o0 + h)),
            pl.BlockSpec((1, SEQ_CHUNK, LANES), lambda b, h, c: (b, c, 0)),
        ],
        out_specs=pl.BlockSpec((1, SEQ_CHUNK, dv), lambda b, h, c: (b, c, h)),
        out_shape=jax.ShapeDtypeStruct((B, L, MLSTM_WIDTH), BF16),
        scratch_shapes=[
            pltpu.VMEM((G, MLSTM_QK_DIM, MLSTM_V_DIM), F32),
            pltpu.VMEM((G, 1, MLSTM_QK_DIM), F32),
            pltpu.VMEM((G, 1, 1), F32),
        ],
        compiler_params=_params(("parallel", "parallel", "arbitrary")),
        name="mlstm",
    )(proj3, proj3, proj3, proj3, gates3)


def _pack_bf16_pair(lo, hi):
    lo_b = lax.bitcast_convert_type(lo.astype(BF16).astype(F32), jnp.uint32)
    hi_b = lax.bitcast_convert_type(hi.astype(BF16).astype(F32), jnp.uint32)
    return (lo_b >> 16) | hi_b


def _unpack_bf16_pair(w):
    lo = lax.bitcast_convert_type(w << 16, F32).astype(BF16)
    hi = lax.bitcast_convert_type(w & jnp.uint32(0xFFFF0000), F32).astype(BF16)
    return lo, hi


def _router_kernel(h_ref, g_ref, wr_ref, br_ref, n_ref, e_ref, p_ref):
    y = _rms(h_ref[...], g_ref[...])
    half = D_MODEL // 2
    n_ref[...] = _pack_bf16_pair(y[:, :half], y[:, half:])
    logits = _dot3_nt(y, wr_ref[...]) + br_ref[...]
    lane = lax.broadcasted_iota(jnp.int32, logits.shape, 1).astype(F32)
    l = jnp.where(lane < N_EXPERTS, logits, -jnp.inf)
    vals, idxs = [], []
    for _ in range(TOP_K):
        m = jnp.max(l, axis=1, keepdims=True)
        idx = jnp.min(jnp.where(l == m, lane, float(LANES)), axis=1, keepdims=True)
        vals.append(m)
        idxs.append(idx)
        l = jnp.where(lane == idx, -jnp.inf, l)
    ex = [jnp.exp(v - vals[0]) for v in vals]
    tot = ex[0] + ex[1] + ex[2] + ex[3]
    e_out = jnp.zeros(logits.shape, F32)
    p_out = jnp.zeros(logits.shape, F32)
    for kk in range(TOP_K):
        e_out = jnp.where(lane == kk, idxs[kk], e_out)
        p_out = jnp.where(lane == kk, ex[kk] / tot, p_out)
    e_ref[...] = e_out.astype(jnp.int32)
    p_ref[...] = p_out


def _router(h, g, wr, br, bm):
    T = h.shape[0]
    return pl.pallas_call(
        _router_kernel,
        grid=(T // bm,),
        in_specs=[
            pl.BlockSpec((bm, D_MODEL), lambda i: (i, 0)),
            pl.BlockSpec((1, D_MODEL), lambda i: (0, 0)),
            pl.BlockSpec((N_EXPERTS, D_MODEL), lambda i: (0, 0)),
            pl.BlockSpec((1, N_EXPERTS), lambda i: (0, 0)),
        ],
        out_specs=[
            pl.BlockSpec((bm, D_MODEL // 2), lambda i: (i, 0)),
            pl.BlockSpec((bm, N_EXPERTS), lambda i: (i, 0)),
            pl.BlockSpec((bm, N_EXPERTS), lambda i: (i, 0)),
        ],
        out_shape=[
            jax.ShapeDtypeStruct((T, D_MODEL // 2), jnp.uint32),
            jax.ShapeDtypeStruct((T, N_EXPERTS), jnp.int32),
            jax.ShapeDtypeStruct((T, N_EXPERTS), F32),
        ],
        compiler_params=_params(("parallel",)),
        name="norm2_router",
    )(h, g, wr, br)


def _experts_kernel(sbe_ref, rb0_ref, nbk_ref, used_ref, tok_ref,
                    x_hbm, wgu_hbm, wd_hbm, bgu_ref, bd_ref,
                    y_hbm,
                    x_res, hmid, wgl_buf, wd_buf, ystage, gsem, wsem, dsem, ysem, ypend, ycnt, gcnt):
    s = pl.program_id(0)
    ns = pl.num_programs(0)
    nb = nbk_ref[s]
    expert = sbe_ref[s]
    nxt = jnp.minimum(s + 1, ns - 1)
    next_active = jnp.logical_and(s + 1 < ns, nbk_ref[nxt] > 0)
    half = D_MODEL // 2
    n_blocks_total = y_hbm.shape[0] // MOE_BLOCK

    def gateup_copies(e, j, slot):
        gate_cols = pl.ds(pl.multiple_of(j * FF_TILE, FF_TILE), FF_TILE)
        lin_cols = pl.ds(pl.multiple_of(D_FF + j * FF_TILE, FF_TILE), FF_TILE)
        return (pltpu.make_async_copy(wgu_hbm.at[e, :, gate_cols], wgl_buf.at[slot, 0], wsem.at[slot]),
                pltpu.make_async_copy(wgu_hbm.at[e, :, lin_cols], wgl_buf.at[slot, 1], wsem.at[slot]))

    def down_copy(e, j, slot):
        cols = pl.ds(pl.multiple_of(j * DOWN_TILE, DOWN_TILE), DOWN_TILE)
        return pltpu.make_async_copy(wd_hbm.at[e, :, cols], wd_buf.at[slot], dsem.at[slot])

    def x_row_copy(tok, r8, rs):
        return pltpu.make_async_copy(x_hbm.at[tok >> 3, pl.ds(tok & 7, 1), :], x_res.at[r8, pl.ds(rs, 1), :], gsem.at[0])

    def x_block_wait():
        g = MOE_BLOCK // SUBLANES
        pltpu.make_async_copy(x_hbm.at[pl.ds(0, g)], x_res.at[pl.ds(0, g)], gsem.at[0]).wait()

    def gather_upto(sb, n_rows):
        base = rb0_ref[sb] * MOE_BLOCK

        def body(r, carry):
            x_row_copy(tok_ref[base + r], r >> 3, r & 7).start()
            return carry
        lax.fori_loop(gcnt[0], n_rows, body, 0)
        gcnt[0] = jnp.maximum(gcnt[0], n_rows)

    def gather_some(sb, n):
        base = rb0_ref[sb] * MOE_BLOCK
        cur = gcnt[0]
        cur8 = cur >> 3
        for i in range(n):
            x_row_copy(tok_ref[base + cur + i], cur8 + i // SUBLANES, i % SUBLANES).start()
        gcnt[0] = cur + n

    def gather_wait():
        n = gcnt[0]
        n_full = n // MOE_BLOCK

        def block_body(r, carry):
            x_block_wait()
            return carry

        def row_body(r, carry):
            x_row_copy(0, 0, 0).wait()
            return carry
        lax.fori_loop(0, n_full, block_body, 0)
        lax.fori_loop(n_full * MOE_BLOCK, n, row_body, 0)
        gcnt[0] = 0

    def y_copy(slot, piece, row0, word0):
        return pltpu.make_async_copy(
            ystage.at[slot, pl.ds(piece * MOE_BLOCK, MOE_BLOCK), :],
            y_hbm.at[pl.ds(pl.multiple_of(row0, MOE_BLOCK), MOE_BLOCK), pl.ds(pl.multiple_of(word0, Y_WORDS), Y_WORDS)],
            ysem.at[slot])

    def y_drain(slot):
        for piece in range(CHUNK_BLOCKS):
            @pl.when(ypend[slot] > piece)
            def _():
                y_copy(slot, 0, 0, 0).wait()
        ypend[slot] = 0

    def y_emit(val, n_pieces, slot, row0, word0):
        ystage[slot, pl.ds(0, n_pieces * MOE_BLOCK), :] = _pack_bf16_pair(val[:, :Y_WORDS], val[:, Y_WORDS:])
        for piece in range(n_pieces):
            y_copy(slot, piece, row0 + piece * MOE_BLOCK, word0).start()
        ypend[slot] = n_pieces

    def for_chunks(n_blocks, fn):
        done = 0
        for pos, k in enumerate(CHUNK_SIZES):
            if pos == 0:
                n_big = n_blocks // k

                def body(c, carry, k=k):
                    fn(c * k, k)
                    return carry
                lax.fori_loop(0, n_big, body, 0)
                done = n_big * k
            else:
                fits = n_blocks - done >= k

                @pl.when(fits)
                def _(k=k, done=done):
                    fn(done, k)
                done = jnp.where(fits, done + k, done)

    @pl.when(s == 0)
    def _():
        ypend[0] = 0
        ypend[1] = 0
        ycnt[0] = 0
        gcnt[0] = 0

        @pl.when(nb > 0)
        def _():
            for c in gateup_copies(expert, 0, 0):
                c.start()
        gather_upto(0, nb * MOE_BLOCK)

    @pl.when(nb > 0)
    def _():
        gather_wait()
        row_base = rb0_ref[s] * MOE_BLOCK

        def gateup_tile(j, carry):
            slot = lax.rem(j, 2)
            for c in gateup_copies(expert, j, slot):
                c.wait()

            @pl.when(j + 1 < N_FF_TILES)
            def _():
                for c in gateup_copies(expert, j + 1, 1 - slot):
                    c.start()

            @pl.when(j + 1 == N_FF_TILES)
            def _():
                down_copy(expert, 0, 0).start()

            bg = bgu_ref[0, pl.ds(j, 1), :]
            bl = bgu_ref[0, pl.ds(N_FF_TILES + j, 1), :]

            def chunk(b0, k):
                rows = pl.ds(pl.multiple_of(b0 * MOE_BLOCK, MOE_BLOCK), k * MOE_BLOCK)
                groups = pl.ds(pl.multiple_of(b0 * (MOE_BLOCK // SUBLANES), MOE_BLOCK // SUBLANES),
                               k * (MOE_BLOCK // SUBLANES))
                x_lo, x_hi = _unpack_bf16_pair(x_res[groups].reshape(k * MOE_BLOCK, half))
                gate = _dotw(x_lo, wgl_buf[slot, 0, :half, :]) + _dotw(x_hi, wgl_buf[slot, 0, half:, :]) + bg
                lin = _dotw(x_lo, wgl_buf[slot, 1, :half, :]) + _dotw(x_hi, wgl_buf[slot, 1, half:, :]) + bl
                gate = jnp.minimum(gate, SWIGLU_LIMIT)
                lin = jnp.clip(lin, -SWIGLU_LIMIT, SWIGLU_LIMIT)
                hmid[j, rows, :] = (gate * jax.nn.sigmoid(SWIGLU_ALPHA * gate) * (lin + 1.0)).astype(BF16)
            for_chunks(nb, chunk)
            return carry
        lax.fori_loop(0, N_FF_TILES, gateup_tile, 0)

        def down_tile(fetch_next, j, carry):
            slot = lax.rem(j, 2)
            down_copy(expert, j, slot).wait()

            @pl.when(j + 1 < N_DOWN_TILES)
            def _():
                down_copy(expert, j + 1, 1 - slot).start()

            @pl.when(jnp.logical_and(j + 1 == N_DOWN_TILES, next_active))
            def _():
                for c in gateup_copies(sbe_ref[nxt], 0, 0):
                    c.start()

            bd = bd_ref[0, pl.ds(j, 1), :]
            word0 = j * Y_WORDS

            def chunk(b0, k):
                yslot = lax.rem(ycnt[0], 2)
                ycnt[0] = ycnt[0] + 1
                y_drain(yslot)
                if fetch_next:
                    gather_some(nxt, k * GATHER_PER_VISIT)
                rows = pl.ds(pl.multiple_of(b0 * MOE_BLOCK, MOE_BLOCK), k * MOE_BLOCK)
                h = jnp.concatenate([hmid[jj, rows, :] for jj in range(N_FF_TILES)], axis=1)
                acc = bd + _dotw(h, wd_buf[slot])
                y_emit(acc, k, yslot, row_base + b0 * MOE_BLOCK, word0)
            for_chunks(nb, chunk)
            return carry
        lax.fori_loop(0, GATHER_TILES, functools.partial(down_tile, True), 0)
        lax.fori_loop(GATHER_TILES, N_DOWN_TILES, functools.partial(down_tile, False), 0)
        gather_upto(nxt, jnp.where(s + 1 < ns, nbk_ref[nxt] * MOE_BLOCK, 0))

    @pl.when(s == ns - 1)
    def _():
        gather_wait()
        y_drain(0)
        y_drain(1)
        ystage[0, pl.ds(0, MOE_BLOCK), :] = jnp.zeros((MOE_BLOCK, Y_WORDS), jnp.uint32)

        def zero_copy(b, j):
            return y_copy(0, 0, b * MOE_BLOCK, j * Y_WORDS)

        def start_body(b, carry):
            for j in range(N_DOWN_TILES):
                zero_copy(b, j).start()
            return carry

        def wait_body(b, carry):
            for j in range(N_DOWN_TILES):
                zero_copy(0, 0).wait()
            return carry
        lax.fori_loop(used_ref[0], n_blocks_total, start_body, 0)
        lax.fori_loop(used_ref[0], n_blocks_total, wait_body, 0)


def _experts(sb_e, sb_rb0, sb_nb, n_used, row_tok, n2p, w_gu, b_gu, w_down, b_down):
    R = row_tok.shape[0] - SB_BLOCKS * MOE_BLOCK
    S = sb_e.shape[0]
    sb_rows = SB_BLOCKS * MOE_BLOCK

    grid_spec = pltpu.PrefetchScalarGridSpec(
        num_scalar_prefetch=5,
        grid=(S,),
        in_specs=[
            pl.BlockSpec(memory_space=pl.ANY),
            pl.BlockSpec(memory_space=pl.ANY),
            pl.BlockSpec(memory_space=pl.ANY),
            pl.BlockSpec((1, 2 * N_FF_TILES, FF_TILE), lambda s, e, r0, nbk, u, tok: (e[s], 0, 0)),
            pl.BlockSpec((1, N_DOWN_TILES, DOWN_TILE), lambda s, e, r0, nbk, u, tok: (e[s], 0, 0)),
        ],
        out_specs=pl.BlockSpec(memory_space=pl.ANY),
        scratch_shapes=[
            pltpu.VMEM((sb_rows // SUBLANES, SUBLANES, D_MODEL // 2), jnp.uint32),
            pltpu.VMEM((N_FF_TILES, sb_rows, FF_TILE), BF16),
            pltpu.VMEM((2, 2, D_MODEL, FF_TILE), F32),
            pltpu.VMEM((2, D_FF, DOWN_TILE), F32),
            pltpu.VMEM((2, CHUNK_BLOCKS * MOE_BLOCK, Y_WORDS), jnp.uint32),
            pltpu.SemaphoreType.DMA((1,)),
            pltpu.SemaphoreType.DMA((2,)),
            pltpu.SemaphoreType.DMA((2,)),
            pltpu.SemaphoreType.DMA((2,)),
            pltpu.SMEM((2,), jnp.int32),
            pltpu.SMEM((1,), jnp.int32),
            pltpu.SMEM((1,), jnp.int32),
        ],
    )
    return pl.pallas_call(
        _experts_kernel,
        grid_spec=grid_spec,
        out_shape=jax.ShapeDtypeStruct((R, D_MODEL // 2), jnp.uint32),
        compiler_params=_params(("arbitrary",)),
        name="moe_experts",
    )(sb_e, sb_rb0, sb_nb, n_used, row_tok, n2p.reshape(-1, SUBLANES, D_MODEL // 2), w_gu, w_down,
      b_gu.reshape(N_EXPERTS, 2 * N_FF_TILES, FF_TILE), b_down.reshape(N_EXPERTS, N_DOWN_TILES, DOWN_TILE))


def _combine_kernel(tb, dest_ref, piece_ref, y_hbm, h_ref, p_ref, g_ref, out_hbm, buf0, buf1, ostage, sem, osem,
                    opend):
    i = pl.program_id(0)
    nb = pl.num_programs(0)
    rows = tb * TOP_K
    n_pieces = tb // N_META
    bufs = (buf0, buf1)

    def gather_start(step, slot, first, n):
        for r in range(first, first + n):
            row = dest_ref[step * rows + r]
            pltpu.make_async_copy(y_hbm.at[pl.ds(row, 1), :], bufs[slot].at[pl.ds(r, 1), :], sem.at[slot]).start()

    def gather_wait(slot):
        pltpu.make_async_copy(y_hbm.at[pl.ds(0, rows), :], bufs[slot], sem.at[slot]).wait()

    def out_copy(slot, piece, dst):
        return pltpu.make_async_copy(ostage.at[slot, pl.ds(piece * N_META, N_META), :],
                                     out_hbm.at[pl.ds(pl.multiple_of(dst, N_META), N_META), :], osem.at[slot])

    def out_drain(slot):
        for piece in range(n_pieces):
            @pl.when(opend[slot] > piece)
            def _():
                out_copy(slot, 0, 0).wait()
        opend[slot] = 0

    @pl.when(i == 0)
    def _():
        opend[0] = 0
        opend[1] = 0
        gather_start(0, 0, 0, rows)

    def step(slot):
        gather_wait(slot)
        nxt = jnp.minimum(i + 1, nb - 1)
        acc = h_ref[...]
        p = p_ref[...]
        for kk in range(TOP_K):
            gather_start(nxt, 1 - slot, kk * tb, tb)
            w = bufs[slot][pl.ds(kk * tb, tb), :]
            lo = lax.bitcast_convert_type(w << 16, F32)
            hi = lax.bitcast_convert_type(w & jnp.uint32(0xFFFF0000), F32)
            pieces = []
            for j in range(N_DOWN_TILES):
                pieces += [lo[:, j * Y_WORDS:(j + 1) * Y_WORDS], hi[:, j * Y_WORDS:(j + 1) * Y_WORDS]]
            acc = acc + p[:, kk:kk + 1] * jnp.concatenate(pieces, axis=1)
        out_drain(slot)
        ostage[slot] = _rms(acc, g_ref[...])
        started = 0
        for piece in range(n_pieces):
            dst = piece_ref[i * n_pieces + piece]

            @pl.when(dst >= 0)
            def _():
                out_copy(slot, piece, dst).start()
            started = started + (dst >= 0).astype(jnp.int32)
        opend[slot] = started

    for slot in range(2):
        pl.when(lax.rem(i, 2) == slot)(functools.partial(step, slot))

    @pl.when(i == nb - 1)
    def _():
        for slot in range(2):
            @pl.when(lax.rem(i, 2) == slot)
            def _():
                gather_wait(1 - slot)
        out_drain(0)
        out_drain(1)


def _combine(dest_km, pieces, y, h, probs, g, tb, n_out):
    T = h.shape[0]
    grid_spec = pltpu.PrefetchScalarGridSpec(
        num_scalar_prefetch=2,
        grid=(T // tb,),
        in_specs=[
            pl.BlockSpec(memory_space=pl.ANY),
            pl.BlockSpec((tb, D_MODEL), lambda i, d, q: (i, 0)),
            pl.BlockSpec((tb, N_EXPERTS), lambda i, d, q: (i, 0)),
            pl.BlockSpec((1, D_MODEL), lambda i, d, q: (0, 0)),
        ],
        out_specs=pl.BlockSpec(memory_space=pl.ANY),
        scratch_shapes=[
            pltpu.VMEM((tb * TOP_K, D_MODEL // 2), jnp.uint32),
            pltpu.VMEM((tb * TOP_K, D_MODEL // 2), jnp.uint32),
            pltpu.VMEM((2, tb, D_MODEL), F32),
            pltpu.SemaphoreType.DMA((2,)),
            pltpu.SemaphoreType.DMA((2,)),
            pltpu.SMEM((2,), jnp.int32),
        ],
    )
    return pl.pallas_call(
        functools.partial(_combine_kernel, tb),
        grid_spec=grid_spec,
        out_shape=jax.ShapeDtypeStruct((n_out, D_MODEL), F32),
        compiler_params=_params(("arbitrary",)),
        name="moe_combine_norm",
    )(dest_km, pieces, y, h, probs, g)


def _routing(top_e, tb):
    T = top_e.shape[0]
    A = T * TOP_K
    n_blocks = -(-(A + N_EXPERTS * (MOE_BLOCK - 1)) // MOE_BLOCK)
    R = n_blocks * MOE_BLOCK
    n_sb = N_EXPERTS + n_blocks // SB_BLOCKS
    i32 = jnp.int32
    e_flat = top_e.reshape(A)
    onehot = (e_flat[:, None] == jnp.arange(N_EXPERTS, dtype=i32)[None, :]).astype(i32)
    csum = jnp.cumsum(onehot, axis=0)
    rank = jnp.sum(csum * onehot, axis=1) - 1
    counts = csum[-1]
    blocks = (counts + MOE_BLOCK - 1) // MOE_BLOCK
    blk_end = jnp.cumsum(blocks)
    blk_start = blk_end - blocks
    dest = (blk_start * MOE_BLOCK)[e_flat] + rank
    row_tok = jnp.zeros((R + SB_BLOCKS * MOE_BLOCK,), i32).at[dest].set(
        jnp.arange(A, dtype=i32) // TOP_K, unique_indices=True)
    dest_km = dest.reshape(T // tb, tb, TOP_K).transpose(0, 2, 1).reshape(A).astype(i32)

    sbs = (blocks + SB_BLOCKS - 1) // SB_BLOCKS
    sb_end = jnp.cumsum(sbs)
    sb_start = sb_end - sbs
    s_idx = jnp.arange(n_sb, dtype=i32)
    active = s_idx < sb_end[-1]
    e_s = jnp.minimum(jnp.sum((sb_end[None, :] <= s_idx[:, None]).astype(i32), axis=1), N_EXPERTS - 1)
    k_s = s_idx - sb_start[e_s]
    sb_rb0 = jnp.where(active, blk_start[e_s] + k_s * SB_BLOCKS, 0)
    sb_nb = jnp.where(active, jnp.minimum(blocks[e_s] - k_s * SB_BLOCKS, SB_BLOCKS), 0)
    sb_e = jnp.where(active, e_s, jnp.max(jnp.where(active, e_s, 0)))
    n_used = blk_end[-1:].astype(i32)
    return row_tok, dest_km, sb_e.astype(i32), sb_rb0.astype(i32), sb_nb.astype(i32), n_used


def kernel(x, meta_tokens, norm1_g, w_in, b_igate, b_fgate, w_pool_mix, pool_scale, w_out, norm2_g, w_router,
           b_router, w_gu, b_gu, w_down, b_down, norm_f_g):
    B, S, D = x.shape
    L = N_META + S
    T = B * L
    H = MLSTM_HEADS
    BM_IN = 688
    BM_OUT = 688
    BM_NORM1 = 688
    BM_NORM = 688
    TB = 192
    assert T % BM_IN == 0 and T % BM_OUT == 0 and T % BM_NORM1 == 0 and T % BM_NORM == 0 and T % TB == 0
    assert w_in.shape[0] == 1
    assert L % N_META == 0 and TB % N_META == 0

    pieces = jnp.asarray(_piece_table(B, S))
    x2 = x.reshape(B * S, D)
    meta = meta_tokens.astype(x.dtype)

    l = 0
    w_in_t = w_in.reshape(D, PROJ_COLS + 2 * H).T
    bg = jnp.pad(jnp.concatenate([b_igate[l], b_fgate[l]]), (0, LANES - 2 * H)).reshape(1, LANES)
    n1, gates = _norm1(x2, meta, B, norm1_g[l].reshape(1, D), w_in_t, bg, BM_NORM1)

    proj3 = _inproj(n1, w_in_t, PROJ_COLS, BM_IN, 1024).reshape(B, L, PROJ_COLS)
    pool_out = _pool(proj3, w_pool_mix[l].astype(BF16), pool_scale[l].reshape(1, POOL_WIDTH))
    mlstm_out = _mlstm(proj3, gates.reshape(B, L, LANES))

    h1 = _outproj(pool_out.reshape(T, POOL_WIDTH), mlstm_out.reshape(T, MLSTM_WIDTH),
                  w_out.reshape(D, D), x2, meta, B, BM_OUT, 512)

    n2p, top_e, probs = _router(h1, norm2_g[l].reshape(1, D), w_router.reshape(D, N_EXPERTS).T,
                                b_router.reshape(1, N_EXPERTS), BM_NORM)

    row_tok, dest_km, sb_e, sb_rb0, sb_nb, n_used = _routing(top_e[:, :TOP_K], TB)
    y = _experts(sb_e, sb_rb0, sb_nb, n_used, row_tok, n2p,
                 w_gu.reshape(N_EXPERTS, D, 2 * D_FF), b_gu.reshape(N_EXPERTS, 1, 2 * D_FF),
                 w_down.reshape(N_EXPERTS, D_FF, D), b_down.reshape(N_EXPERTS, 1, D))
    out = _combine(dest_km, pieces, y, h1, probs, norm_f_g.reshape(1, D), TB, B * S)
    return out.reshape(B, S, D)
```

```python
import functools

import jax
import jax.numpy as jnp
import numpy as np
from jax import lax
from jax.experimental import pallas as pl
from jax.experimental.pallas import tpu as pltpu

D_MODEL = 4096
N_META = 16
POOL_WINDOWS = (2, 4, 8, 16)
POOL_WIDTH = D_MODEL // 4
POOL_GROUP = POOL_WIDTH // len(POOL_WINDOWS)
MLSTM_WIDTH = D_MODEL - POOL_WIDTH
MLSTM_HEADS = 6
MLSTM_V_DIM = MLSTM_WIDTH // MLSTM_HEADS
MLSTM_QK_DIM = MLSTM_V_DIM // 2
GATE_SOFTCAP = 15.0
N_EXPERTS = 32
TOP_K = 4
D_FF = D_MODEL // 2
SWIGLU_ALPHA = 1.702
SWIGLU_LIMIT = 7.0
MOE_BLOCK = 128
EPS = 1e-6

LANES = 128
SUBLANES = 8
SEQ_CHUNK = 256
PROJ_COLS = POOL_WIDTH + 2 * MLSTM_HEADS * MLSTM_QK_DIM + 2 * MLSTM_WIDTH
VMEM_LIMIT = 56 * 1024 * 1024
SB_BLOCKS = 10
CHUNK_SIZES = (6, 3, 2, 1)
CHUNK_BLOCKS = max(CHUNK_SIZES)
FF_TILE = 256
DOWN_TILE = 1024
N_FF_TILES = D_FF // FF_TILE
N_DOWN_TILES = D_MODEL // DOWN_TILE
Y_WORDS = DOWN_TILE // 2
WEIGHT_DMA_THREAD = 1
GATHER_TILES = N_DOWN_TILES // 2
GATHER_PER_VISIT = MOE_BLOCK // GATHER_TILES

F32 = jnp.float32
BF16 = jnp.bfloat16


def _params(sem, vmem=VMEM_LIMIT):
    return pltpu.CompilerParams(dimension_semantics=sem, vmem_limit_bytes=vmem)


def _split3(a):
    hi = a.astype(BF16)
    r1 = a - hi.astype(F32)
    mid = r1.astype(BF16)
    lo = (r1 - mid.astype(F32)).astype(BF16)
    return hi, mid, lo


def _dot(a, b):
    return jnp.dot(a, b, preferred_element_type=F32)


def _dotw(a, w):
    return lax.dot_general(a, w, (((1,), (0,)), ((), ())), preferred_element_type=F32)


def _dot_nt(a, wt):
    return lax.dot_general(a, wt, (((1,), (1,)), ((), ())), preferred_element_type=F32)


def _dot3_nt(a, wt):
    a_hi = a.astype(BF16)
    a_lo = (a - a_hi.astype(F32)).astype(BF16)
    w_hi = wt.astype(BF16)
    w_lo = (wt - w_hi.astype(F32)).astype(BF16)
    n = wt.shape[0]
    if n == LANES:
        both = _dot_nt(a_hi, jnp.concatenate([w_hi, w_lo], axis=0))
        return both[:, :n] + (both[:, n:] + _dot_nt(a_lo, w_hi))
    return _dot_nt(a_hi, w_hi) + (_dot_nt(a_hi, w_lo) + _dot_nt(a_lo, w_hi))


def _rms(x, g):
    return x * lax.rsqrt(jnp.mean(x * x, axis=-1, keepdims=True) + EPS) * g


GATE_ROWS = 16


def _piece_table(batch, seq):
    L = N_META + seq
    first = np.arange(batch * L // N_META) * N_META
    pos = first % L
    return np.where(pos >= N_META, (first // L) * seq + pos - N_META, -1).astype(np.int32)


def _token_block_start(blk, seq, x_hbm, meta_hbm, cols, dst, sem):
    bm = dst.shape[0]
    per_seq = (N_META + seq) // bm
    b = blk // per_seq
    r = blk - b * per_seq

    @pl.when(r == 0)
    def _():
        pltpu.make_async_copy(meta_hbm.at[:, cols], dst.at[pl.ds(0, N_META), :], sem).start()
        pltpu.make_async_copy(x_hbm.at[pl.ds(pl.multiple_of(b * seq, N_META), bm - N_META), cols],
                              dst.at[pl.ds(N_META, bm - N_META), :], sem).start()

    @pl.when(r > 0)
    def _():
        pltpu.make_async_copy(x_hbm.at[pl.ds(pl.multiple_of(b * seq + r * bm - N_META, N_META), bm), cols],
                              dst, sem).start()


def _token_block_wait(x_hbm, cols, dst, sem):
    pltpu.make_async_copy(x_hbm.at[pl.ds(0, dst.shape[0]), cols], dst, sem).wait()


def _norm1_kernel(seq, x_hbm, meta_hbm, g_ref, wg_ref, bg_ref, n_ref, gate_ref, hbuf, sem):
    i = pl.program_id(0)
    all_cols = pl.ds(0, D_MODEL)

    @pl.when(i == 0)
    def _():
        _token_block_start(0, seq, x_hbm, meta_hbm, all_cols, hbuf.at[0], sem.at[0])

    def step(slot):
        @pl.when(i + 1 < pl.num_programs(0))
        def _():
            _token_block_start(i + 1, seq, x_hbm, meta_hbm, all_cols, hbuf.at[1 - slot], sem.at[1 - slot])
        _token_block_wait(x_hbm, all_cols, hbuf.at[slot], sem.at[slot])
        _norm1_body(hbuf[slot], g_ref, wg_ref, bg_ref, n_ref, gate_ref)

    for slot in range(2):
        pl.when(lax.rem(i, 2) == slot)(functools.partial(step, slot))


def _norm1_body(h, g_ref, wg_ref, bg_ref, n_ref, gate_ref):
    y = _rms(h, g_ref[...])
    n_ref[...] = y.astype(BF16)
    sub = lax.broadcasted_iota(jnp.int32, (GATE_ROWS, D_MODEL), 0)
    wg = jnp.where(sub < 2 * MLSTM_HEADS, wg_ref[...], 0.0)
    wg = jnp.concatenate([wg, jnp.zeros((LANES - GATE_ROWS, D_MODEL), F32)], axis=0)
    gate_ref[...] = _dot3_nt(y, wg) + bg_ref[...]


def _norm1(x2, meta, batch, g, w_in_t, bg, bm):
    seq = x2.shape[0] // batch
    T = batch * (N_META + seq)
    assert (N_META + seq) % bm == 0
    return pl.pallas_call(
        functools.partial(_norm1_kernel, seq),
        grid=(T // bm,),
        in_specs=[
            pl.BlockSpec(memory_space=pl.ANY),
            pl.BlockSpec(memory_space=pl.ANY),
            pl.BlockSpec((1, D_MODEL), lambda i: (0, 0)),
            pl.BlockSpec((GATE_ROWS, D_MODEL), lambda i: (PROJ_COLS // GATE_ROWS, 0)),
            pl.BlockSpec((1, LANES), lambda i: (0, 0)),
        ],
        out_specs=[
            pl.BlockSpec((bm, D_MODEL), lambda i: (i, 0)),
            pl.BlockSpec((bm, LANES), lambda i: (i, 0)),
        ],
        out_shape=[
            jax.ShapeDtypeStruct((T, D_MODEL), BF16),
            jax.ShapeDtypeStruct((T, LANES), F32),
        ],
        scratch_shapes=[pltpu.VMEM((2, bm, D_MODEL), F32), pltpu.SemaphoreType.DMA((2,))],
        compiler_params=_params(("arbitrary",)),
        name="norm1_gates",
    )(x2, meta, g, w_in_t, bg)


def _inproj_kernel(x_ref, wt_ref, o_ref):
    o_ref[...] = _dot_nt(x_ref[...], wt_ref[...]).astype(o_ref.dtype)


def _inproj(x, wt, n_cols, bm, bn):
    M, K = x.shape
    return pl.pallas_call(
        _inproj_kernel,
        grid=(n_cols // bn, M // bm),
        in_specs=[
            pl.BlockSpec((bm, K), lambda j, i: (i, 0)),
            pl.BlockSpec((bn, K), lambda j, i: (j, 0)),
        ],
        out_specs=pl.BlockSpec((bm, bn), lambda j, i: (i, j)),
        out_shape=jax.ShapeDtypeStruct((M, n_cols), F32),
        compiler_params=_params(("parallel", "parallel")),
        name="in_proj",
    )(x, wt)


def _outproj_kernel(seq, p_ref, m_ref, w_ref, x_hbm, meta_hbm, o_ref, rbuf, sem):
    j = pl.program_id(0)
    i = pl.program_id(1)
    ni = pl.num_programs(1)
    bn = rbuf.shape[2]
    t = j * ni + i

    def fetch(step, slot):
        cols = pl.ds(pl.multiple_of((step // ni) * bn, bn), bn)
        _token_block_start(lax.rem(step, ni), seq, x_hbm, meta_hbm, cols, rbuf.at[slot], sem.at[slot])

    @pl.when(t == 0)
    def _():
        fetch(0, 0)

    def step(slot):
        @pl.when(t + 1 < pl.num_programs(0) * ni)
        def _():
            fetch(t + 1, 1 - slot)
        acc = _dotw(p_ref[...], w_ref[:POOL_WIDTH, :]) + _dotw(m_ref[...], w_ref[POOL_WIDTH:, :])
        _token_block_wait(x_hbm, pl.ds(0, bn), rbuf.at[slot], sem.at[slot])
        o_ref[...] = rbuf[slot] + acc

    for slot in range(2):
        pl.when(lax.rem(t, 2) == slot)(functools.partial(step, slot))


def _outproj(p, m, w, x2, meta, batch, bm, bn):
    M = p.shape[0]
    K, N = w.shape
    seq = x2.shape[0] // batch
    assert (N_META + seq) % bm == 0
    return pl.pallas_call(
        functools.partial(_outproj_kernel, seq),
        grid=(N // bn, M // bm),
        in_specs=[
            pl.BlockSpec((bm, POOL_WIDTH), lambda j, i: (i, 0)),
            pl.BlockSpec((bm, MLSTM_WIDTH), lambda j, i: (i, 0)),
            pl.BlockSpec((K, bn), lambda j, i: (0, j)),
            pl.BlockSpec(memory_space=pl.ANY),
            pl.BlockSpec(memory_space=pl.ANY),
        ],
        out_specs=pl.BlockSpec((bm, bn), lambda j, i: (i, j)),
        out_shape=jax.ShapeDtypeStruct((M, N), F32),
        scratch_shapes=[pltpu.VMEM((2, bm, bn), F32), pltpu.SemaphoreType.DMA((2,))],
        compiler_params=_params(("arbitrary", "arbitrary")),
        name="out_proj",
    )(p, m, w, x2, meta)


def _pool_kernel(u_ref, w_ref, s_ref, o_ref, carry_ref):
    c = pl.program_id(1)

    @pl.when(c == 0)
    def _():
        carry_ref[...] = jnp.zeros_like(carry_ref)

    u = u_ref[0]
    ext = jnp.concatenate([carry_ref[...], u], axis=0)
    carry_ref[...] = u[SEQ_CHUNK - 16:, :]
    pos = c * SEQ_CHUNK + lax.broadcasted_iota(jnp.int32, (SEQ_CHUNK, 1), 0)
    for g, win in enumerate(POOL_WINDOWS):
        cols = slice(g * POOL_GROUP, (g + 1) * POOL_GROUP)
        s = ext[:, cols]
        span = 1
        while span < win:
            s = s + pltpu.roll(s, span, axis=0)
            span *= 2
        cnt = jnp.minimum(pos + 1, win).astype(F32)
        d = s[16:, :] / cnt - u[:, cols]
        y = _dot(d.astype(BF16), w_ref[g]) * s_ref[:, cols]
        o_ref[0, :, cols] = y.astype(o_ref.dtype)


def _pool(proj3, w_mix, scale):
    B, L, _ = proj3.shape
    nc = pl.cdiv(L, SEQ_CHUNK)
    return pl.pallas_call(
        _pool_kernel,
        grid=(B, nc),
        in_specs=[
            pl.BlockSpec((1, SEQ_CHUNK, POOL_WIDTH), lambda b, c: (b, c, 0)),
            pl.BlockSpec((len(POOL_WINDOWS), POOL_GROUP, POOL_GROUP), lambda b, c: (0, 0, 0)),
            pl.BlockSpec((1, POOL_WIDTH), lambda b, c: (0, 0)),
        ],
        out_specs=pl.BlockSpec((1, SEQ_CHUNK, POOL_WIDTH), lambda b, c: (b, c, 0)),
        out_shape=jax.ShapeDtypeStruct((B, L, POOL_WIDTH), BF16),
        scratch_shapes=[pltpu.VMEM((16, POOL_WIDTH), F32)],
        compiler_params=_params(("parallel", "arbitrary")),
        name="pool_mixer",
    )(proj3, w_mix, scale)


def _soft_cap(a):
    return GATE_SOFTCAP * jnp.tanh(a / GATE_SOFTCAP)


def _log_sigmoid(a):
    return jnp.minimum(a, 0.0) - jnp.log1p(jnp.exp(-jnp.abs(a)))


HEADS_PER_STEP = 2


def _mlstm_kernel(seq_len, q_ref, k_ref, v_ref, o_ref, gate_ref, out_ref, c_ref, n_ref, m_ref):
    c = pl.program_id(2)
    Lc = SEQ_CHUNK
    dk, dv = MLSTM_QK_DIM, MLSTM_V_DIM

    @pl.when(c == 0)
    def _():
        c_ref[...] = jnp.zeros_like(c_ref)
        n_ref[...] = jnp.zeros_like(n_ref)
        m_ref[...] = jnp.zeros_like(m_ref)

    row = lax.broadcasted_iota(jnp.int32, (Lc, 1), 0)
    col = lax.broadcasted_iota(jnp.int32, (1, Lc), 1)
    ok_col = (c * Lc + row) < seq_len
    causal = col <= row
    tri = causal.astype(BF16)
    tri_t = (row <= col).astype(BF16)
    lane = lax.broadcasted_iota(jnp.int32, (1, LANES), 1)
    capped = _soft_cap(gate_ref[0])
    gc = jnp.where(ok_col, jnp.where(lane < MLSTM_HEADS, capped, _log_sigmoid(capped)), 0.0)
    gt = gc.T
    for i in range(HEADS_PER_STEP):
        _mlstm_head(pl.program_id(1) * HEADS_PER_STEP + i, ok_col, causal, tri, tri_t, gc, gt,
                    q_ref[0, :, i * dk:(i + 1) * dk], k_ref[0, :, i * dk:(i + 1) * dk],
                    v_ref[0, :, i * dv:(i + 1) * dv], o_ref[0, :, i * dv:(i + 1) * dv],
                    out_ref.at[0, :, i * dv:(i + 1) * dv], c_ref.at[i], n_ref.at[i], m_ref.at[i])


def _mlstm_head(h, ok_col, causal, tri, tri_t, gc, gt, q_in, k_in, v_in, o_in, out_ref, c_ref, n_ref, m_ref):
    Lc = SEQ_CHUNK
    q = jnp.where(ok_col, q_in, 0.0) * (MLSTM_QK_DIM ** -0.5)
    k = jnp.where(ok_col, k_in, 0.0)
    v = jnp.where(ok_col, v_in, 0.0)

    lane = lax.broadcasted_iota(jnp.int32, (1, LANES), 1)
    sub = lax.broadcasted_iota(jnp.int32, (LANES, 1), 0)
    i_c = jnp.sum(jnp.where(lane == h, gc, 0.0), axis=1, keepdims=True)
    f_c = jnp.sum(jnp.where(lane == h + MLSTM_HEADS, gc, 0.0), axis=1, keepdims=True)
    i_r = jnp.sum(jnp.where(sub == h, gt, 0.0), axis=0, keepdims=True)
    f_r = jnp.sum(jnp.where(sub == h + MLSTM_HEADS, gt, 0.0), axis=0, keepdims=True)

    fb_c = jnp.broadcast_to(f_c, (Lc, LANES))
    b_c = sum(_dot(tri, p) for p in _split3(fb_c))[:, 0:1]
    fb_r = jnp.broadcast_to(f_r, (8, Lc))
    b_r8 = sum(_dot(p, tri_t) for p in _split3(fb_r))
    b_r = b_r8[0:1, :]
    g_tot = b_r8[0:1, Lc - 1:Lc]

    m_prev = m_ref[...]
    dlog = jnp.where(causal, b_c - b_r + i_r, -jnp.inf)
    inter_log = b_c + m_prev
    m_out = jnp.maximum(inter_log, jnp.max(dlog, axis=1, keepdims=True))
    wts = jnp.exp(dlog - m_out)
    inter_w = jnp.exp(inter_log - m_out)

    qb = q.astype(BF16)
    kb = k.astype(BF16)
    vb = v.astype(BF16)
    s = lax.dot_general(qb, kb, (((1,), (1,)), ((), ())), preferred_element_type=F32) * wts
    num = _dot(s.astype(BF16), vb) + inter_w * _dot(qb, c_ref[...].astype(BF16))
    den = jnp.sum(s, axis=1, keepdims=True) + inter_w * jnp.sum(q * n_ref[...], axis=1, keepdims=True)
    hh = num / jnp.maximum(jnp.abs(den), jnp.exp(-m_out))
    out_ref[...] = (jax.nn.sigmoid(o_in) * hh).astype(out_ref.dtype)

    a_c = g_tot - b_c + i_c
    m_new = jnp.maximum(g_tot + m_prev, jnp.max(a_c, axis=0, keepdims=True))
    wk = k * jnp.exp(a_c - m_new)
    decay = jnp.exp(g_tot + m_prev - m_new)
    c_ref[...] = decay * c_ref[...] + lax.dot_general(
        wk.astype(BF16), vb, (((0,), (0,)), ((), ())), preferred_element_type=F32)
    n_ref[...] = decay * n_ref[...] + jnp.sum(wk, axis=0, keepdims=True)
    m_ref[...] = m_new


def _mlstm(proj3, gates3):
    B, L, _ = proj3.shape
    nc = pl.cdiv(L, SEQ_CHUNK)
    G = HEADS_PER_STEP
    dk, dv, H = G * MLSTM_QK_DIM, G * MLSTM_V_DIM, MLSTM_HEADS // G
    q0 = POOL_WIDTH // dk
    k0 = q0 + H
    v0 = (POOL_WIDTH + 2 * MLSTM_HEADS * MLSTM_QK_DIM) // dv
    o0 = v0 + H
    return pl.pallas_call(
        functools.partial(_mlstm_kernel, L),
        grid=(B, H, nc),
        in_specs=[
            pl.BlockSpec((1, SEQ_CHUNK, dk), lambda b, h, c: (b, c, q0 + h)),
            pl.BlockSpec((1, SEQ_CHUNK, dk), lambda b, h, c: (b, c, k0 + h)),
            pl.BlockSpec((1, SEQ_CHUNK, dv), lambda b, h, c: (b, c, v0 + h)),
            pl.BlockSpec((1, SEQ_CHUNK, dv), lambda b, h, c: (b, c, o0 + h)),
            pl.BlockSpec((1, SEQ_CHUNK, LANES), lambda b, h, c: (b, c, 0)),
        ],
        out_specs=pl.BlockSpec((1, SEQ_CHUNK, dv), lambda b, h, c: (b, c, h)),
        out_shape=jax.ShapeDtypeStruct((B, L, MLSTM_WIDTH), BF16),
        scratch_shapes=[
            pltpu.VMEM((G, MLSTM_QK_DIM, MLSTM_V_DIM), F32),
            pltpu.VMEM((G, 1, MLSTM_QK_DIM), F32),
            pltpu.VMEM((G, 1, 1), F32),
        ],
        compiler_params=_params(("parallel", "parallel", "arbitrary")),
        name="mlstm",
    )(proj3, proj3, proj3, proj3, gates3)


def _pack_bf16_pair(lo, hi):
    lo_b = lax.bitcast_convert_type(lo.astype(BF16).astype(F32), jnp.uint32)
    hi_b = lax.bitcast_convert_type(hi.astype(BF16).astype(F32), jnp.uint32)
    return (lo_b >> 16) | hi_b


def _unpack_bf16_pair(w):
    lo = lax.bitcast_convert_type(w << 16, F32).astype(BF16)
    hi = lax.bitcast_convert_type(w & jnp.uint32(0xFFFF0000), F32).astype(BF16)
    return lo, hi


def _router_kernel(h_ref, g_ref, wr_ref, br_ref, n_ref, e_ref, p_ref):
    y = _rms(h_ref[...], g_ref[...])
    half = D_MODEL // 2
    n_ref[...] = _pack_bf16_pair(y[:, :half], y[:, half:])
    logits = _dot3_nt(y, wr_ref[...]) + br_ref[...]
    lane = lax.broadcasted_iota(jnp.int32, logits.shape, 1).astype(F32)
    l = jnp.where(lane < N_EXPERTS, logits, -jnp.inf)
    vals, idxs = [], []
    for _ in range(TOP_K):
        m = jnp.max(l, axis=1, keepdims=True)
        idx = jnp.min(jnp.where(l == m, lane, float(LANES)), axis=1, keepdims=True)
        vals.append(m)
        idxs.append(idx)
        l = jnp.where(lane == idx, -jnp.inf, l)
    ex = [jnp.exp(v - vals[0]) for v in vals]
    tot = ex[0] + ex[1] + ex[2] + ex[3]
    e_out = jnp.zeros(logits.shape, F32)
    p_out = jnp.zeros(logits.shape, F32)
    for kk in range(TOP_K):
        e_out = jnp.where(lane == kk, idxs[kk], e_out)
        p_out = jnp.where(lane == kk, ex[kk] / tot, p_out)
    e_ref[...] = e_out.astype(jnp.int32)
    p_ref[...] = p_out


def _router(h, g, wr, br, bm):
    T = h.shape[0]
    return pl.pallas_call(
        _router_kernel,
        grid=(T // bm,),
        in_specs=[
            pl.BlockSpec((bm, D_MODEL), lambda i: (i, 0)),
            pl.BlockSpec((1, D_MODEL), lambda i: (0, 0)),
            pl.BlockSpec((N_EXPERTS, D_MODEL), lambda i: (0, 0)),
            pl.BlockSpec((1, N_EXPERTS), lambda i: (0, 0)),
        ],
        out_specs=[
            pl.BlockSpec((bm, D_MODEL // 2), lambda i: (i, 0)),
            pl.BlockSpec((bm, N_EXPERTS), lambda i: (i, 0)),
            pl.BlockSpec((bm, N_EXPERTS), lambda i: (i, 0)),
        ],
        out_shape=[
            jax.ShapeDtypeStruct((T, D_MODEL // 2), jnp.uint32),
            jax.ShapeDtypeStruct((T, N_EXPERTS), jnp.int32),
            jax.ShapeDtypeStruct((T, N_EXPERTS), F32),
        ],
        compiler_params=_params(("parallel",)),
        name="norm2_router",
    )(h, g, wr, br)


def _experts_kernel(sbe_ref, rb0_ref, nbk_ref, used_ref, tok_ref,
                    x_hbm, wgu_hbm, wd_hbm, bgu_ref, bd_ref,
                    y_hbm,
                    x_res, hmid, wgl_buf, wd_buf, ystage, gsem, wsem, dsem, ysem, ypend, ycnt, gcnt):
    s = pl.program_id(0)
    ns = pl.num_programs(0)
    nb = nbk_ref[s]
    expert = sbe_ref[s]
    nxt = jnp.minimum(s + 1, ns - 1)
    next_active = jnp.logical_and(s + 1 < ns, nbk_ref[nxt] > 0)
    half = D_MODEL // 2
    n_blocks_total = y_hbm.shape[0] // MOE_BLOCK

    def gateup_copies(e, j, slot):
        gate_cols = pl.ds(pl.multiple_of(j * FF_TILE, FF_TILE), FF_TILE)
        lin_cols = pl.ds(pl.multiple_of(D_FF + j * FF_TILE, FF_TILE), FF_TILE)
        return (pltpu.make_async_copy(wgu_hbm.at[e, :, gate_cols], wgl_buf.at[slot, 0], wsem.at[slot]),
                pltpu.make_async_copy(wgu_hbm.at[e, :, lin_cols], wgl_buf.at[slot, 1], wsem.at[slot]))

    def down_copy(e, j, slot):
        cols = pl.ds(pl.multiple_of(j * DOWN_TILE, DOWN_TILE), DOWN_TILE)
        return pltpu.make_async_copy(wd_hbm.at[e, :, cols], wd_buf.at[slot], dsem.at[slot])

    def x_row_copy(tok, r8, rs):
        return pltpu.make_async_copy(x_hbm.at[tok >> 3, pl.ds(tok & 7, 1), :], x_res.at[r8, pl.ds(rs, 1), :], gsem.at[0])

    def x_block_wait():
        g = MOE_BLOCK // SUBLANES
        pltpu.make_async_copy(x_hbm.at[pl.ds(0, g)], x_res.at[pl.ds(0, g)], gsem.at[0]).wait()

    def gather_upto(sb, n_rows):
        base = rb0_ref[sb] * MOE_BLOCK

        def body(r, carry):
            x_row_copy(tok_ref[base + r], r >> 3, r & 7).start()
            return carry
        lax.fori_loop(gcnt[0], n_rows, body, 0)
        gcnt[0] = jnp.maximum(gcnt[0], n_rows)

    def gather_some(sb, n):
        base = rb0_ref[sb] * MOE_BLOCK
        cur = gcnt[0]
        cur8 = cur >> 3
        for i in range(n):
            x_row_copy(tok_ref[base + cur + i], cur8 + i // SUBLANES, i % SUBLANES).start()
        gcnt[0] = cur + n

    def gather_wait():
        n = gcnt[0]
        n_full = n // MOE_BLOCK

        def block_body(r, carry):
            x_block_wait()
            return carry

        def row_body(r, carry):
            x_row_copy(0, 0, 0).wait()
            return carry
        lax.fori_loop(0, n_full, block_body, 0)
        lax.fori_loop(n_full * MOE_BLOCK, n, row_body, 0)
        gcnt[0] = 0

    def y_copy(slot, piece, row0, word0):
        return pltpu.make_async_copy(
            ystage.at[slot, pl.ds(piece * MOE_BLOCK, MOE_BLOCK), :],
            y_hbm.at[pl.ds(pl.multiple_of(row0, MOE_BLOCK), MOE_BLOCK), pl.ds(pl.multiple_of(word0, Y_WORDS), Y_WORDS)],
            ysem.at[slot])

    def y_drain(slot):
        for piece in range(CHUNK_BLOCKS):
            @pl.when(ypend[slot] > piece)
            def _():
                y_copy(slot, 0, 0, 0).wait()
        ypend[slot] = 0

    def y_emit(val, n_pieces, slot, row0, word0):
        ystage[slot, pl.ds(0, n_pieces * MOE_BLOCK), :] = _pack_bf16_pair(val[:, :Y_WORDS], val[:, Y_WORDS:])
        for piece in range(n_pieces):
            y_copy(slot, piece, row0 + piece * MOE_BLOCK, word0).start()
        ypend[slot] = n_pieces

    def for_chunks(n_blocks, fn):
        done = 0
        for pos, k in enumerate(CHUNK_SIZES):
            if pos == 0:
                n_big = n_blocks // k

                def body(c, carry, k=k):
                    fn(c * k, k)
                    return carry
                lax.fori_loop(0, n_big, body, 0)
                done = n_big * k
            else:
                fits = n_blocks - done >= k

                @pl.when(fits)
                def _(k=k, done=done):
                    fn(done, k)
                done = jnp.where(fits, done + k, done)

    @pl.when(s == 0)
    def _():
        ypend[0] = 0
        ypend[1] = 0
        ycnt[0] = 0
        gcnt[0] = 0

        @pl.when(nb > 0)
        def _():
            for c in gateup_copies(expert, 0, 0):
                c.start(priority=WEIGHT_DMA_THREAD)
        gather_upto(0, nb * MOE_BLOCK)

    @pl.when(nb > 0)
    def _():
        gather_wait()
        row_base = rb0_ref[s] * MOE_BLOCK

        def gateup_tile(j, carry):
            slot = lax.rem(j, 2)
            for c in gateup_copies(expert, j, slot):
                c.wait()

            @pl.when(j + 1 < N_FF_TILES)
            def _():
                for c in gateup_copies(expert, j + 1, 1 - slot):
                    c.start(priority=WEIGHT_DMA_THREAD)

            @pl.when(j + 1 == N_FF_TILES)
            def _():
                down_copy(expert, 0, 0).start(priority=WEIGHT_DMA_THREAD)

            bg = bgu_ref[0, pl.ds(j, 1), :]
            bl = bgu_ref[0, pl.ds(N_FF_TILES + j, 1), :]

            def chunk(b0, k):
                rows = pl.ds(pl.multiple_of(b0 * MOE_BLOCK, MOE_BLOCK), k * MOE_BLOCK)
                groups = pl.ds(pl.multiple_of(b0 * (MOE_BLOCK // SUBLANES), MOE_BLOCK // SUBLANES),
                               k * (MOE_BLOCK // SUBLANES))
                x_lo, x_hi = _unpack_bf16_pair(x_res[groups].reshape(k * MOE_BLOCK, half))
                gate = _dotw(x_lo, wgl_buf[slot, 0, :half, :]) + _dotw(x_hi, wgl_buf[slot, 0, half:, :]) + bg
                lin = _dotw(x_lo, wgl_buf[slot, 1, :half, :]) + _dotw(x_hi, wgl_buf[slot, 1, half:, :]) + bl
                gate = jnp.minimum(gate, SWIGLU_LIMIT)
                lin = jnp.clip(lin, -SWIGLU_LIMIT, SWIGLU_LIMIT)
                hmid[j, rows, :] = (gate * jax.nn.sigmoid(SWIGLU_ALPHA * gate) * (lin + 1.0)).astype(BF16)
            for_chunks(nb, chunk)
            return carry
        lax.fori_loop(0, N_FF_TILES, gateup_tile, 0)

        def down_tile(fetch_next, j, carry):
            slot = lax.rem(j, 2)
            down_copy(expert, j, slot).wait()

            @pl.when(j + 1 < N_DOWN_TILES)
            def _():
                down_copy(expert, j + 1, 1 - slot).start(priority=WEIGHT_DMA_THREAD)

            @pl.when(jnp.logical_and(j + 1 == N_DOWN_TILES, next_active))
            def _():
                for c in gateup_copies(sbe_ref[nxt], 0, 0):
                    c.start(priority=WEIGHT_DMA_THREAD)

            bd = bd_ref[0, pl.ds(j, 1), :]
            word0 = j * Y_WORDS

            def chunk(b0, k):
                yslot = lax.rem(ycnt[0], 2)
                ycnt[0] = ycnt[0] + 1
                y_drain(yslot)
                if fetch_next:
                    gather_some(nxt, k * GATHER_PER_VISIT)
                rows = pl.ds(pl.multiple_of(b0 * MOE_BLOCK, MOE_BLOCK), k * MOE_BLOCK)
                h = jnp.concatenate([hmid[jj, rows, :] for jj in range(N_FF_TILES)], axis=1)
                acc = bd + _dotw(h, wd_buf[slot])
                y_emit(acc, k, yslot, row_base + b0 * MOE_BLOCK, word0)
            for_chunks(nb, chunk)
            return carry
        lax.fori_loop(0, GATHER_TILES, functools.partial(down_tile, True), 0)
        lax.fori_loop(GATHER_TILES, N_DOWN_TILES, functools.partial(down_tile, False), 0)
        gather_upto(nxt, jnp.where(s + 1 < ns, nbk_ref[nxt] * MOE_BLOCK, 0))

    @pl.when(s == ns - 1)
    def _():
        gather_wait()
        y_drain(0)
        y_drain(1)
        ystage[0, pl.ds(0, MOE_BLOCK), :] = jnp.zeros((MOE_BLOCK, Y_WORDS), jnp.uint32)

        def zero_copy(b, j):
            return y_copy(0, 0, b * MOE_BLOCK, j * Y_WORDS)

        def start_body(b, carry):
            for j in range(N_DOWN_TILES):
                zero_copy(b, j).start()
            return carry

        def wait_body(b, carry):
            for j in range(N_DOWN_TILES):
                zero_copy(0, 0).wait()
            return carry
        lax.fori_loop(used_ref[0], n_blocks_total, start_body, 0)
        lax.fori_loop(used_ref[0], n_blocks_total, wait_body, 0)


def _experts(sb_e, sb_rb0, sb_nb, n_used, row_tok, n2p, w_gu, b_gu, w_down, b_down):
    R = row_tok.shape[0] - SB_BLOCKS * MOE_BLOCK
    S = sb_e.shape[0]
    sb_rows = SB_BLOCKS * MOE_BLOCK

    grid_spec = pltpu.PrefetchScalarGridSpec(
        num_scalar_prefetch=5,
        grid=(S,),
        in_specs=[
            pl.BlockSpec(memory_space=pl.ANY),
            pl.BlockSpec(memory_space=pl.ANY),
            pl.BlockSpec(memory_space=pl.ANY),
            pl.BlockSpec((1, 2 * N_FF_TILES, FF_TILE), lambda s, e, r0, nbk, u, tok: (e[s], 0, 0)),
            pl.BlockSpec((1, N_DOWN_TILES, DOWN_TILE), lambda s, e, r0, nbk, u, tok: (e[s], 0, 0)),
        ],
        out_specs=pl.BlockSpec(memory_space=pl.ANY),
        scratch_shapes=[
            pltpu.VMEM((sb_rows // SUBLANES, SUBLANES, D_MODEL // 2), jnp.uint32),
            pltpu.VMEM((N_FF_TILES, sb_rows, FF_TILE), BF16),
            pltpu.VMEM((2, 2, D_MODEL, FF_TILE), F32),
            pltpu.VMEM((2, D_FF, DOWN_TILE), F32),
            pltpu.VMEM((2, CHUNK_BLOCKS * MOE_BLOCK, Y_WORDS), jnp.uint32),
            pltpu.SemaphoreType.DMA((1,)),
            pltpu.SemaphoreType.DMA((2,)),
            pltpu.SemaphoreType.DMA((2,)),
            pltpu.SemaphoreType.DMA((2,)),
            pltpu.SMEM((2,), jnp.int32),
            pltpu.SMEM((1,), jnp.int32),
            pltpu.SMEM((1,), jnp.int32),
        ],
    )
    return pl.pallas_call(
        _experts_kernel,
        grid_spec=grid_spec,
        out_shape=jax.ShapeDtypeStruct((R, D_MODEL // 2), jnp.uint32),
        compiler_params=_params(("arbitrary",)),
        name="moe_experts",
    )(sb_e, sb_rb0, sb_nb, n_used, row_tok, n2p.reshape(-1, SUBLANES, D_MODEL // 2), w_gu, w_down,
      b_gu.reshape(N_EXPERTS, 2 * N_FF_TILES, FF_TILE), b_down.reshape(N_EXPERTS, N_DOWN_TILES, DOWN_TILE))


def _combine_kernel(tb, dest_ref, piece_ref, y_hbm, h_ref, p_ref, g_ref, out_hbm, buf0, buf1, ostage, sem, osem,
                    opend):
    i = pl.program_id(0)
    nb = pl.num_programs(0)
    rows = tb * TOP_K
    n_pieces = tb // N_META
    bufs = (buf0, buf1)

    def gather_start(step, slot, first, n):
        for r in range(first, first + n):
            row = dest_ref[step * rows + r]
            pltpu.make_async_copy(y_hbm.at[pl.ds(row, 1), :], bufs[slot].at[pl.ds(r, 1), :],
                                  sem.at[slot]).start(priority=r % 2)

    def gather_wait(slot):
        pltpu.make_async_copy(y_hbm.at[pl.ds(0, rows), :], bufs[slot], sem.at[slot]).wait()

    def out_copy(slot, piece, dst):
        return pltpu.make_async_copy(ostage.at[slot, pl.ds(piece * N_META, N_META), :],
                                     out_hbm.at[pl.ds(pl.multiple_of(dst, N_META), N_META), :], osem.at[slot])

    def out_drain(slot):
        for piece in range(n_pieces):
            @pl.when(opend[slot] > piece)
            def _():
                out_copy(slot, 0, 0).wait()
        opend[slot] = 0

    @pl.when(i == 0)
    def _():
        opend[0] = 0
        opend[1] = 0
        gather_start(0, 0, 0, rows)

    def step(slot):
        gather_wait(slot)
        nxt = jnp.minimum(i + 1, nb - 1)
        acc = h_ref[...]
        p = p_ref[...]
        for kk in range(TOP_K):
            gather_start(nxt, 1 - slot, kk * tb, tb)
            w = bufs[slot][pl.ds(kk * tb, tb), :]
            lo = lax.bitcast_convert_type(w << 16, F32)
            hi = lax.bitcast_convert_type(w & jnp.uint32(0xFFFF0000), F32)
            pieces = []
            for j in range(N_DOWN_TILES):
                pieces += [lo[:, j * Y_WORDS:(j + 1) * Y_WORDS], hi[:, j * Y_WORDS:(j + 1) * Y_WORDS]]
            acc = acc + p[:, kk:kk + 1] * jnp.concatenate(pieces, axis=1)
        out_drain(slot)
        ostage[slot] = _rms(acc, g_ref[...])
        started = 0
        for piece in range(n_pieces):
            dst = piece_ref[i * n_pieces + piece]

            @pl.when(dst >= 0)
            def _():
                out_copy(slot, piece, dst).start()
            started = started + (dst >= 0).astype(jnp.int32)
        opend[slot] = started

    for slot in range(2):
        pl.when(lax.rem(i, 2) == slot)(functools.partial(step, slot))

    @pl.when(i == nb - 1)
    def _():
        for slot in range(2):
            @pl.when(lax.rem(i, 2) == slot)
            def _():
                gather_wait(1 - slot)
        out_drain(0)
        out_drain(1)


def _combine(dest_km, pieces, y, h, probs, g, tb, n_out):
    T = h.shape[0]
    grid_spec = pltpu.PrefetchScalarGridSpec(
        num_scalar_prefetch=2,
        grid=(T // tb,),
        in_specs=[
            pl.BlockSpec(memory_space=pl.ANY),
            pl.BlockSpec((tb, D_MODEL), lambda i, d, q: (i, 0)),
            pl.BlockSpec((tb, N_EXPERTS), lambda i, d, q: (i, 0)),
            pl.BlockSpec((1, D_MODEL), lambda i, d, q: (0, 0)),
        ],
        out_specs=pl.BlockSpec(memory_space=pl.ANY),
        scratch_shapes=[
            pltpu.VMEM((tb * TOP_K, D_MODEL // 2), jnp.uint32),
            pltpu.VMEM((tb * TOP_K, D_MODEL // 2), jnp.uint32),
            pltpu.VMEM((2, tb, D_MODEL), F32),
            pltpu.SemaphoreType.DMA((2,)),
            pltpu.SemaphoreType.DMA((2,)),
            pltpu.SMEM((2,), jnp.int32),
        ],
    )
    return pl.pallas_call(
        functools.partial(_combine_kernel, tb),
        grid_spec=grid_spec,
        out_shape=jax.ShapeDtypeStruct((n_out, D_MODEL), F32),
        compiler_params=_params(("arbitrary",)),
        name="moe_combine_norm",
    )(dest_km, pieces, y, h, probs, g)


def _routing(top_e, tb):
    T = top_e.shape[0]
    A = T * TOP_K
    n_blocks = -(-(A + N_EXPERTS * (MOE_BLOCK - 1)) // MOE_BLOCK)
    R = n_blocks * MOE_BLOCK
    n_sb = N_EXPERTS + n_blocks // SB_BLOCKS
    i32 = jnp.int32
    e_flat = top_e.reshape(A)
    onehot = (e_flat[:, None] == jnp.arange(N_EXPERTS, dtype=i32)[None, :]).astype(i32)
    csum = jnp.cumsum(onehot, axis=0)
    rank = jnp.sum(csum * onehot, axis=1) - 1
    counts = csum[-1]
    blocks = (counts + MOE_BLOCK - 1) // MOE_BLOCK
    blk_end = jnp.cumsum(blocks)
    blk_start = blk_end - blocks
    dest = (blk_start * MOE_BLOCK)[e_flat] + rank
    row_tok = jnp.zeros((R + SB_BLOCKS * MOE_BLOCK,), i32).at[dest].set(
        jnp.arange(A, dtype=i32) // TOP_K, unique_indices=True)
    dest_km = dest.reshape(T // tb, tb, TOP_K).transpose(0, 2, 1).reshape(A).astype(i32)

    sbs = (blocks + SB_BLOCKS - 1) // SB_BLOCKS
    sb_end = jnp.cumsum(sbs)
    sb_start = sb_end - sbs
    s_idx = jnp.arange(n_sb, dtype=i32)
    active = s_idx < sb_end[-1]
    e_s = jnp.minimum(jnp.sum((sb_end[None, :] <= s_idx[:, None]).astype(i32), axis=1), N_EXPERTS - 1)
    k_s = s_idx - sb_start[e_s]
    sb_rb0 = jnp.where(active, blk_start[e_s] + k_s * SB_BLOCKS, 0)
    sb_nb = jnp.where(active, jnp.minimum(blocks[e_s] - k_s * SB_BLOCKS, SB_BLOCKS), 0)
    sb_e = jnp.where(active, e_s, jnp.max(jnp.where(active, e_s, 0)))
    n_used = blk_end[-1:].astype(i32)
    return row_tok, dest_km, sb_e.astype(i32), sb_rb0.astype(i32), sb_nb.astype(i32), n_used


def kernel(x, meta_tokens, norm1_g, w_in, b_igate, b_fgate, w_pool_mix, pool_scale, w_out, norm2_g, w_router,
           b_router, w_gu, b_gu, w_down, b_down, norm_f_g):
    B, S, D = x.shape
    L = N_META + S
    T = B * L
    H = MLSTM_HEADS
    BM_IN = 688
    BM_OUT = 688
    BM_NORM1 = 688
    BM_NORM = 688
    TB = 192
    assert T % BM_IN == 0 and T % BM_OUT == 0 and T % BM_NORM1 == 0 and T % BM_NORM == 0 and T % TB == 0
    assert w_in.shape[0] == 1
    assert L % N_META == 0 and TB % N_META == 0

    pieces = jnp.asarray(_piece_table(B, S))
    x2 = x.reshape(B * S, D)
    meta = meta_tokens.astype(x.dtype)

    l = 0
    w_in_t = w_in.reshape(D, PROJ_COLS + 2 * H).T
    bg = jnp.pad(jnp.concatenate([b_igate[l], b_fgate[l]]), (0, LANES - 2 * H)).reshape(1, LANES)
    n1, gates = _norm1(x2, meta, B, norm1_g[l].reshape(1, D), w_in_t, bg, BM_NORM1)

    proj3 = _inproj(n1, w_in_t, PROJ_COLS, BM_IN, 1024).reshape(B, L, PROJ_COLS)
    pool_out = _pool(proj3, w_pool_mix[l].astype(BF16), pool_scale[l].reshape(1, POOL_WIDTH))
    mlstm_out = _mlstm(proj3, gates.reshape(B, L, LANES))

    h1 = _outproj(pool_out.reshape(T, POOL_WIDTH), mlstm_out.reshape(T, MLSTM_WIDTH),
                  w_out.reshape(D, D), x2, meta, B, BM_OUT, 512)

    n2p, top_e, probs = _router(h1, norm2_g[l].reshape(1, D), w_router.reshape(D, N_EXPERTS).T,
                                b_router.reshape(1, N_EXPERTS), BM_NORM)

    row_tok, dest_km, sb_e, sb_rb0, sb_nb, n_used = _routing(top_e[:, :TOP_K], TB)
    y = _experts(sb_e, sb_rb0, sb_nb, n_used, row_tok, n2p,
                 w_gu.reshape(N_EXPERTS, D, 2 * D_FF), b_gu.reshape(N_EXPERTS, 1, 2 * D_FF),
                 w_down.reshape(N_EXPERTS, D_FF, D), b_down.reshape(N_EXPERTS, 1, D))
    out = _combine(dest_km, pieces, y, h1, probs, norm_f_g.reshape(1, D), TB, B * S)
    return out.reshape(B, S, D)
```
